```python
import jax, jax.numpy as jnp
from jax import lax
import numpy as np

D_MODEL = 2048
BATCH = 16
SEQ = 256
DEPTH = 2
DEC_BATCH = 4
DEC_SEQ = 4096
PAST_LEN = 512

GRID_W = 64
HEAD_DIM = 128
A_HEADS = 8
NA_ROWS = 8
NA_COLS = 16
NA_QBLOCK = 16
NA_KBLOCK = 32
B_HEADS = 8
B_KV_HEADS = 2
B_WINDOW = 128
B_BLOCK = 128
ROPE_THETA = 10000.0
C_HEADS = 8
C_QK_DIM = 128
C_V_DIM = 256
C_CHUNK = 128
N_EXPERTS = 16
N_GROUPS = 4
EXPERTS_PER_GROUP = N_EXPERTS // N_GROUPS
TOP_K = 2
D_FF_EXPERT = 1024
MOE_BLOCK = 128
EPS = 1e-6

N_EVEN = (DEPTH + 1) // 2
N_ODD = DEPTH // 2
A_WIDTH = A_HEADS * HEAD_DIM
B_Q_WIDTH = B_HEADS * HEAD_DIM
B_KV_WIDTH = B_KV_HEADS * HEAD_DIM
AB_IN = 3 * A_WIDTH + B_Q_WIDTH + 2 * B_KV_WIDTH
AB_OUT = A_WIDTH + B_Q_WIDTH
C_QK_WIDTH = C_HEADS * C_QK_DIM
C_V_WIDTH = C_HEADS * C_V_DIM
C_IN = 2 * C_QK_WIDTH + 2 * C_V_WIDTH + 4 * C_HEADS

kernel_name = "hybrid_diffusion_natten_swa_mlstm_moe_step"

F32 = jnp.float32


def rmsnorm(x, g):
    xf = x.astype(F32)
    y = xf * lax.rsqrt(jnp.mean(xf * xf, axis=-1, keepdims=True) + EPS)
    return (y * g.astype(F32)).astype(x.dtype)


def split_cols(z, sizes):
    return jnp.split(z, [int(i) for i in np.cumsum(sizes)[:-1]], axis=-1)


def ada_mod(cvec, w, b):
    m = jax.nn.silu(cvec) @ w + b
    return [t[:, None, :] for t in jnp.split(m, 6, axis=-1)]


def axial_rope(x):
    T = x.shape[1]
    t = jnp.arange(T)
    row = (t // GRID_W).astype(F32)
    col = (t % GRID_W).astype(F32)
    nf = HEAD_DIM // 4
    freqs = ROPE_THETA ** (-jnp.arange(nf, dtype=F32) / nf)

    def rot(xh, pos):
        ang = pos[:, None] * freqs
        cos = jnp.cos(ang)[None, :, None, :]
        sin = jnp.sin(ang)[None, :, None, :]
        x1, x2 = xh[..., :nf], xh[..., nf:]
        return jnp.concatenate([x1 * cos - x2 * sin, x2 * cos + x1 * sin], axis=-1)

    xf = x.astype(F32)
    half = HEAD_DIM // 2
    return jnp.concatenate([rot(xf[..., :half], row), rot(xf[..., half:], col)], axis=-1).astype(x.dtype)


def dense_attn(q, k, v):
    s = jnp.einsum('bqhd,bkhd->bhqk', q, k).astype(F32) * (HEAD_DIM ** -0.5)
    p = jax.nn.softmax(s, axis=-1).astype(v.dtype)
    return jnp.einsum('bhqk,bkhd->bqhd', p, v)


def dense_gqa_sink(q, k, v, sink):
    B, S, HB, d = q.shape
    G = HB // B_KV_HEADS
    qg = q.reshape(B, S, B_KV_HEADS, G, d)
    s = jnp.einsum('bqhgd,bkhd->bhgqk', qg, k).astype(F32) * (HEAD_DIM ** -0.5)
    sk = jnp.broadcast_to(sink.astype(F32).reshape(B_KV_HEADS, G)[None, :, :, None, None], s.shape[:-1] + (1,))
    p = jax.nn.softmax(jnp.concatenate([s, sk], axis=-1), axis=-1)[..., :-1].astype(v.dtype)
    return jnp.einsum('bhgqk,bkhd->bqhgd', p, v).reshape(B, S, HB, d)


def neighbourhood_attn(q, k, v, kc, vc, rel_bias):
    B, T, H, d = q.shape
    rows = T // GRID_W
    kh = min(NA_ROWS, rows)
    ncb = GRID_W // NA_QBLOCK
    qcols = np.arange(GRID_W).reshape(ncb, NA_QBLOCK)
    kstart = np.clip(np.arange(ncb) * NA_QBLOCK - NA_COLS // 2, 0, GRID_W - NA_KBLOCK)
    kcols = kstart[:, None] + np.arange(NA_KBLOCK)
    cstart = np.clip(qcols - NA_COLS // 2, 0, GRID_W - NA_COLS)
    col_ok = (kcols[:, None, :] >= cstart[:, :, None]) & (kcols[:, None, :] < cstart[:, :, None] + NA_COLS)
    col_idx = np.clip(kcols[:, None, :] - qcols[:, :, None] + NA_COLS - 1, 0, 2 * NA_COLS - 2)
    mask = np.broadcast_to(col_ok[:, :, None, :], (ncb, NA_QBLOCK, kh, NA_KBLOCK)).reshape(ncb, NA_QBLOCK, kh * NA_KBLOCK)
    col_bias = rel_bias[:, :, col_idx]
    nlat = kh * NA_KBLOCK
    qg = (q * (HEAD_DIM ** -0.5)).reshape(B, rows, GRID_W, H, d)
    kg = k.reshape(B, rows, GRID_W, H, d)
    vg = v.reshape(B, rows, GRID_W, H, d)

    def row_block(r):
        rs = jnp.clip(r - kh // 2, 0, rows - kh)
        qr = lax.dynamic_index_in_dim(qg, r, axis=1, keepdims=False).reshape(B, ncb, NA_QBLOCK, H, d)
        kr = lax.dynamic_slice_in_dim(kg, rs, kh, axis=1)[:, :, kcols]
        vr = lax.dynamic_slice_in_dim(vg, rs, kh, axis=1)[:, :, kcols]
        kb = kr.transpose(0, 2, 1, 3, 4, 5).reshape(B, ncb, nlat, H, d)
        vb = vr.transpose(0, 2, 1, 3, 4, 5).reshape(B, ncb, nlat, H, d)
        ridx = rs + jnp.arange(kh) - r + (NA_ROWS - 1)
        bias = col_bias[:, ridx].transpose(0, 2, 3, 1, 4).reshape(H, ncb, NA_QBLOCK, nlat)
        s_lat = jnp.einsum('bnqhd,bnkhd->bhnqk', qr, kb).astype(F32) + bias.astype(F32)[None]
        s_lat = jnp.where(mask, s_lat, -jnp.inf)
        s_ctx = jnp.einsum('bnqhd,bphd->bhnqp', qr, kc).astype(F32)
        p = jax.nn.softmax(jnp.concatenate([s_lat, s_ctx], axis=-1), axis=-1).astype(v.dtype)
        out = (jnp.einsum('bhnqk,bnkhd->bnqhd', p[..., :nlat], vb)
               + jnp.einsum('bhnqp,bphd->bnqhd', p[..., nlat:], vc))
        return out.reshape(B, GRID_W, H, d)

    out = lax.map(row_block, jnp.arange(rows))
    return out.transpose(1, 0, 2, 3, 4).reshape(B, T, H, d)


def banded_gqa_sink(q, k, v, kc, vc, sink):
    B, T, HB, d = q.shape
    G = HB // B_KV_HEADS
    nb = T // B_BLOCK
    P = kc.shape[1]
    qb = (q * (HEAD_DIM ** -0.5)).reshape(B, nb, B_BLOCK, B_KV_HEADS, G, d)

    def neigh(x):
        xp = jnp.pad(x, ((0, 0), (B_BLOCK, B_BLOCK), (0, 0), (0, 0))).reshape(B, nb + 2, B_BLOCK, B_KV_HEADS, d)
        return jnp.concatenate([xp[:, :-2], xp[:, 1:-1], xp[:, 2:]], axis=2)

    kb, vb = neigh(k), neigh(v)
    qpos = np.arange(nb)[:, None] * B_BLOCK + np.arange(B_BLOCK)
    kpos = (np.arange(nb)[:, None] - 1) * B_BLOCK + np.arange(3 * B_BLOCK)
    ok = ((np.abs(qpos[:, :, None] - kpos[:, None, :]) <= B_WINDOW)
          & (kpos[:, None, :] >= 0) & (kpos[:, None, :] < T))
    s_lat = jnp.einsum('bnqhgd,bnkhd->bnhgqk', qb, kb).astype(F32)
    s_lat = jnp.where(ok[None, :, None, None], s_lat, -jnp.inf)
    s_ctx = jnp.einsum('bnqhgd,bphd->bnhgqp', qb, kc).astype(F32)
    s_sink = jnp.broadcast_to(sink.astype(F32).reshape(B_KV_HEADS, G)[None, None, :, :, None, None], s_lat.shape[:-1] + (1,))
    p = jax.nn.softmax(jnp.concatenate([s_lat, s_ctx, s_sink], axis=-1), axis=-1).astype(v.dtype)
    L = 3 * B_BLOCK
    out = (jnp.einsum('bnhgqk,bnkhd->bnqhgd', p[..., :L], vb)
           + jnp.einsum('bnhgqp,bphd->bnqhgd', p[..., L:L + P], vc))
    return out.reshape(B, T, HB, d)


def ab_project(h, w_in):
    B, T, _ = h.shape
    qa, ka, va, qb, kb, vb = split_cols(h @ w_in, [A_WIDTH, A_WIDTH, A_WIDTH, B_Q_WIDTH, B_KV_WIDTH, B_KV_WIDTH])
    heads = lambda z, n: z.reshape(B, T, n, HEAD_DIM)
    return (heads(qa, A_HEADS), heads(ka, A_HEADS), heads(va, A_HEADS),
            heads(qb, B_HEADS), heads(kb, B_KV_HEADS), heads(vb, B_KV_HEADS))


def ab_merge(oa, ob, w_out):
    B, T = oa.shape[:2]
    return jnp.concatenate([oa.reshape(B, T, A_WIDTH), ob.reshape(B, T, B_Q_WIDTH)], axis=-1) @ w_out


def mlstm_project(h, w_in, b_gates):
    B, T, _ = h.shape
    q, k, v, o, g = split_cols(h @ w_in, [C_QK_WIDTH, C_QK_WIDTH, C_V_WIDTH, C_V_WIDTH, 4 * C_HEADS])
    q = q.reshape(B, T, C_HEADS, C_QK_DIM) * (C_QK_DIM ** -0.5)
    k = k.reshape(B, T, C_HEADS, C_QK_DIM)
    v = v.reshape(B, T, C_HEADS, C_V_DIM)
    g = g.reshape(B, T, 4, C_HEADS).astype(F32) + b_gates.astype(F32)
    return q, k, v, o, g


def mlstm_chunkwise(q, k, v, ig, lf, C0, n0, m0):
    B, T, H, _ = q.shape
    DV = v.shape[-1]
    L = C_CHUNK
    nc = T // L

    def chunks(x):
        return jnp.moveaxis(x.astype(F32).reshape((B, nc, L) + x.shape[2:]), 1, 0)

    tril = np.tril(np.ones((L, L), dtype=bool))

    def step(carry, inp):
        C, n, m = carry
        qc, kc, vc, ic, fc = inp
        ic = ic.transpose(0, 2, 1)
        b = jnp.cumsum(fc.transpose(0, 2, 1), axis=-1)
        logw = jnp.where(tril, b[..., :, None] - b[..., None, :] + ic[..., None, :], -jnp.inf)
        inter = b + m[..., None]
        mt = jnp.maximum(inter, jnp.max(logw, axis=-1))
        w = jnp.exp(logw - mt[..., None])
        wi = jnp.exp(inter - mt)
        s = jnp.einsum('blhd,bshd->bhls', qc, kc) * w
        num = jnp.einsum('bhls,bshv->blhv', s, vc) + jnp.einsum('bhl,blhd,bhdv->blhv', wi, qc, C)
        den = jnp.sum(s, axis=-1) + wi * jnp.einsum('blhd,bhd->bhl', qc, n)
        hc = num / jnp.maximum(jnp.abs(den), jnp.exp(-mt)).transpose(0, 2, 1)[..., None]
        bl = b[..., -1]
        gl = bl[..., None] - b + ic
        m_new = jnp.maximum(bl + m, jnp.max(gl, axis=-1))
        ws = jnp.exp(gl - m_new[..., None])
        wc = jnp.exp(bl + m - m_new)
        C_new = wc[..., None, None] * C + jnp.einsum('bhs,bshd,bshv->bhdv', ws, kc, vc)
        n_new = wc[..., None] * n + jnp.einsum('bhs,bshd->bhd', ws, kc)
        return (C_new, n_new, m_new), hc

    init = (C0.astype(F32), n0.astype(F32), m0.astype(F32))
    (C, n, m), hs = lax.scan(step, init, (chunks(q), chunks(k), chunks(v), chunks(ig), chunks(lf)))
    return jnp.moveaxis(hs, 0, 1).reshape(B, T, H, DV), C, n, m


def mlstm_bidir(q, k, v, g, st_fwd, st_bwd):
    flip = lambda x: jnp.flip(x, axis=1)
    lsig = jax.nn.log_sigmoid
    h_f, Cf, nf, mf = mlstm_chunkwise(q, k, v, g[:, :, 0], lsig(g[:, :, 1]), *st_fwd)
    h_b, Cb, nb, mb = mlstm_chunkwise(flip(q), flip(k), flip(v), flip(g[:, :, 2]), flip(lsig(g[:, :, 3])), *st_bwd)
    return h_f + flip(h_b), (Cf, nf, mf), (Cb, nb, mb)


def mlstm_out(hsum, o, norm_g, w_out):
    B, T = hsum.shape[:2]
    hn = rmsnorm(hsum, norm_g.reshape(C_HEADS, C_V_DIM)).reshape(B, T, C_V_WIDTH).astype(o.dtype)
    return (jax.nn.sigmoid(o) * hn) @ w_out


def moe(h, w_router, b_router, wg, wu, wd):
    n, d = h.shape
    aff = jax.nn.sigmoid((h @ w_router).astype(F32))
    sel = (aff + b_router.astype(F32)).reshape(n, N_GROUPS, EXPERTS_PER_GROUP)
    grp = jnp.argmax(lax.top_k(sel, TOP_K)[0].sum(-1), axis=-1)
    in_grp = jnp.take_along_axis(sel, grp[:, None, None], axis=1)[:, 0]
    expert = grp[:, None] * EXPERTS_PER_GROUP + lax.top_k(in_grp, TOP_K)[1]
    wsel = jnp.take_along_axis(aff, expert, axis=1)
    wsel = wsel / jnp.sum(wsel, axis=-1, keepdims=True)
    n_assign = n * TOP_K
    flat_e = expert.reshape(n_assign)
    flat_tok = jnp.repeat(jnp.arange(n), TOP_K)
    order = jnp.argsort(flat_e)
    se, stok, sw = flat_e[order], flat_tok[order], wsel.reshape(n_assign)[order]
    counts = jnp.bincount(flat_e, length=N_EXPERTS)
    padded = (counts + MOE_BLOCK - 1) // MOE_BLOCK * MOE_BLOCK
    pend = jnp.cumsum(padded)
    dest = (pend - padded)[se] + jnp.arange(n_assign) - (jnp.cumsum(counts) - counts)[se]
    n_blocks = -(-n_assign // MOE_BLOCK) + N_EXPERTS
    xbuf = jnp.zeros((n_blocks * MOE_BLOCK, d), h.dtype).at[dest].set(h[stok])
    block_e = jnp.minimum(jnp.searchsorted(pend, jnp.arange(n_blocks) * MOE_BLOCK, side='right'), N_EXPERTS - 1)

    def expert_block(args):
        xb, e = args
        return (jax.nn.silu(xb @ wg[e]) * (xb @ wu[e])) @ wd[e]

    ybuf = lax.map(expert_block, (xbuf.reshape(n_blocks, MOE_BLOCK, d), block_e)).reshape(-1, d)
    contrib = ybuf[dest] * sw[:, None].astype(h.dtype)
    return jax.ops.segment_sum(contrib, stok, num_segments=n)


def channel_mixer(x, g, shift, scale, gate, w_router, b_router, wg, wu, wd):
    h = rmsnorm(x, g) * (1 + scale) + shift
    y = moe(h.reshape(-1, x.shape[-1]), w_router, b_router, wg, wu, wd)
    return x + gate * y.reshape(x.shape)


def setup_inputs(seed: int = 0) -> dict:
    key = jax.random.key(seed)
    ks = jax.random.split(key, 32)
    D = D_MODEL

    def nrm(k, shape, s):
        return jax.random.normal(k, shape, F32) * s

    return {
        "x_prompt": nrm(ks[0], (BATCH, SEQ, D), 1.0),
        "x_sample": nrm(ks[1], (DEC_BATCH, DEC_SEQ, D), 1.0),
        "c": nrm(ks[2], (DEC_BATCH, D), 1.0),
        "cache_a_k": nrm(ks[3], (DEC_BATCH, N_EVEN, PAST_LEN, A_HEADS, HEAD_DIM), 1.0),
        "cache_a_v": nrm(ks[4], (DEC_BATCH, N_EVEN, PAST_LEN, A_HEADS, HEAD_DIM), 1.0),
        "cache_b_k": nrm(ks[5], (DEC_BATCH, N_EVEN, PAST_LEN, B_KV_HEADS, HEAD_DIM), 1.0),
        "cache_b_v": nrm(ks[6], (DEC_BATCH, N_EVEN, PAST_LEN, B_KV_HEADS, HEAD_DIM), 1.0),
        "state_C": nrm(ks[7], (DEC_BATCH, N_ODD, 2, C_HEADS, C_QK_DIM, C_V_DIM), 0.1),
        "state_n": nrm(ks[8], (DEC_BATCH, N_ODD, 2, C_HEADS, C_QK_DIM), 0.5),
        "state_m": nrm(ks[9], (DEC_BATCH, N_ODD, 2, C_HEADS), 0.5),
        "c_ctx": nrm(ks[10], (D,), 1.0),
        "norm1_g": 1.0 + nrm(ks[11], (DEPTH, D), 0.02),
        "norm2_g": 1.0 + nrm(ks[12], (DEPTH, D), 0.02),
        "w_ada": nrm(ks[13], (DEPTH, D, 6 * D), 0.5 * D ** -0.5),
        "b_ada": nrm(ks[14], (DEPTH, 6 * D), 0.02),
        "w_in_ab": nrm(ks[15], (N_EVEN, D, AB_IN), D ** -0.5),
        "w_out_ab": nrm(ks[16], (N_EVEN, AB_OUT, D), AB_OUT ** -0.5),
        "rel_bias_a": nrm(ks[17], (N_EVEN, A_HEADS, 2 * NA_ROWS - 1, 2 * NA_COLS - 1), 0.1),
        "sink_b": nrm(ks[18], (N_EVEN, B_HEADS), 0.5),
        "w_in_c": nrm(ks[19], (N_ODD, D, C_IN), D ** -0.5),
        "b_gates_c": jnp.array([0.0, 3.0, 0.0, 3.0], F32)[None, :, None] + nrm(ks[20], (N_ODD, 4, C_HEADS), 0.5),
        "norm_c_g": 1.0 + nrm(ks[21], (N_ODD, C_V_WIDTH), 0.02),
        "w_out_c": nrm(ks[22], (N_ODD, C_V_WIDTH, D), C_V_WIDTH ** -0.5),
        "w_router": nrm(ks[23], (D, N_EXPERTS), D ** -0.5),
        "b_router": nrm(ks[24], (N_EXPERTS,), 0.01),
        "w_gate_e": nrm(ks[25], (DEPTH, N_EXPERTS, D, D_FF_EXPERT), D ** -0.5),
        "w_up_e": nrm(ks[26], (DEPTH, N_EXPERTS, D, D_FF_EXPERT), D ** -0.5),
        "w_down_e": nrm(ks[27], (DEPTH, N_EXPERTS, D_FF_EXPERT, D), D_FF_EXPERT ** -0.5),
        "final_norm_g": 1.0 + nrm(ks[28], (D,), 0.02),
    }


def reference(x_prompt, x_sample, c, cache_a_k, cache_a_v, cache_b_k, cache_b_v, state_C, state_n, state_m,
              c_ctx, norm1_g, norm2_g, w_ada, b_ada, w_in_ab, w_out_ab, rel_bias_a, sink_b, w_in_c, b_gates_c,
              norm_c_g, w_out_c, w_router, b_router, w_gate_e, w_up_e, w_down_e, final_norm_g):
    x = x_prompt
    Bp = x.shape[0]
    ak, av, bk, bv, sC, sn, sm = [], [], [], [], [], [], []
    for l in range(DEPTH):
        sh1, sc1, g1, sh2, sc2, g2 = ada_mod(c_ctx[None, :], w_ada[l], b_ada[l])
        h = rmsnorm(x, norm1_g[l]) * (1 + sc1) + sh1
        j = l // 2
        if l % 2 == 0:
            qa, ka, va, qb, kb, vb = ab_project(h, w_in_ab[j])
            out = ab_merge(dense_attn(qa, ka, va), dense_gqa_sink(qb, kb, vb, sink_b[j]), w_out_ab[j])
            ak.append(ka); av.append(va); bk.append(kb); bv.append(vb)
        else:
            q, k, v, o, g = mlstm_project(h, w_in_c[j], b_gates_c[j])
            zero = (jnp.zeros((Bp, C_HEADS, C_QK_DIM, C_V_DIM), F32), jnp.zeros((Bp, C_HEADS, C_QK_DIM), F32),
                    jnp.zeros((Bp, C_HEADS), F32))
            hsum, (Cf, nf, mf), (Cb, nb_, mb) = mlstm_bidir(q, k, v, g, zero, zero)
            out = mlstm_out(hsum.astype(x.dtype), o, norm_c_g[j], w_out_c[j])
            sC.append(jnp.stack([Cf, Cb], axis=1)); sn.append(jnp.stack([nf, nb_], axis=1)); sm.append(jnp.stack([mf, mb], axis=1))
        x = x + g1 * out
        x = channel_mixer(x, norm2_g[l], sh2, sc2, g2, w_router, b_router, w_gate_e[l], w_up_e[l], w_down_e[l])
    y_prompt = rmsnorm(x, final_norm_g)
    new_a_k = jnp.stack(ak, axis=1)
    new_a_v = jnp.stack(av, axis=1)
    new_b_k = jnp.stack(bk, axis=1)
    new_b_v = jnp.stack(bv, axis=1)
    new_C = jnp.stack(sC, axis=1)
    new_n = jnp.stack(sn, axis=1)
    new_m = jnp.stack(sm, axis=1)

    x = x_sample
    for l in range(DEPTH):
        sh1, sc1, g1, sh2, sc2, g2 = ada_mod(c, w_ada[l], b_ada[l])
        h = rmsnorm(x, norm1_g[l]) * (1 + sc1) + sh1
        j = l // 2
        if l % 2 == 0:
            qa, ka, va, qb, kb, vb = ab_project(h, w_in_ab[j])
            oa = neighbourhood_attn(qa, ka, va, cache_a_k[:, j].astype(x.dtype), cache_a_v[:, j].astype(x.dtype), rel_bias_a[j])
            ob = banded_gqa_sink(axial_rope(qb), axial_rope(kb), vb, cache_b_k[:, j].astype(x.dtype),
                                 cache_b_v[:, j].astype(x.dtype), sink_b[j])
            out = ab_merge(oa, ob, w_out_ab[j])
        else:
            q, k, v, o, g = mlstm_project(h, w_in_c[j], b_gates_c[j])
            st_f = (state_C[:, j, 0], state_n[:, j, 0], state_m[:, j, 0])
            st_b = (state_C[:, j, 1], state_n[:, j, 1], state_m[:, j, 1])
            hsum, _, _ = mlstm_bidir(q, k, v, g, st_f, st_b)
            out = mlstm_out(hsum.astype(x.dtype), o, norm_c_g[j], w_out_c[j])
        x = x + g1 * out
        x = channel_mixer(x, norm2_g[l], sh2, sc2, g2, w_router, b_router, w_gate_e[l], w_up_e[l], w_down_e[l])
    y_sample = rmsnorm(x, final_norm_g)
    return (y_prompt, y_sample, new_a_k, new_a_v, new_b_k, new_b_v, new_C, new_n, new_m)
```

```python
import functools

import numpy as np
import jax
import jax.numpy as jnp
from jax import lax
from jax.experimental import pallas as pl
from jax.experimental.pallas import tpu as pltpu

F32 = jnp.float32
BF16 = jnp.bfloat16
I32 = jnp.int32

D_MODEL = 2048
BATCH = 16
SEQ = 256
DEPTH = 2
DEC_BATCH = 4
DEC_SEQ = 4096
PAST_LEN = 512
GRID_W = 64
HEAD_DIM = 128
A_HEADS = 8
NA_ROWS = 8
NA_COLS = 16
B_HEADS = 8
B_KV_HEADS = 2
B_WINDOW = 128
ROPE_THETA = 10000.0
C_HEADS = 8
C_QK_DIM = 128
C_V_DIM = 256
C_CHUNK = 128
N_EXPERTS = 16
N_GROUPS = 4
EXPERTS_PER_GROUP = N_EXPERTS // N_GROUPS
D_FF_EXPERT = 1024
EPS = 1e-6

A_WIDTH = A_HEADS * HEAD_DIM
B_Q_WIDTH = B_HEADS * HEAD_DIM
B_KV_WIDTH = B_KV_HEADS * HEAD_DIM
C_QK_WIDTH = C_HEADS * C_QK_DIM
C_V_WIDTH = C_HEADS * C_V_DIM

SEG_ROWS = 4096
NEG_BIG = -1e30
MOE_ROWS = 256
MIB = 1024 * 1024

_NT = (((1,), (1,)), ((), ()))


def _cparams(sem, vmem_mib):
    return pltpu.CompilerParams(dimension_semantics=sem, vmem_limit_bytes=vmem_mib * MIB)


def _modulated_norm(x, g, mod, shift_idx, scale_idx):
    ms = jnp.mean(x * x, axis=-1, keepdims=True)
    y = x * lax.rsqrt(ms + EPS) * g
    return y * (1.0 + mod[scale_idx:scale_idx + 1, :]) + mod[shift_idx:shift_idx + 1, :]


def _ada_kernel(c_ref, w_ref, b_ref, o_ref):
    c = c_ref[...]
    s = (c * jax.nn.sigmoid(c)).astype(BF16)
    o_ref[0] = jnp.dot(s, w_ref[0].astype(BF16), preferred_element_type=F32) + b_ref[0]


def ada_mod_all(cvec8, w_ada, b_ada, tn=1024):
    L, D, D6 = w_ada.shape
    return pl.pallas_call(
        _ada_kernel,
        out_shape=jax.ShapeDtypeStruct((L, 8, D6), F32),
        grid=(L, D6 // tn),
        in_specs=[pl.BlockSpec((8, D), lambda l, j: (0, 0)),
                  pl.BlockSpec((1, D, tn), lambda l, j: (l, 0, j)),
                  pl.BlockSpec((1, 1, tn), lambda l, j: (l, 0, j))],
        out_specs=pl.BlockSpec((1, 8, tn), lambda l, j: (l, 0, j)),
        compiler_params=_cparams(("parallel", "parallel"), 40),
        name="ada_mod",
    )(cvec8, w_ada, b_ada.reshape(L, 1, D6))


def _nmm_kernel(x_ref, g_ref, mod_ref, w_ref, *rest, shift_idx, scale_idx, has_aux):
    if has_aux:
        waux_ref, o_ref, oaux_ref, h_scr = rest
    else:
        o_ref, h_scr = rest

    @pl.when(pl.program_id(1) == 0)
    def _():
        h = _modulated_norm(x_ref[...], g_ref[...], mod_ref[0], shift_idx, scale_idx).astype(BF16)
        h_scr[...] = h
        if has_aux:
            oaux_ref[...] = jnp.dot(h, waux_ref[...], preferred_element_type=F32)

    o_ref[...] = jnp.dot(h_scr[...], w_ref[...], preferred_element_type=F32).astype(o_ref.dtype)


def norm_mod_matmul(x, row_off, n_rows, g, mod, w, out_dtype, shift_idx, scale_idx, w_aux=None,
                    tm=512, tn=512):
    D = x.shape[1]
    n_out = w.shape[1]
    off_b = row_off // tm
    per_seg = SEG_ROWS // tm
    has_aux = w_aux is not None
    in_specs = [pl.BlockSpec((tm, D), lambda i, j: (i + off_b, 0)),
                pl.BlockSpec((1, D), lambda i, j: (0, 0)),
                pl.BlockSpec((1, 6, D), lambda i, j: ((i + off_b) // per_seg, 0, 0)),
                pl.BlockSpec((D, tn), lambda i, j: (0, j))]
    out_shape = jax.ShapeDtypeStruct((n_rows, n_out), out_dtype)
    out_specs = pl.BlockSpec((tm, tn), lambda i, j: (i, j))
    args = [x, g.reshape(1, D), mod, w]
    if has_aux:
        n_aux = w_aux.shape[1]
        in_specs.append(pl.BlockSpec((D, n_aux), lambda i, j: (0, 0)))
        out_shape = (out_shape, jax.ShapeDtypeStruct((n_rows, n_aux), F32))
        out_specs = (out_specs, pl.BlockSpec((tm, n_aux), lambda i, j: (i, 0)))
        args.append(w_aux)
    return pl.pallas_call(
        functools.partial(_nmm_kernel, shift_idx=shift_idx, scale_idx=scale_idx, has_aux=has_aux),
        out_shape=out_shape,
        grid=(n_rows // tm, n_out // tn),
        in_specs=in_specs,
        out_specs=out_specs,
        scratch_shapes=[pltpu.VMEM((tm, D), BF16)],
        compiler_params=_cparams(("parallel", "arbitrary"), 40),
        name="norm_mod_matmul",
    )(*args)


def _mmres_kernel(a_ref, w_ref, x_ref, mod_ref, o_ref, *, gate_idx):
    acc = jnp.dot(a_ref[...], w_ref[...], preferred_element_type=F32)
    o_ref[...] = x_ref[...] + mod_ref[0, gate_idx:gate_idx + 1, :] * acc


def matmul_gated_residual(a, w, x, mod, gate_idx, tm=512):
    n, K = a.shape
    D = w.shape[1]
    per_seg = SEG_ROWS // tm
    return pl.pallas_call(
        functools.partial(_mmres_kernel, gate_idx=gate_idx),
        out_shape=jax.ShapeDtypeStruct((n, D), F32),
        grid=(n // tm,),
        in_specs=[pl.BlockSpec((tm, K), lambda i: (i, 0)),
                  pl.BlockSpec((K, D), lambda i: (0, 0)),
                  pl.BlockSpec((tm, D), lambda i: (i, 0)),
                  pl.BlockSpec((1, 6, D), lambda i: (i // per_seg, 0, 0))],
        out_specs=pl.BlockSpec((tm, D), lambda i: (i, 0)),
        compiler_params=_cparams(("parallel",), 48),
        name="matmul_gated_residual",
    )(a, w, x, mod)


def _mlstm_out_kernel(hf_ref, hb_ref, o_ref, ng_ref, w_ref, x_ref, mod_ref, out_ref, *, gate_idx):
    hs = hf_ref[0] + hb_ref[0]
    parts = []
    for h in range(C_HEADS):
        sl = slice(h * C_V_DIM, (h + 1) * C_V_DIM)
        xs = hs[:, sl]
        ms = jnp.mean(xs * xs, axis=-1, keepdims=True)
        hn = xs * lax.rsqrt(ms + EPS) * ng_ref[:, sl]
        parts.append((jax.nn.sigmoid(o_ref[:, sl].astype(F32)) * hn).astype(BF16))
    a = jnp.concatenate(parts, axis=-1)
    acc = jnp.dot(a, w_ref[...], preferred_element_type=F32)
    out_ref[...] = x_ref[...] + mod_ref[0, gate_idx:gate_idx + 1, :] * acc


def mlstm_out_residual(h_dir, proj, o_col_block, norm_g, w, x, mod, gate_idx, tm=256):
    n, D = x.shape
    V = C_V_WIDTH
    per_seg = SEG_ROWS // tm
    return pl.pallas_call(
        functools.partial(_mlstm_out_kernel, gate_idx=gate_idx),
        out_shape=jax.ShapeDtypeStruct((n, D), F32),
        grid=(n // tm,),
        in_specs=[pl.BlockSpec((1, tm, V), lambda i: (0, i, 0)),
                  pl.BlockSpec((1, tm, V), lambda i: (1, i, 0)),
                  pl.BlockSpec((tm, V), lambda i: (i, o_col_block)),
                  pl.BlockSpec((1, V), lambda i: (0, 0)),
                  pl.BlockSpec((V, D), lambda i: (0, 0)),
                  pl.BlockSpec((tm, D), lambda i: (i, 0)),
                  pl.BlockSpec((1, 6, D), lambda i: (i // per_seg, 0, 0))],
        out_specs=pl.BlockSpec((tm, D), lambda i: (i, 0)),
        compiler_params=_cparams(("parallel",), 48),
        name="mlstm_out_residual",
    )(h_dir, h_dir, proj, norm_g.reshape(1, V), w, x, mod)


def _ctx_attn_kernel(sink_ref, q_ref, k_ref, v_ref, o_ref):
    h = pl.program_id(1)
    q = q_ref[...].astype(BF16)
    k = k_ref[...].astype(BF16)
    v = v_ref[...].astype(BF16)
    s = lax.dot_general(q, k, _NT, preferred_element_type=F32) * (HEAD_DIM ** -0.5)
    sk = sink_ref[h]
    m = jnp.maximum(jnp.max(s, axis=-1, keepdims=True), sk)
    p = jnp.exp(s - m)
    l = jnp.sum(p, axis=-1, keepdims=True) + jnp.exp(sk - m)
    o = jnp.dot(p.astype(BF16), v, preferred_element_type=F32) / l
    o_ref[...] = o.astype(o_ref.dtype)


def ctx_attention(proj, sink_b, n_batch, seq):
    n_heads = A_HEADS + B_HEADS
    group = B_HEADS // B_KV_HEADS
    qb0 = 3 * A_HEADS
    kb0 = qb0 + B_HEADS
    vb0 = kb0 + B_KV_HEADS
    sinks = jnp.concatenate([jnp.full((A_HEADS,), NEG_BIG, F32), sink_b.astype(F32)])

    def q_map(b, h, s):
        return (b, jnp.where(h < A_HEADS, h, qb0 + h - A_HEADS))

    def k_map(b, h, s):
        return (b, jnp.where(h < A_HEADS, A_HEADS + h, kb0 + (h - A_HEADS) // group))

    def v_map(b, h, s):
        return (b, jnp.where(h < A_HEADS, 2 * A_HEADS + h, vb0 + (h - A_HEADS) // group))

    blk = (seq, HEAD_DIM)
    return pl.pallas_call(
        _ctx_attn_kernel,
        out_shape=jax.ShapeDtypeStruct((n_batch * seq, n_heads * HEAD_DIM), BF16),
        grid_spec=pltpu.PrefetchScalarGridSpec(
            num_scalar_prefetch=1,
            grid=(n_batch, n_heads),
            in_specs=[pl.BlockSpec(blk, q_map), pl.BlockSpec(blk, k_map), pl.BlockSpec(blk, v_map)],
            out_specs=pl.BlockSpec(blk, lambda b, h, s: (b, h))),
        compiler_params=_cparams(("parallel", "parallel"), 32),
        name="ctx_attention",
    )(sinks, proj, proj, proj)


NAT_QROWS = 4


def nat_bias_mask(rel_bias, rows):
    W = GRID_W
    nb = rows // NAT_QROWS
    kh = min(NA_ROWS, rows)
    variants = []
    for g in (0, 1, nb - 1):
        i = np.arange(NAT_QROWS)[:, None, None, None, None]
        qc = np.arange(W)[None, :, None, None, None]
        j = np.arange(3)[None, None, :, None, None]
        jr = np.arange(NAT_QROWS)[None, None, None, :, None]
        kc = np.arange(W)[None, None, None, None, :]
        r = NAT_QROWS * g + i
        kblk = g - 1 + j
        kr = NAT_QROWS * kblk + jr
        rs = np.clip(r - kh // 2, 0, rows - kh)
        row_ok = (kblk >= 0) & (kblk < nb) & (kr >= rs) & (kr < rs + kh)
        cstart = np.clip(qc - NA_COLS // 2, 0, W - NA_COLS)
        col_ok = (kc >= cstart) & (kc < cstart + NA_COLS)
        ok = np.broadcast_to(row_ok & col_ok, (NAT_QROWS, W, 3, NAT_QROWS, W))
        ridx = np.broadcast_to(np.clip(kr - r + NA_ROWS - 1, 0, 2 * NA_ROWS - 2), ok.shape)
        cidx = np.broadcast_to(np.clip(kc - qc + NA_COLS - 1, 0, 2 * NA_COLS - 2), ok.shape)
        n_q, n_k = NAT_QROWS * W, 3 * NAT_QROWS * W
        bias = rel_bias.astype(F32)[:, ridx.reshape(n_q, n_k), cidx.reshape(n_q, n_k)]
        variants.append(jnp.where(ok.reshape(1, n_q, n_k), bias, NEG_BIG))
    return jnp.stack(variants, axis=0)


def _nat_kernel(q_ref, k0_ref, k1_ref, k2_ref, v0_ref, v1_ref, v2_ref, kc_ref, vc_ref, bm_ref, o_ref):
    scale = HEAD_DIM ** -0.5
    q = q_ref[...]
    tq = q.shape[0]
    kc = kc_ref[0].astype(BF16)
    vc = vc_ref[0].astype(BF16)
    s_lat = [lax.dot_general(q, k_ref[...], _NT, preferred_element_type=F32) * scale
             + bm_ref[0, 0, :, j * tq:(j + 1) * tq]
             for j, k_ref in enumerate((k0_ref, k1_ref, k2_ref))]
    s_ctx = lax.dot_general(q, kc, _NT, preferred_element_type=F32) * scale
    m = jnp.max(s_ctx, axis=-1, keepdims=True)
    for s in s_lat:
        m = jnp.maximum(m, jnp.max(s, axis=-1, keepdims=True))
    p_ctx = jnp.exp(s_ctx - m)
    l = jnp.sum(p_ctx, axis=-1, keepdims=True)
    acc = jnp.dot(p_ctx.astype(BF16), vc, preferred_element_type=F32)
    for s, v_ref in zip(s_lat, (v0_ref, v1_ref, v2_ref)):
        p = jnp.exp(s - m)
        l = l + jnp.sum(p, axis=-1, keepdims=True)
        acc = acc + jnp.dot(p.astype(BF16), v_ref[...], preferred_element_type=F32)
    o_ref[...] = (acc / l).astype(o_ref.dtype)


def nat_attention(proj, cache_k, cache_v, bias_mask, n_batch, T, n_heads):
    tq = NAT_QROWS * GRID_W
    nb = T // tq
    P = cache_k.shape[1]

    def kv_map(col0, j):
        return lambda b, h, g: (b * nb + jnp.clip(g - 1 + j, 0, nb - 1), col0 + h)

    blk = (tq, HEAD_DIM)
    in_specs = [pl.BlockSpec(blk, lambda b, h, g: (b * nb + g, h))]
    in_specs += [pl.BlockSpec(blk, kv_map(n_heads, j)) for j in range(3)]
    in_specs += [pl.BlockSpec(blk, kv_map(2 * n_heads, j)) for j in range(3)]
    in_specs += [pl.BlockSpec((1, P, HEAD_DIM), lambda b, h, g: (b, 0, h))] * 2
    in_specs += [pl.BlockSpec((1, 1, tq, 3 * tq),
                              lambda b, h, g: (jnp.where(g == 0, 0, jnp.where(g == nb - 1, 2, 1)), h, 0, 0))]
    return pl.pallas_call(
        _nat_kernel,
        out_shape=jax.ShapeDtypeStruct((n_batch * T, n_heads * HEAD_DIM), BF16),
        grid=(n_batch, n_heads, nb),
        in_specs=in_specs,
        out_specs=pl.BlockSpec(blk, lambda b, h, g: (b * nb + g, h)),
        compiler_params=_cparams(("parallel", "parallel", "arbitrary"), 32),
        name="nat_attention",
    )(proj, proj, proj, proj, proj, proj, proj, cache_k, cache_v, bias_mask)


def rope_tables(T):
    t = jnp.arange(T)
    row = (t // GRID_W).astype(F32)
    col = (t % GRID_W).astype(F32)
    nf = HEAD_DIM // 4
    freqs = ROPE_THETA ** (-jnp.arange(nf, dtype=F32) / nf)
    ar = row[:, None] * freqs
    ac = col[:, None] * freqs
    cos = jnp.concatenate([jnp.cos(ar), jnp.cos(ar), jnp.cos(ac), jnp.cos(ac)], axis=-1)
    sin = jnp.concatenate([-jnp.sin(ar), jnp.sin(ar), -jnp.sin(ac), jnp.sin(ac)], axis=-1)
    return cos, sin


def _rope_kernel(x_ref, cos_ref, sin_ref, o_ref):
    x = x_ref[...].astype(F32)
    nf = HEAD_DIM // 4
    lane = lax.broadcasted_iota(I32, x.shape, 1)
    upper = pltpu.roll(x, HEAD_DIM - nf, 1)
    lower = pltpu.roll(x, nf, 1)
    partner = jnp.where((lane & nf) == 0, upper, lower)
    o_ref[...] = (x * cos_ref[...] + partner * sin_ref[...]).astype(o_ref.dtype)


def rope_heads(proj, col0, n_heads, cos, sin, T, tm=512):
    n = proj.shape[0]
    per_seq = T // tm
    return pl.pallas_call(
        _rope_kernel,
        out_shape=jax.ShapeDtypeStruct((n, n_heads * HEAD_DIM), BF16),
        grid=(n // tm, n_heads),
        in_specs=[pl.BlockSpec((tm, HEAD_DIM), lambda i, j: (i, col0 + j)),
                  pl.BlockSpec((tm, HEAD_DIM), lambda i, j: (i % per_seq, 0)),
                  pl.BlockSpec((tm, HEAD_DIM), lambda i, j: (i % per_seq, 0))],
        out_specs=pl.BlockSpec((tm, HEAD_DIM), lambda i, j: (i, j)),
        compiler_params=_cparams(("parallel", "parallel"), 32),
        name="rope_heads",
    )(proj, cos, sin)


SWA_TQ = 2 * B_WINDOW


def _swa_kernel(sink_ref, q_ref, k0_ref, k1_ref, k2_ref, k3_ref, v0_ref, v1_ref, v2_ref, v3_ref,
                kc_ref, vc_ref, o_ref, *, T):
    scale = HEAD_DIM ** -0.5
    group = B_HEADS // B_KV_HEADS
    kvh = pl.program_id(1)
    n = pl.program_id(2)
    k = jnp.concatenate([k0_ref[...], k1_ref[...], k2_ref[...], k3_ref[...]], axis=0)
    v = jnp.concatenate([v0_ref[...], v1_ref[...], v2_ref[...], v3_ref[...]], axis=0)
    kc = kc_ref[0].astype(BF16)
    vc = vc_ref[0].astype(BF16)
    nk = k.shape[0]
    qpos = n * SWA_TQ + lax.broadcasted_iota(I32, (SWA_TQ, nk), 0)
    kpos = n * SWA_TQ - B_WINDOW + lax.broadcasted_iota(I32, (SWA_TQ, nk), 1)
    dist = jnp.abs(qpos - kpos)
    ok = jnp.where(kpos >= 0, jnp.where(kpos < T, dist, B_WINDOW + 1), B_WINDOW + 1) <= B_WINDOW
    outs = []
    for gi in range(group):
        q = q_ref[:, gi * HEAD_DIM:(gi + 1) * HEAD_DIM]
        s_lat = jnp.where(ok, lax.dot_general(q, k, _NT, preferred_element_type=F32) * scale, NEG_BIG)
        s_ctx = lax.dot_general(q, kc, _NT, preferred_element_type=F32) * scale
        sk = sink_ref[kvh * group + gi]
        m = jnp.maximum(jnp.maximum(jnp.max(s_lat, axis=-1, keepdims=True),
                                    jnp.max(s_ctx, axis=-1, keepdims=True)), sk)
        p_lat = jnp.exp(s_lat - m)
        p_ctx = jnp.exp(s_ctx - m)
        l = (jnp.sum(p_lat, axis=-1, keepdims=True) + jnp.sum(p_ctx, axis=-1, keepdims=True)
             + jnp.exp(sk - m))
        acc = (jnp.dot(p_lat.astype(BF16), v, preferred_element_type=F32)
               + jnp.dot(p_ctx.astype(BF16), vc, preferred_element_type=F32))
        outs.append((acc / l).astype(o_ref.dtype))
    o_ref[...] = jnp.concatenate(outs, axis=-1)


def swa_attention(qk_rot, proj, v_col0, cache_k, cache_v, sink, n_batch, T):
    group = B_HEADS // B_KV_HEADS
    nq = T // SWA_TQ
    nkb = T // B_WINDOW
    P = cache_k.shape[1]

    def kv_map(col0, j):
        return lambda b, kvh, n, s: (b * nkb + jnp.clip(2 * n - 1 + j, 0, nkb - 1), col0 + kvh)

    kblk = (B_WINDOW, HEAD_DIM)
    in_specs = [pl.BlockSpec((SWA_TQ, group * HEAD_DIM), lambda b, kvh, n, s: (b * nq + n, kvh))]
    in_specs += [pl.BlockSpec(kblk, kv_map(B_HEADS, j)) for j in range(4)]
    in_specs += [pl.BlockSpec(kblk, kv_map(v_col0, j)) for j in range(4)]
    in_specs += [pl.BlockSpec((1, P, HEAD_DIM), lambda b, kvh, n, s: (b, 0, kvh))] * 2
    return pl.pallas_call(
        functools.partial(_swa_kernel, T=T),
        out_shape=jax.ShapeDtypeStruct((n_batch * T, B_Q_WIDTH), BF16),
        grid_spec=pltpu.PrefetchScalarGridSpec(
            num_scalar_prefetch=1,
            grid=(n_batch, B_KV_HEADS, nq),
            in_specs=in_specs,
            out_specs=pl.BlockSpec((SWA_TQ, group * HEAD_DIM), lambda b, kvh, n, s: (b * nq + n, kvh))),
        compiler_params=_cparams(("parallel", "parallel", "arbitrary"), 32),
        name="swa_attention",
    )(sink.astype(F32), qk_rot, qk_rot, qk_rot, qk_rot, qk_rot, proj, proj, proj, proj, cache_k, cache_v)


def _mlstm_kernel(rowblk_ref, seq_ref, first_ref, last_ref,
                  q_ref, k_ref, v_ref, g_ref, bg_ref, C0_ref, n0_ref, m0_ref,
                  h_ref, Cf_ref, nf_ref, mf_ref, C_scr, n_scr, m_scr):
    H, DK, DV, L = C_HEADS, C_QK_DIM, C_V_DIM, C_CHUNK
    d = pl.program_id(0)
    s = pl.program_id(1)
    scale = DK ** -0.5

    @pl.when(first_ref[s] == 1)
    def _():
        C_scr[...] = C0_ref[0, 0]
        n_scr[...] = n0_ref[0, 0]
        m_scr[...] = m0_ref[0, 0]

    g = g_ref[0] + bg_ref[0]
    lf = jax.nn.log_sigmoid(g)
    row = lax.broadcasted_iota(I32, (L, L), 0)
    col = lax.broadcasted_iota(I32, (L, L), 1)
    causal = (row - col) * (1 - 2 * d) >= 0
    b_col = jnp.dot(causal.astype(F32), lf, precision=lax.Precision.HIGHEST,
                    preferred_element_type=F32)
    b_row = b_col.T
    g_row = g.T
    for h in range(H):
        bcol = b_col[:, H + h:H + h + 1]
        brow = b_row[H + h:H + h + 1, :]
        icol = g[:, h:h + 1]
        irow = g_row[h:h + 1, :]
        mh = m_scr[h:h + 1, 0:1]
        nh = n_scr[h:h + 1, :]
        Ch = C_scr[h]
        qh = q_ref[:, h * DK:(h + 1) * DK]
        kh = k_ref[:, h * DK:(h + 1) * DK]
        vh = v_ref[:, h * DV:(h + 1) * DV]

        logw = jnp.where(causal, bcol - brow + irow, -jnp.inf)
        inter = bcol + mh
        mt = jnp.maximum(inter, jnp.max(logw, axis=-1, keepdims=True))
        w = jnp.exp(logw - mt)
        wi = jnp.exp(inter - mt)
        sqk = lax.dot_general(qh, kh, _NT, preferred_element_type=F32) * scale * w
        qC = jnp.dot(qh, Ch.astype(BF16), preferred_element_type=F32) * scale
        num = jnp.dot(sqk.astype(BF16), vh, preferred_element_type=F32) + wi * qC
        qn = jnp.sum(qh.astype(F32) * nh, axis=-1, keepdims=True) * scale
        den = jnp.sum(sqk, axis=-1, keepdims=True) + wi * qn
        h_ref[0, :, h * DV:(h + 1) * DV] = num / jnp.maximum(jnp.abs(den), jnp.exp(-mt))

        bl = jnp.where(d == 0, brow[:, L - 1:L], brow[:, 0:1])
        gl_row = bl - brow + irow
        m_new = jnp.maximum(bl + mh, jnp.max(gl_row, axis=-1, keepdims=True))
        ws_col = jnp.exp(bl - bcol + icol - m_new)
        wc = jnp.exp(bl + mh - m_new)
        kw = kh.astype(F32) * ws_col
        C_scr[h] = wc * Ch + jnp.dot(kw.T.astype(BF16), vh, preferred_element_type=F32)
        n_scr[h:h + 1, :] = wc * nh + jnp.sum(kw, axis=0, keepdims=True)
        m_scr[h:h + 1, :] = jnp.broadcast_to(m_new, (1, m_scr.shape[1]))

    @pl.when(last_ref[s] == 1)
    def _():
        Cf_ref[0, 0] = C_scr[...]
        nf_ref[0, 0] = n_scr[...]
        mf_ref[0, 0] = m_scr[...]


def mlstm_bidir(proj, gates_dir, b_gates_dir, C0, n0, m0, seq_chunks):
    H, DK, DV, L = C_HEADS, C_QK_DIM, C_V_DIM, C_CHUNK
    N = proj.shape[0]
    S = len(seq_chunks)
    rowblk, seq_id, first, last = [[], []], [], [], []
    base = 0
    for sq, nc in enumerate(seq_chunks):
        rowblk[0] += [base + c for c in range(nc)]
        rowblk[1] += [base + nc - 1 - c for c in range(nc)]
        seq_id += [sq] * nc
        first += [1] + [0] * (nc - 1)
        last += [0] * (nc - 1) + [1]
        base += nc
    n_steps = base
    rowblk = jnp.asarray(np.array(rowblk, np.int32).reshape(-1))
    tables = (rowblk, jnp.asarray(seq_id, I32), jnp.asarray(first, I32), jnp.asarray(last, I32))

    def row_map(colblk):
        return lambda d, s, rb, sq, fi, la: (rb[d * n_steps + s], colblk)

    def state_map(nd):
        return lambda d, s, rb, sq, fi, la: (sq[s], d) + (0,) * nd

    in_specs = [pl.BlockSpec((L, H * DK), row_map(0)),
                pl.BlockSpec((L, H * DK), row_map(1)),
                pl.BlockSpec((L, H * DV), row_map(1)),
                pl.BlockSpec((1, L, 128), lambda d, s, rb, sq, fi, la: (d, rb[d * n_steps + s], 0)),
                pl.BlockSpec((1, 1, 128), lambda d, s, rb, sq, fi, la: (d, 0, 0)),
                pl.BlockSpec((1, 1, H, DK, DV), state_map(3)),
                pl.BlockSpec((1, 1, H, DK), state_map(2)),
                pl.BlockSpec((1, 1, H, 128), state_map(2))]
    out_specs = (pl.BlockSpec((1, L, H * DV), lambda d, s, rb, sq, fi, la: (d, rb[d * n_steps + s], 0)),
                 pl.BlockSpec((1, 1, H, DK, DV), state_map(3)),
                 pl.BlockSpec((1, 1, H, DK), state_map(2)),
                 pl.BlockSpec((1, 1, H, 128), state_map(2)))
    out_shape = (jax.ShapeDtypeStruct((2, N, H * DV), F32),
                 jax.ShapeDtypeStruct((S, 2, H, DK, DV), F32),
                 jax.ShapeDtypeStruct((S, 2, H, DK), F32),
                 jax.ShapeDtypeStruct((S, 2, H, 128), F32))
    return pl.pallas_call(
        _mlstm_kernel,
        out_shape=out_shape,
        grid_spec=pltpu.PrefetchScalarGridSpec(
            num_scalar_prefetch=4,
            grid=(2, n_steps),
            in_specs=in_specs,
            out_specs=out_specs,
            scratch_shapes=[pltpu.VMEM((H, DK, DV), F32), pltpu.VMEM((H, DK), F32), pltpu.VMEM((H, 128), F32)]),
        compiler_params=_cparams(("arbitrary", "arbitrary"), 40),
        name="mlstm_bidir",
    )(*tables, proj, proj, proj, gates_dir, b_gates_dir, C0, n0, m0)


def _top2_of4(vals):
    m1, i1 = vals[0], jnp.zeros(vals[0].shape, I32)
    for j in range(1, 4):
        better = vals[j] > m1
        m1 = jnp.where(better, vals[j], m1)
        i1 = jnp.where(better, j, i1)
    m2, i2 = jnp.full(vals[0].shape, -jnp.inf, F32), jnp.zeros(vals[0].shape, I32)
    for j in range(4):
        cand = jnp.where(i1 == j, -jnp.inf, vals[j])
        better = cand > m2
        m2 = jnp.where(better, cand, m2)
        i2 = jnp.where(better, j, i2)
    return m1, i1, m2, i2


def _router_kernel(x_ref, g_ref, mod_ref, wr_ref, br_ref, h_ref, ids_ref, wts_ref, *, shift_idx, scale_idx):
    h = _modulated_norm(x_ref[...], g_ref[...], mod_ref[0], shift_idx, scale_idx)
    h_ref[...] = h.astype(BF16)
    logits = lax.dot_general(wr_ref[...], h, _NT, precision=lax.Precision.HIGHEST,
                             preferred_element_type=F32)
    aff = jax.nn.sigmoid(logits)
    sel = aff + br_ref[...]
    aff_rows = [aff[e:e + 1, :] for e in range(N_EXPERTS)]
    sel_rows = [sel[e:e + 1, :] for e in range(N_EXPERTS)]
    tops = [_top2_of4(sel_rows[4 * gidx:4 * gidx + 4]) for gidx in range(N_GROUPS)]
    best = tops[0][0] + tops[0][2]
    grp = jnp.zeros(best.shape, I32)
    i1, i2 = tops[0][1], tops[0][3]
    for gidx in range(1, N_GROUPS):
        score = tops[gidx][0] + tops[gidx][2]
        better = score > best
        best = jnp.where(better, score, best)
        grp = jnp.where(better, gidx, grp)
        i1 = jnp.where(better, tops[gidx][1], i1)
        i2 = jnp.where(better, tops[gidx][3], i2)
    e1 = grp * EXPERTS_PER_GROUP + i1
    e2 = grp * EXPERTS_PER_GROUP + i2
    w1 = jnp.zeros(best.shape, F32)
    w2 = jnp.zeros(best.shape, F32)
    for e in range(N_EXPERTS):
        w1 = jnp.where(e1 == e, aff_rows[e], w1)
        w2 = jnp.where(e2 == e, aff_rows[e], w2)
    tot = w1 + w2
    ids_ref[...] = jnp.zeros(ids_ref.shape, I32)
    wts_ref[...] = jnp.zeros(wts_ref.shape, F32)
    ids_ref[0:1, :] = e1
    ids_ref[1:2, :] = e2
    wts_ref[0:1, :] = w1 / tot
    wts_ref[1:2, :] = w2 / tot


def moe_router(x, g, mod, w_router_t, b_router, shift_idx, scale_idx, tm=512):
    N, D = x.shape
    per_seg = SEG_ROWS // tm
    return pl.pallas_call(
        functools.partial(_router_kernel, shift_idx=shift_idx, scale_idx=scale_idx),
        out_shape=(jax.ShapeDtypeStruct((N, D), BF16),
                   jax.ShapeDtypeStruct((8, N), I32),
                   jax.ShapeDtypeStruct((8, N), F32)),
        grid=(N // tm,),
        in_specs=[pl.BlockSpec((tm, D), lambda i: (i, 0)),
                  pl.BlockSpec((1, D), lambda i: (0, 0)),
                  pl.BlockSpec((1, 6, D), lambda i: (i // per_seg, 0, 0)),
                  pl.BlockSpec((N_EXPERTS, D), lambda i: (0, 0)),
                  pl.BlockSpec((N_EXPERTS, 1), lambda i: (0, 0))],
        out_specs=(pl.BlockSpec((tm, D), lambda i: (i, 0)),
                   pl.BlockSpec((8, tm), lambda i: (0, i)),
                   pl.BlockSpec((8, tm), lambda i: (0, i))),
        compiler_params=_cparams(("parallel",), 40),
        name="moe_router",
    )(x, g.reshape(1, D), mod, w_router_t, b_router.reshape(N_EXPERTS, 1).astype(F32))


def _moe_expert_kernel(be_ref, nu_ref, x_ref, wg_ref, wu_ref, wd_ref, y_ref):
    i = pl.program_id(0)

    @pl.when(i < nu_ref[0])
    def _():
        x = x_ref[...]
        gate = jnp.dot(x, wg_ref[0], preferred_element_type=F32)
        up = jnp.dot(x, wu_ref[0], preferred_element_type=F32)
        act = (gate * jax.nn.sigmoid(gate) * up).astype(BF16)
        y_ref[...] = jnp.dot(act, wd_ref[0], preferred_element_type=F32).astype(y_ref.dtype)

    @pl.when(i >= nu_ref[0])
    def _():
        y_ref[...] = jnp.zeros(y_ref.shape, y_ref.dtype)


def moe_experts(xg, block_e, n_used, wg, wu, wd):
    R, D = xg.shape
    F = wg.shape[2]
    nb = R // MOE_ROWS
    return pl.pallas_call(
        _moe_expert_kernel,
        out_shape=jax.ShapeDtypeStruct((R, D), BF16),
        grid_spec=pltpu.PrefetchScalarGridSpec(
            num_scalar_prefetch=2,
            grid=(nb,),
            in_specs=[pl.BlockSpec((MOE_ROWS, D), lambda i, be, nu: (i, 0)),
                      pl.BlockSpec((1, D, F), lambda i, be, nu: (be[i], 0, 0)),
                      pl.BlockSpec((1, D, F), lambda i, be, nu: (be[i], 0, 0)),
                      pl.BlockSpec((1, F, D), lambda i, be, nu: (be[i], 0, 0))],
            out_specs=pl.BlockSpec((MOE_ROWS, D), lambda i, be, nu: (i, 0))),
        compiler_params=_cparams(("arbitrary",), 48),
        name="moe_experts",
    )(block_e, n_used, xg, wg, wu, wd)


def _combine_kernel(x_ref, y0_ref, y1_ref, w_ref, mod_ref, *rest, gate_idx, final):
    if final:
        fg_ref, o_ref = rest
    else:
        (o_ref,) = rest
    y = w_ref[:, 0:1] * y0_ref[...].astype(F32) + w_ref[:, 1:2] * y1_ref[...].astype(F32)
    x = x_ref[...] + mod_ref[0, gate_idx:gate_idx + 1, :] * y
    if final:
        ms = jnp.mean(x * x, axis=-1, keepdims=True)
        x = x * lax.rsqrt(ms + EPS) * fg_ref[...]
    o_ref[...] = x


def moe_combine(x, y0, y1, wts, mod, gate_idx, row_off, n_rows, final_g=None, tm=512):
    D = x.shape[1]
    off_b = row_off // tm
    per_seg = SEG_ROWS // tm
    final = final_g is not None
    row_spec = pl.BlockSpec((tm, D), lambda i: (i + off_b, 0))
    in_specs = [row_spec, row_spec, row_spec,
                pl.BlockSpec((tm, 2), lambda i: (i + off_b, 0)),
                pl.BlockSpec((1, 6, D), lambda i: ((i + off_b) // per_seg, 0, 0))]
    args = [x, y0, y1, wts, mod]
    if final:
        in_specs.append(pl.BlockSpec((1, D), lambda i: (0, 0)))
        args.append(final_g.reshape(1, D))
    return pl.pallas_call(
        functools.partial(_combine_kernel, gate_idx=gate_idx, final=final),
        out_shape=jax.ShapeDtypeStruct((n_rows, D), F32),
        grid=(n_rows // tm,),
        in_specs=in_specs,
        out_specs=pl.BlockSpec((tm, D), lambda i: (i, 0)),
        compiler_params=_cparams(("parallel",), 40),
        name="moe_combine",
    )(*args)


def moe_dispatch_plan(ids, n_tokens):
    n_assign = 2 * n_tokens
    n_blocks = n_assign // MOE_ROWS + N_EXPERTS
    flat_e = ids.T.reshape(n_assign)
    onehot = (flat_e[:, None] == jnp.arange(N_EXPERTS, dtype=I32)[None, :]).astype(I32)
    csum = jnp.cumsum(onehot, axis=0)
    rank = jnp.sum((csum - onehot) * onehot, axis=1)
    counts = csum[-1]
    padded = (counts + MOE_ROWS - 1) // MOE_ROWS * MOE_ROWS
    pend = jnp.cumsum(padded)
    dest = (pend - padded)[flat_e] + rank
    n_used = (pend[-1] // MOE_ROWS).astype(I32).reshape(1)
    block_e = jnp.minimum(jnp.searchsorted(pend, jnp.arange(n_blocks, dtype=I32) * MOE_ROWS, side='right'),
                          N_EXPERTS - 1).astype(I32)
    src_tok = jnp.zeros((n_blocks * MOE_ROWS,), I32).at[dest].set(jnp.arange(n_assign, dtype=I32) // 2)
    return dest.reshape(n_tokens, 2), src_tok, block_e, n_used


def channel_mixer(x, g, mod, w_router_t, b_router, wg, wu, wd, splits, final_g=None):
    N = x.shape[0]
    h, ids8, wts8 = moe_router(x, g, mod, w_router_t, b_router, shift_idx=3, scale_idx=4)
    dest, src_tok, block_e, n_used = moe_dispatch_plan(ids8[:2], N)
    xg = jnp.take(h, src_tok, axis=0)
    ybuf = moe_experts(xg, block_e, n_used, wg, wu, wd)
    y0 = jnp.take(ybuf, dest[:, 0], axis=0)
    y1 = jnp.take(ybuf, dest[:, 1], axis=0)
    wts = wts8[:2].T
    return [moe_combine(x, y0, y1, wts, mod, 5, off, n, final_g) for off, n in splits]


def kernel(x_prompt, x_sample, c, cache_a_k, cache_a_v, cache_b_k, cache_b_v, state_C, state_n, state_m,
           c_ctx, norm1_g, norm2_g, w_ada, b_ada, w_in_ab, w_out_ab, rel_bias_a, sink_b, w_in_c, b_gates_c,
           norm_c_g, w_out_c, w_router, b_router, w_gate_e, w_up_e, w_down_e, final_norm_g):
    D = D_MODEL
    Np, Nl = BATCH * SEQ, DEC_BATCH * DEC_SEQ
    N = Np + Nl
    H = C_HEADS
    x = jnp.concatenate([x_prompt.reshape(Np, D), x_sample.reshape(Nl, D)], axis=0)

    cvec = jnp.concatenate([c_ctx[None, :], c, jnp.zeros((8 - 1 - DEC_BATCH, D), F32)], axis=0)
    mod_all = ada_mod_all(cvec, w_ada, b_ada).reshape(DEPTH, 8, 6, D)
    w_router_t = w_router.T.astype(F32)

    outs = {}
    for l in range(DEPTH):
        mod = mod_all[l]
        j = l // 2
        if l % 2 == 0:
            w_in = w_in_ab[j].astype(BF16)
            proj_ctx = norm_mod_matmul(x, 0, Np, norm1_g[l], mod, w_in, F32, 0, 1)
            proj_lat = norm_mod_matmul(x, Np, Nl, norm1_g[l], mod, w_in, BF16, 0, 1)
            o_ctx = ctx_attention(proj_ctx, sink_b[j], BATCH, SEQ)
            bias_mask = nat_bias_mask(rel_bias_a[j], DEC_SEQ // GRID_W)
            oa = nat_attention(proj_lat, cache_a_k[:, j].reshape(DEC_BATCH, PAST_LEN, A_WIDTH),
                               cache_a_v[:, j].reshape(DEC_BATCH, PAST_LEN, A_WIDTH), bias_mask,
                               DEC_BATCH, DEC_SEQ, A_HEADS)
            cos, sin = rope_tables(DEC_SEQ)
            qk_rot = rope_heads(proj_lat, 3 * A_HEADS, B_HEADS + B_KV_HEADS, cos, sin, DEC_SEQ)
            ob = swa_attention(qk_rot, proj_lat, 3 * A_HEADS + B_HEADS + B_KV_HEADS,
                               cache_b_k[:, j].reshape(DEC_BATCH, PAST_LEN, B_KV_WIDTH),
                               cache_b_v[:, j].reshape(DEC_BATCH, PAST_LEN, B_KV_WIDTH), sink_b[j],
                               DEC_BATCH, DEC_SEQ)
            o_all = jnp.concatenate([o_ctx, jnp.concatenate([oa, ob], axis=-1)], axis=0)
            x = matmul_gated_residual(o_all, w_out_ab[j].astype(BF16), x, mod, gate_idx=2)
            kv = proj_ctx[:, A_WIDTH:]
            outs.setdefault("ak", []).append(kv[:, :A_WIDTH].reshape(BATCH, SEQ, A_HEADS, HEAD_DIM))
            outs.setdefault("av", []).append(kv[:, A_WIDTH:2 * A_WIDTH].reshape(BATCH, SEQ, A_HEADS, HEAD_DIM))
            kvb = proj_ctx[:, 3 * A_WIDTH + B_Q_WIDTH:]
            outs.setdefault("bk", []).append(kvb[:, :B_KV_WIDTH].reshape(BATCH, SEQ, B_KV_HEADS, HEAD_DIM))
            outs.setdefault("bv", []).append(kvb[:, B_KV_WIDTH:].reshape(BATCH, SEQ, B_KV_HEADS, HEAD_DIM))
        else:
            n_main = 2 * C_QK_WIDTH + 2 * C_V_WIDTH
            w_main = w_in_c[j][:, :n_main].astype(BF16)
            w_gates = jnp.pad(w_in_c[j][:, n_main:], ((0, 0), (0, 128 - 4 * H))).astype(BF16)
            proj, gates = norm_mod_matmul(x, 0, N, norm1_g[l], mod, w_main, BF16, 0, 1, w_aux=w_gates)
            gates_dir = jnp.stack([jnp.pad(gates[:, 0:2 * H], ((0, 0), (0, 128 - 2 * H))),
                                   jnp.pad(gates[:, 2 * H:4 * H], ((0, 0), (0, 128 - 2 * H)))], axis=0)
            bg = b_gates_c[j].astype(F32).reshape(2, 1, 2 * H)
            bg = jnp.pad(bg, ((0, 0), (0, 0), (0, 128 - 2 * H)))
            S = BATCH + DEC_BATCH
            C0 = jnp.concatenate([jnp.zeros((BATCH, 2, H, C_QK_DIM, C_V_DIM), F32), state_C[:, j]], axis=0)
            n0 = jnp.concatenate([jnp.zeros((BATCH, 2, H, C_QK_DIM), F32), state_n[:, j]], axis=0)
            m0 = jnp.concatenate([jnp.zeros((BATCH, 2, H), F32), state_m[:, j]], axis=0)
            m0 = jnp.broadcast_to(m0[..., None], (S, 2, H, 128))
            seq_chunks = [SEQ // C_CHUNK] * BATCH + [DEC_SEQ // C_CHUNK] * DEC_BATCH
            h_dir, Cf, nf, mf = mlstm_bidir(proj, gates_dir, bg, C0, n0, m0, seq_chunks)
            x = mlstm_out_residual(h_dir, proj, 2, norm_c_g[j], w_out_c[j].astype(BF16), x, mod, gate_idx=2)
            outs.setdefault("C", []).append(Cf[:BATCH])
            outs.setdefault("n", []).append(nf[:BATCH])
            outs.setdefault("m", []).append(mf[:BATCH, :, :, 0])
        last = l == DEPTH - 1
        pieces = channel_mixer(x, norm2_g[l], mod, w_router_t, b_router,
                               w_gate_e[l].astype(BF16), w_up_e[l].astype(BF16), w_down_e[l].astype(BF16),
                               [(0, Np), (Np, Nl)] if last else [(0, N)],
                               final_norm_g if last else None)
        if last:
            y_prompt = pieces[0].reshape(BATCH, SEQ, D)
            y_sample = pieces[1].reshape(DEC_BATCH, DEC_SEQ, D)
        else:
            x = pieces[0]

    return (y_prompt, y_sample,
            jnp.stack(outs["ak"], axis=1), jnp.stack(outs["av"], axis=1),
            jnp.stack(outs["bk"], axis=1), jnp.stack(outs["bv"], axis=1),
            jnp.stack(outs["C"], axis=1), jnp.stack(outs["n"], axis=1), jnp.stack(outs["m"], axis=1))
```

```python
import functools

import numpy as np
import jax
import jax.numpy as jnp
from jax import lax
from jax.experimental import pallas as pl
from jax.experimental.pallas import tpu as pltpu

F32 = jnp.float32
BF16 = jnp.bfloat16
I32 = jnp.int32

D_MODEL = 2048
BATCH = 16
SEQ = 256
DEPTH = 2
DEC_BATCH = 4
DEC_SEQ = 4096
PAST_LEN = 512
GRID_W = 64
HEAD_DIM = 128
A_HEADS = 8
NA_ROWS = 8
NA_COLS = 16
B_HEADS = 8
B_KV_HEADS = 2
B_WINDOW = 128
ROPE_THETA = 10000.0
C_HEADS = 8
C_QK_DIM = 128
C_V_DIM = 256
C_CHUNK = 128
N_EXPERTS = 16
N_GROUPS = 4
EXPERTS_PER_GROUP = N_EXPERTS // N_GROUPS
D_FF_EXPERT = 1024
EPS = 1e-6

A_WIDTH = A_HEADS * HEAD_DIM
B_Q_WIDTH = B_HEADS * HEAD_DIM
B_KV_WIDTH = B_KV_HEADS * HEAD_DIM
C_QK_WIDTH = C_HEADS * C_QK_DIM
C_V_WIDTH = C_HEADS * C_V_DIM

SEG_ROWS = 4096
NEG_BIG = -1e30
MOE_ROWS = 256
NORM_ROWS = 256
MIB = 1024 * 1024
LANES = 128

_NT = (((1,), (1,)), ((), ()))


def _cparams(sem, vmem_mib):
    return pltpu.CompilerParams(dimension_semantics=sem, vmem_limit_bytes=vmem_mib * MIB)


def _modulated_norm(x, g, mod, shift_idx, scale_idx):
    ms = jnp.mean(x * x, axis=-1, keepdims=True)
    y = x * lax.rsqrt(ms + EPS) * g
    return y * (1.0 + mod[scale_idx:scale_idx + 1, :]) + mod[shift_idx:shift_idx + 1, :]


def _ada_kernel(c_ref, w_ref, b_ref, o_ref):
    c = c_ref[...]
    s = (c * jax.nn.sigmoid(c)).astype(BF16)
    o_ref[0] = jnp.dot(s, w_ref[0].astype(BF16), preferred_element_type=F32) + b_ref[0]


def ada_mod_all(cvec8, w_ada, b_ada, tn=1024):
    L, D, D6 = w_ada.shape
    return pl.pallas_call(
        _ada_kernel,
        out_shape=jax.ShapeDtypeStruct((L, 8, D6), F32),
        grid=(L, D6 // tn),
        in_specs=[pl.BlockSpec((8, D), lambda l, j: (0, 0)),
                  pl.BlockSpec((1, D, tn), lambda l, j: (l, 0, j)),
                  pl.BlockSpec((1, 1, tn), lambda l, j: (l, 0, j))],
        out_specs=pl.BlockSpec((1, 8, tn), lambda l, j: (l, 0, j)),
        compiler_params=_cparams(("parallel", "parallel"), 40),
        name="ada_mod",
    )(cvec8, w_ada, b_ada.reshape(L, 1, D6))


def _nmm_kernel(x_ref, g_ref, mod_ref, w_ref, *rest, shift_idx, scale_idx, has_aux):
    if has_aux:
        waux_ref, o_ref, oaux_ref, h_scr = rest
    else:
        o_ref, h_scr = rest

    @pl.when(pl.program_id(1) == 0)
    def _():
        for r in range(x_ref.shape[0] // NORM_ROWS):
            rows = pl.ds(r * NORM_ROWS, NORM_ROWS)
            h = _modulated_norm(x_ref[rows, :], g_ref[...], mod_ref[0], shift_idx, scale_idx).astype(BF16)
            h_scr[rows, :] = h
            if has_aux:
                oaux_ref[rows, :] = jnp.dot(h, waux_ref[...].astype(BF16), preferred_element_type=F32)

    o_ref[...] = jnp.dot(h_scr[...], w_ref[...].astype(BF16), preferred_element_type=F32).astype(o_ref.dtype)


def norm_mod_matmul(x, row_off, n_rows, g, mod, w, out_dtype, shift_idx, scale_idx, w_aux=None, n_out=None,
                    tm=1024, tn=512):
    D = x.shape[1]
    n_out = w.shape[1] if n_out is None else n_out
    off_b = row_off // tm
    per_seg = SEG_ROWS // tm
    has_aux = w_aux is not None
    in_specs = [pl.BlockSpec((tm, D), lambda i, j: (i + off_b, 0)),
                pl.BlockSpec((1, D), lambda i, j: (0, 0)),
                pl.BlockSpec((1, 6, D), lambda i, j: ((i + off_b) // per_seg, 0, 0)),
                pl.BlockSpec((D, tn), lambda i, j: (0, j))]
    out_shape = jax.ShapeDtypeStruct((n_rows, n_out), out_dtype)
    out_specs = pl.BlockSpec((tm, tn), lambda i, j: (i, j))
    args = [x, g.reshape(1, D), mod, w]
    if has_aux:
        n_aux = w_aux.shape[1]
        in_specs.append(pl.BlockSpec((D, n_aux), lambda i, j: (0, 0)))
        out_shape = (out_shape, jax.ShapeDtypeStruct((n_rows, n_aux), F32))
        out_specs = (out_specs, pl.BlockSpec((tm, n_aux), lambda i, j: (i, 0)))
        args.append(w_aux)
    return pl.pallas_call(
        functools.partial(_nmm_kernel, shift_idx=shift_idx, scale_idx=scale_idx, has_aux=has_aux),
        out_shape=out_shape,
        grid=(n_rows // tm, n_out // tn),
        in_specs=in_specs,
        out_specs=out_specs,
        scratch_shapes=[pltpu.VMEM((tm, D), BF16)],
        compiler_params=_cparams(("parallel", "arbitrary"), 52),
        name="norm_mod_matmul",
    )(*args)


def _mmres_kernel(a_ref, w_ref, x_ref, mod_ref, o_ref, *, gate_idx):
    acc = jnp.dot(a_ref[...], w_ref[...], preferred_element_type=F32)
    o_ref[...] = x_ref[...] + mod_ref[0, gate_idx:gate_idx + 1, :] * acc


def matmul_gated_residual(a, w, x, mod, gate_idx, tm=512):
    n, K = a.shape
    D = w.shape[1]
    per_seg = SEG_ROWS // tm
    return pl.pallas_call(
        functools.partial(_mmres_kernel, gate_idx=gate_idx),
        out_shape=jax.ShapeDtypeStruct((n, D), F32),
        grid=(n // tm,),
        in_specs=[pl.BlockSpec((tm, K), lambda i: (i, 0)),
                  pl.BlockSpec((K, D), lambda i: (0, 0)),
                  pl.BlockSpec((tm, D), lambda i: (i, 0)),
                  pl.BlockSpec((1, 6, D), lambda i: (i // per_seg, 0, 0))],
        out_specs=pl.BlockSpec((tm, D), lambda i: (i, 0)),
        compiler_params=_cparams(("parallel",), 48),
        name="matmul_gated_residual",
    )(a, w, x, mod)


def _mlstm_out_kernel(hf_ref, hb_ref, o_ref, ng_ref, w_ref, x_ref, mod_ref, out_ref, *, gate_idx):
    hs = hf_ref[0] + hb_ref[0]
    parts = []
    for h in range(C_HEADS):
        sl = slice(h * C_V_DIM, (h + 1) * C_V_DIM)
        xs = hs[:, sl]
        ms = jnp.mean(xs * xs, axis=-1, keepdims=True)
        hn = xs * lax.rsqrt(ms + EPS) * ng_ref[:, sl]
        parts.append((jax.nn.sigmoid(o_ref[:, sl].astype(F32)) * hn).astype(BF16))
    a = jnp.concatenate(parts, axis=-1)
    acc = jnp.dot(a, w_ref[...], preferred_element_type=F32)
    out_ref[...] = x_ref[...] + mod_ref[0, gate_idx:gate_idx + 1, :] * acc


def mlstm_out_residual(h_dir, proj, o_col_block, norm_g, w, x, mod, gate_idx, tm=256):
    n, D = x.shape
    V = C_V_WIDTH
    per_seg = SEG_ROWS // tm
    return pl.pallas_call(
        functools.partial(_mlstm_out_kernel, gate_idx=gate_idx),
        out_shape=jax.ShapeDtypeStruct((n, D), F32),
        grid=(n // tm,),
        in_specs=[pl.BlockSpec((1, tm, V), lambda i: (0, i, 0)),
                  pl.BlockSpec((1, tm, V), lambda i: (1, i, 0)),
                  pl.BlockSpec((tm, V), lambda i: (i, o_col_block)),
                  pl.BlockSpec((1, V), lambda i: (0, 0)),
                  pl.BlockSpec((V, D), lambda i: (0, 0)),
                  pl.BlockSpec((tm, D), lambda i: (i, 0)),
                  pl.BlockSpec((1, 6, D), lambda i: (i // per_seg, 0, 0))],
        out_specs=pl.BlockSpec((tm, D), lambda i: (i, 0)),
        compiler_params=_cparams(("parallel",), 48),
        name="mlstm_out_residual",
    )(h_dir, h_dir, proj, norm_g.reshape(1, V), w, x, mod)


def _ctx_attn_kernel(sink_ref, q_ref, k_ref, v_ref, o_ref):
    h = pl.program_id(1)
    q = q_ref[...].astype(BF16)
    k = k_ref[...].astype(BF16)
    v = v_ref[...].astype(BF16)
    s = lax.dot_general(q, k, _NT, preferred_element_type=F32) * (HEAD_DIM ** -0.5)
    sk = sink_ref[h]
    m = jnp.maximum(jnp.max(s, axis=-1, keepdims=True), sk)
    p = jnp.exp(s - m)
    l = jnp.sum(p, axis=-1, keepdims=True) + jnp.exp(sk - m)
    o = jnp.dot(p.astype(BF16), v, preferred_element_type=F32) / l
    o_ref[...] = o.astype(o_ref.dtype)


def ctx_attention(proj, sink_b, n_batch, seq):
    n_heads = A_HEADS + B_HEADS
    group = B_HEADS // B_KV_HEADS
    qb0 = 3 * A_HEADS
    kb0 = qb0 + B_HEADS
    vb0 = kb0 + B_KV_HEADS
    sinks = jnp.concatenate([jnp.full((A_HEADS,), NEG_BIG, F32), sink_b.astype(F32)])

    def q_map(b, h, s):
        return (b, jnp.where(h < A_HEADS, h, qb0 + h - A_HEADS))

    def k_map(b, h, s):
        return (b, jnp.where(h < A_HEADS, A_HEADS + h, kb0 + (h - A_HEADS) // group))

    def v_map(b, h, s):
        return (b, jnp.where(h < A_HEADS, 2 * A_HEADS + h, vb0 + (h - A_HEADS) // group))

    blk = (seq, HEAD_DIM)
    return pl.pallas_call(
        _ctx_attn_kernel,
        out_shape=jax.ShapeDtypeStruct((n_batch * seq, n_heads * HEAD_DIM), BF16),
        grid_spec=pltpu.PrefetchScalarGridSpec(
            num_scalar_prefetch=1,
            grid=(n_batch, n_heads),
            in_specs=[pl.BlockSpec(blk, q_map), pl.BlockSpec(blk, k_map), pl.BlockSpec(blk, v_map)],
            out_specs=pl.BlockSpec(blk, lambda b, h, s: (b, h))),
        compiler_params=_cparams(("parallel", "parallel"), 32),
        name="ctx_attention",
    )(sinks, proj, proj, proj)


NAT_QROWS = 4


def nat_bias_mask(rel_bias, rows):
    W = GRID_W
    nb = rows // NAT_QROWS
    kh = min(NA_ROWS, rows)
    n_dr, n_dc = 2 * NA_ROWS - 1, 2 * NA_COLS - 1
    H = rel_bias.shape[0]
    cidx = np.clip(np.arange(W)[None, :] - np.arange(W)[:, None] + NA_COLS - 1, 0, n_dc - 1)
    onehot = jnp.asarray((cidx.reshape(1, W * W) == np.arange(n_dc)[:, None]).astype(np.float32))
    col_bias = jnp.dot(rel_bias.astype(F32).reshape(H * n_dr, n_dc), onehot,
                       precision=lax.Precision.HIGHEST).reshape(H, n_dr, W, W)
    q_rows = []
    for qi in range(NAT_QROWS):
        tiles = [col_bias[:, int(np.clip(NAT_QROWS * (kj - 1) + kjr - qi + NA_ROWS - 1, 0, n_dr - 1))]
                 for kj in range(3) for kjr in range(NAT_QROWS)]
        q_rows.append(jnp.concatenate(tiles, axis=-1))
    bias = jnp.concatenate(q_rows, axis=1)
    variants = []
    for g in (0, 1, nb - 1):
        i = np.arange(NAT_QROWS)[:, None, None, None, None]
        qc = np.arange(W)[None, :, None, None, None]
        j = np.arange(3)[None, None, :, None, None]
        jr = np.arange(NAT_QROWS)[None, None, None, :, None]
        kc = np.arange(W)[None, None, None, None, :]
        r = NAT_QROWS * g + i
        kblk = g - 1 + j
        kr = NAT_QROWS * kblk + jr
        rs = np.clip(r - kh // 2, 0, rows - kh)
        row_ok = (kblk >= 0) & (kblk < nb) & (kr >= rs) & (kr < rs + kh)
        cstart = np.clip(qc - NA_COLS // 2, 0, W - NA_COLS)
        col_ok = (kc >= cstart) & (kc < cstart + NA_COLS)
        ok = np.broadcast_to(row_ok & col_ok, (NAT_QROWS, W, 3, NAT_QROWS, W))
        n_q, n_k = NAT_QROWS * W, 3 * NAT_QROWS * W
        variants.append(jnp.where(jnp.asarray(ok.reshape(1, n_q, n_k)), bias, NEG_BIG))
    return jnp.stack(variants, axis=0)


def _nat_kernel(q_ref, k0_ref, k1_ref, k2_ref, v0_ref, v1_ref, v2_ref, kc_ref, vc_ref, bm_ref, o_ref):
    scale = HEAD_DIM ** -0.5
    q = q_ref[...]
    tq = q.shape[0]
    kc = kc_ref[0].astype(BF16)
    vc = vc_ref[0].astype(BF16)
    s_lat = [lax.dot_general(q, k_ref[...], _NT, preferred_element_type=F32) * scale
             + bm_ref[0, 0, :, j * tq:(j + 1) * tq]
             for j, k_ref in enumerate((k0_ref, k1_ref, k2_ref))]
    s_ctx = lax.dot_general(q, kc, _NT, preferred_element_type=F32) * scale
    m = jnp.max(s_ctx, axis=-1, keepdims=True)
    for s in s_lat:
        m = jnp.maximum(m, jnp.max(s, axis=-1, keepdims=True))
    p_ctx = jnp.exp(s_ctx - m)
    l = jnp.sum(p_ctx, axis=-1, keepdims=True)
    acc = jnp.dot(p_ctx.astype(BF16), vc, preferred_element_type=F32)
    for s, v_ref in zip(s_lat, (v0_ref, v1_ref, v2_ref)):
        p = jnp.exp(s - m)
        l = l + jnp.sum(p, axis=-1, keepdims=True)
        acc = acc + jnp.dot(p.astype(BF16), v_ref[...], preferred_element_type=F32)
    o_ref[...] = (acc / l).astype(o_ref.dtype)


def nat_attention(proj, cache_k, cache_v, bias_mask, n_batch, T, n_heads):
    tq = NAT_QROWS * GRID_W
    nb = T // tq
    P = cache_k.shape[1]

    def kv_map(col0, j):
        return lambda b, h, g: (b * nb + jnp.clip(g - 1 + j, 0, nb - 1), col0 + h)

    blk = (tq, HEAD_DIM)
    in_specs = [pl.BlockSpec(blk, lambda b, h, g: (b * nb + g, h))]
    in_specs += [pl.BlockSpec(blk, kv_map(n_heads, j)) for j in range(3)]
    in_specs += [pl.BlockSpec(blk, kv_map(2 * n_heads, j)) for j in range(3)]
    in_specs += [pl.BlockSpec((1, P, HEAD_DIM), lambda b, h, g: (b, 0, h))] * 2
    in_specs += [pl.BlockSpec((1, 1, tq, 3 * tq),
                              lambda b, h, g: (jnp.where(g == 0, 0, jnp.where(g == nb - 1, 2, 1)), h, 0, 0))]
    return pl.pallas_call(
        _nat_kernel,
        out_shape=jax.ShapeDtypeStruct((n_batch * T, n_heads * HEAD_DIM), BF16),
        grid=(n_batch, n_heads, nb),
        in_specs=in_specs,
        out_specs=pl.BlockSpec(blk, lambda b, h, g: (b * nb + g, h)),
        compiler_params=_cparams(("parallel", "parallel", "arbitrary"), 32),
        name="nat_attention",
    )(proj, proj, proj, proj, proj, proj, proj, cache_k, cache_v, bias_mask)


def rope_tables(T):
    t = jnp.arange(T)
    row = (t // GRID_W).astype(F32)
    col = (t % GRID_W).astype(F32)
    nf = HEAD_DIM // 4
    freqs = ROPE_THETA ** (-jnp.arange(nf, dtype=F32) / nf)
    ar = row[:, None] * freqs
    ac = col[:, None] * freqs
    cos = jnp.concatenate([jnp.cos(ar), jnp.cos(ar), jnp.cos(ac), jnp.cos(ac)], axis=-1)
    sin = jnp.concatenate([-jnp.sin(ar), jnp.sin(ar), -jnp.sin(ac), jnp.sin(ac)], axis=-1)
    return cos, sin


def _rope_kernel(x_ref, cos_ref, sin_ref, o_ref):
    x = x_ref[...].astype(F32)
    nf = HEAD_DIM // 4
    lane = lax.broadcasted_iota(I32, x.shape, 1)
    upper = pltpu.roll(x, HEAD_DIM - nf, 1)
    lower = pltpu.roll(x, nf, 1)
    partner = jnp.where((lane & nf) == 0, upper, lower)
    o_ref[...] = (x * cos_ref[...] + partner * sin_ref[...]).astype(o_ref.dtype)


def rope_heads(proj, col0, n_heads, cos, sin, T, tm=512):
    n = proj.shape[0]
    per_seq = T // tm
    return pl.pallas_call(
        _rope_kernel,
        out_shape=jax.ShapeDtypeStruct((n, n_heads * HEAD_DIM), BF16),
        grid=(n // tm, n_heads),
        in_specs=[pl.BlockSpec((tm, HEAD_DIM), lambda i, j: (i, col0 + j)),
                  pl.BlockSpec((tm, HEAD_DIM), lambda i, j: (i % per_seq, 0)),
                  pl.BlockSpec((tm, HEAD_DIM), lambda i, j: (i % per_seq, 0))],
        out_specs=pl.BlockSpec((tm, HEAD_DIM), lambda i, j: (i, j)),
        compiler_params=_cparams(("parallel", "parallel"), 32),
        name="rope_heads",
    )(proj, cos, sin)


SWA_TQ = 2 * B_WINDOW


def _swa_kernel(sink_ref, q_ref, k0_ref, k1_ref, k2_ref, k3_ref, v0_ref, v1_ref, v2_ref, v3_ref,
                kc_ref, vc_ref, o_ref, *, T):
    scale = HEAD_DIM ** -0.5
    group = B_HEADS // B_KV_HEADS
    kvh = pl.program_id(1)
    n = pl.program_id(2)
    k = jnp.concatenate([k0_ref[...], k1_ref[...], k2_ref[...], k3_ref[...]], axis=0)
    v = jnp.concatenate([v0_ref[...], v1_ref[...], v2_ref[...], v3_ref[...]], axis=0)
    kc = kc_ref[0].astype(BF16)
    vc = vc_ref[0].astype(BF16)
    nk = k.shape[0]
    qpos = n * SWA_TQ + lax.broadcasted_iota(I32, (SWA_TQ, nk), 0)
    kpos = n * SWA_TQ - B_WINDOW + lax.broadcasted_iota(I32, (SWA_TQ, nk), 1)
    dist = jnp.abs(qpos - kpos)
    ok = jnp.where(kpos >= 0, jnp.where(kpos < T, dist, B_WINDOW + 1), B_WINDOW + 1) <= B_WINDOW
    outs = []
    for gi in range(group):
        q = q_ref[:, gi * HEAD_DIM:(gi + 1) * HEAD_DIM]
        s_lat = jnp.where(ok, lax.dot_general(q, k, _NT, preferred_element_type=F32) * scale, NEG_BIG)
        s_ctx = lax.dot_general(q, kc, _NT, preferred_element_type=F32) * scale
        sk = sink_ref[kvh * group + gi]
        m = jnp.maximum(jnp.maximum(jnp.max(s_lat, axis=-1, keepdims=True),
                                    jnp.max(s_ctx, axis=-1, keepdims=True)), sk)
        p_lat = jnp.exp(s_lat - m)
        p_ctx = jnp.exp(s_ctx - m)
        l = (jnp.sum(p_lat, axis=-1, keepdims=True) + jnp.sum(p_ctx, axis=-1, keepdims=True)
             + jnp.exp(sk - m))
        acc = (jnp.dot(p_lat.astype(BF16), v, preferred_element_type=F32)
               + jnp.dot(p_ctx.astype(BF16), vc, preferred_element_type=F32))
        outs.append((acc / l).astype(o_ref.dtype))
    o_ref[...] = jnp.concatenate(outs, axis=-1)


def swa_attention(qk_rot, proj, v_col0, cache_k, cache_v, sink, n_batch, T):
    group = B_HEADS // B_KV_HEADS
    nq = T // SWA_TQ
    nkb = T // B_WINDOW
    P = cache_k.shape[1]

    def kv_map(col0, j):
        return lambda b, kvh, n, s: (b * nkb + jnp.clip(2 * n - 1 + j, 0, nkb - 1), col0 + kvh)

    kblk = (B_WINDOW, HEAD_DIM)
    in_specs = [pl.BlockSpec((SWA_TQ, group * HEAD_DIM), lambda b, kvh, n, s: (b * nq + n, kvh))]
    in_specs += [pl.BlockSpec(kblk, kv_map(B_HEADS, j)) for j in range(4)]
    in_specs += [pl.BlockSpec(kblk, kv_map(v_col0, j)) for j in range(4)]
    in_specs += [pl.BlockSpec((1, P, HEAD_DIM), lambda b, kvh, n, s: (b, 0, kvh))] * 2
    return pl.pallas_call(
        functools.partial(_swa_kernel, T=T),
        out_shape=jax.ShapeDtypeStruct((n_batch * T, B_Q_WIDTH), BF16),
        grid_spec=pltpu.PrefetchScalarGridSpec(
            num_scalar_prefetch=1,
            grid=(n_batch, B_KV_HEADS, nq),
            in_specs=in_specs,
            out_specs=pl.BlockSpec((SWA_TQ, group * HEAD_DIM), lambda b, kvh, n, s: (b * nq + n, kvh))),
        compiler_params=_cparams(("parallel", "parallel", "arbitrary"), 32),
        name="swa_attention",
    )(sink.astype(F32), qk_rot, qk_rot, qk_rot, qk_rot, qk_rot, proj, proj, proj, proj, cache_k, cache_v)


def _mlstm_kernel(rowblk_ref, seq_ref, first_ref, last_ref,
                  q_ref, k_ref, v_ref, g_ref, bg_ref, S0_ref, m0_ref,
                  h_ref, Sf_ref, mf_ref, S_scr, m_scr):
    H, DK, DV, L, R = C_HEADS, C_QK_DIM, C_V_DIM, C_CHUNK, LANES
    d = pl.program_id(0)
    s = pl.program_id(1)
    scale = DK ** -0.5

    @pl.when(first_ref[s] == 1)
    def _():
        S_scr[...] = S0_ref[0, 0]
        m_scr[...] = m0_ref[0, 0]

    gi = g_ref[0, 0] + bg_ref[0, 0]
    lf = jax.nn.log_sigmoid(g_ref[0, 1] + bg_ref[0, 1])
    row = lax.broadcasted_iota(I32, (L, L), 0)
    col = lax.broadcasted_iota(I32, (L, L), 1)
    causal = (row - col) * (1 - 2 * d) >= 0
    b_all = jnp.dot(causal.astype(F32), lf, precision=lax.Precision.HIGHEST,
                    preferred_element_type=F32)
    a_row = (gi - b_all).T[0:H, :]
    b_row = b_all.T[0:H, :]
    lane = lax.broadcasted_iota(I32, (H, L), 1)
    cm = a_row
    k = 1
    while k < L:
        fwd = jnp.where(lane >= k, pltpu.roll(cm, k, 1), -jnp.inf)
        bwd = jnp.where(lane < L - k, pltpu.roll(cm, L - k, 1), -jnp.inf)
        cm = jnp.maximum(cm, jnp.where(d == 0, fwd, bwd))
        k *= 2
    m_all = m_scr[...]
    M_row = jnp.maximum(m_all, cm)
    wi_row = jnp.exp(m_all - M_row)
    emt_row = jnp.exp(-(b_row + M_row))
    M_last = jnp.where(d == 0, M_row[:, L - 1:L], M_row[:, 0:1])
    b_last = jnp.where(d == 0, b_row[:, L - 1:L], b_row[:, 0:1])
    ws_row = jnp.exp(a_row - M_last)
    wc_all = jnp.exp(m_all - M_last)
    m_scr[...] = jnp.broadcast_to(b_last + M_last, m_all.shape)
    cols = jnp.concatenate([M_row, wi_row, emt_row, jnp.zeros((L - 3 * H, L), F32)], axis=0).T
    ones = jnp.ones((L, R), BF16)
    for h in range(H):
        M_col = cols[:, h:h + 1]
        wi_col = cols[:, H + h:H + h + 1]
        emt_col = cols[:, 2 * H + h:2 * H + h + 1]
        qh = q_ref[:, h * DK:(h + 1) * DK]
        kh = k_ref[:, h * DK:(h + 1) * DK]
        vh = v_ref[:, h * DV:(h + 1) * DV]
        v_ext = jnp.concatenate([vh, ones], axis=-1)
        Sh = S_scr[h]

        w = jnp.where(causal, jnp.exp(a_row[h:h + 1, :] - M_col), 0.0)
        sqk = lax.dot_general(qh, kh, _NT, preferred_element_type=F32) * scale * w
        S_hi = Sh.astype(BF16)
        n_lo = (Sh[:, DV:] - S_hi[:, DV:].astype(F32)).astype(BF16)
        inter = jnp.dot(qh, jnp.concatenate([S_hi, n_lo], axis=-1),
                        preferred_element_type=F32) * scale
        qn = inter[:, DV:DV + 1] + inter[:, DV + R:DV + R + 1]
        num = jnp.dot(sqk.astype(BF16), vh, preferred_element_type=F32) + wi_col * inter[:, :DV]
        den = jnp.sum(sqk, axis=-1, keepdims=True) + wi_col * qn
        h_ref[0, :, h * DV:(h + 1) * DV] = num / jnp.maximum(jnp.abs(den), emt_col)

        kwT = kh.astype(F32).T * ws_row[h:h + 1, :]
        kwT_hi = kwT.astype(BF16)
        kwT_lo = (kwT - kwT_hi.astype(F32)).astype(BF16)
        upd = jnp.dot(kwT_hi, v_ext, preferred_element_type=F32)
        upd_n = upd[:, DV:] + jnp.dot(kwT_lo, ones, preferred_element_type=F32)
        wc = jnp.concatenate([wc_all[h:h + 1, :]] * (DV // R + 1), axis=-1)
        S_scr[h] = wc * Sh + jnp.concatenate([upd[:, :DV], upd_n], axis=-1)

    @pl.when(last_ref[s] == 1)
    def _():
        Sf_ref[0, 0] = S_scr[...]
        mf_ref[0, 0] = m_scr[...]


def mlstm_bidir(proj, gates_dir, b_gates_dir, S0, m0, seq_chunks):
    H, DK, DV, L = C_HEADS, C_QK_DIM, C_V_DIM, C_CHUNK
    DS = DV + LANES
    N = proj.shape[0]
    S = len(seq_chunks)
    rowblk, seq_id, first, last = [[], []], [], [], []
    base = 0
    for sq, nc in enumerate(seq_chunks):
        rowblk[0] += [base + c for c in range(nc)]
        rowblk[1] += [base + nc - 1 - c for c in range(nc)]
        seq_id += [sq] * nc
        first += [1] + [0] * (nc - 1)
        last += [0] * (nc - 1) + [1]
        base += nc
    n_steps = base
    rowblk = jnp.asarray(np.array(rowblk, np.int32).reshape(-1))
    tables = (rowblk, jnp.asarray(seq_id, I32), jnp.asarray(first, I32), jnp.asarray(last, I32))

    def row_map(colblk):
        return lambda d, s, rb, sq, fi, la: (rb[d * n_steps + s], colblk)

    def state_map(nd):
        return lambda d, s, rb, sq, fi, la: (sq[s], d) + (0,) * nd

    in_specs = [pl.BlockSpec((L, H * DK), row_map(0)),
                pl.BlockSpec((L, H * DK), row_map(1)),
                pl.BlockSpec((L, H * DV), row_map(1)),
                pl.BlockSpec((1, 2, L, LANES), lambda d, s, rb, sq, fi, la: (d, 0, rb[d * n_steps + s], 0)),
                pl.BlockSpec((1, 2, 1, LANES), lambda d, s, rb, sq, fi, la: (d, 0, 0, 0)),
                pl.BlockSpec((1, 1, H, DK, DS), state_map(3)),
                pl.BlockSpec((1, 1, H, LANES), state_map(2))]
    out_specs = (pl.BlockSpec((1, L, H * DV), lambda d, s, rb, sq, fi, la: (d, rb[d * n_steps + s], 0)),
                 pl.BlockSpec((1, 1, H, DK, DS), state_map(3)),
                 pl.BlockSpec((1, 1, H, LANES), state_map(2)))
    out_shape = (jax.ShapeDtypeStruct((2, N, H * DV), F32),
                 jax.ShapeDtypeStruct((S, 2, H, DK, DS), F32),
                 jax.ShapeDtypeStruct((S, 2, H, LANES), F32))
    return pl.pallas_call(
        _mlstm_kernel,
        out_shape=out_shape,
        grid_spec=pltpu.PrefetchScalarGridSpec(
            num_scalar_prefetch=4,
            grid=(2, n_steps),
            in_specs=in_specs,
            out_specs=out_specs,
            scratch_shapes=[pltpu.VMEM((H, DK, DS), F32), pltpu.VMEM((H, LANES), F32)]),
        compiler_params=_cparams(("arbitrary", "arbitrary"), 40),
        name="mlstm_bidir",
    )(*tables, proj, proj, proj, gates_dir, b_gates_dir, S0, m0)


def _top2_of4(vals):
    m1, i1 = vals[0], jnp.zeros(vals[0].shape, I32)
    for j in range(1, 4):
        better = vals[j] > m1
        m1 = jnp.where(better, vals[j], m1)
        i1 = jnp.where(better, j, i1)
    m2, i2 = jnp.full(vals[0].shape, -jnp.inf, F32), jnp.zeros(vals[0].shape, I32)
    for j in range(4):
        cand = jnp.where(i1 == j, -jnp.inf, vals[j])
        better = cand > m2
        m2 = jnp.where(better, cand, m2)
        i2 = jnp.where(better, j, i2)
    return m1, i1, m2, i2


def _router_kernel(x_ref, g_ref, mod_ref, wr_ref, br_ref, h_ref, ids_ref, wts_ref, *, shift_idx, scale_idx):
    h = _modulated_norm(x_ref[...], g_ref[...], mod_ref[0], shift_idx, scale_idx)
    h_ref[...] = h.astype(BF16)
    logits = lax.dot_general(wr_ref[...], h, _NT, precision=lax.Precision.HIGHEST,
                             preferred_element_type=F32)
    aff = jax.nn.sigmoid(logits)
    sel = aff + br_ref[...]
    aff_rows = [aff[e:e + 1, :] for e in range(N_EXPERTS)]
    sel_rows = [sel[e:e + 1, :] for e in range(N_EXPERTS)]
    tops = [_top2_of4(sel_rows[4 * gidx:4 * gidx + 4]) for gidx in range(N_GROUPS)]
    best = tops[0][0] + tops[0][2]
    grp = jnp.zeros(best.shape, I32)
    i1, i2 = tops[0][1], tops[0][3]
    for gidx in range(1, N_GROUPS):
        score = tops[gidx][0] + tops[gidx][2]
        better = score > best
        best = jnp.where(better, score, best)
        grp = jnp.where(better, gidx, grp)
        i1 = jnp.where(better, tops[gidx][1], i1)
        i2 = jnp.where(better, tops[gidx][3], i2)
    e1 = grp * EXPERTS_PER_GROUP + i1
    e2 = grp * EXPERTS_PER_GROUP + i2
    w1 = jnp.zeros(best.shape, F32)
    w2 = jnp.zeros(best.shape, F32)
    for e in range(N_EXPERTS):
        w1 = jnp.where(e1 == e, aff_rows[e], w1)
        w2 = jnp.where(e2 == e, aff_rows[e], w2)
    tot = w1 + w2
    ids_ref[...] = jnp.zeros(ids_ref.shape, I32)
    wts_ref[...] = jnp.zeros(wts_ref.shape, F32)
    ids_ref[0:1, :] = e1
    ids_ref[1:2, :] = e2
    wts_ref[0:1, :] = w1 / tot
    wts_ref[1:2, :] = w2 / tot


def moe_router(x, g, mod, w_router_t, b_router, shift_idx, scale_idx, tm=512):
    N, D = x.shape
    per_seg = SEG_ROWS // tm
    return pl.pallas_call(
        functools.partial(_router_kernel, shift_idx=shift_idx, scale_idx=scale_idx),
        out_shape=(jax.ShapeDtypeStruct((N, D), BF16),
                   jax.ShapeDtypeStruct((8, N), I32),
                   jax.ShapeDtypeStruct((8, N), F32)),
        grid=(N // tm,),
        in_specs=[pl.BlockSpec((tm, D), lambda i: (i, 0)),
                  pl.BlockSpec((1, D), lambda i: (0, 0)),
                  pl.BlockSpec((1, 6, D), lambda i: (i // per_seg, 0, 0)),
                  pl.BlockSpec((N_EXPERTS, D), lambda i: (0, 0)),
                  pl.BlockSpec((N_EXPERTS, 1), lambda i: (0, 0))],
        out_specs=(pl.BlockSpec((tm, D), lambda i: (i, 0)),
                   pl.BlockSpec((8, tm), lambda i: (0, i)),
                   pl.BlockSpec((8, tm), lambda i: (0, i))),
        compiler_params=_cparams(("parallel",), 40),
        name="moe_router",
    )(x, g.reshape(1, D), mod, w_router_t, b_router.reshape(N_EXPERTS, 1).astype(F32))


def _moe_expert_kernel(be_ref, nu_ref, x_ref, wg_ref, wu_ref, wd_ref, y_ref):
    i = pl.program_id(0)

    @pl.when(i < nu_ref[0])
    def _():
        x = x_ref[...]
        gate = jnp.dot(x, wg_ref[0], preferred_element_type=F32)
        up = jnp.dot(x, wu_ref[0], preferred_element_type=F32)
        act = (gate * jax.nn.sigmoid(gate) * up).astype(BF16)
        y_ref[...] = jnp.dot(act, wd_ref[0], preferred_element_type=F32).astype(y_ref.dtype)

    @pl.when(i >= nu_ref[0])
    def _():
        y_ref[...] = jnp.zeros(y_ref.shape, y_ref.dtype)


def moe_experts(xg, block_e, n_used, wg, wu, wd):
    R, D = xg.shape
    F = wg.shape[2]
    nb = R // MOE_ROWS
    return pl.pallas_call(
        _moe_expert_kernel,
        out_shape=jax.ShapeDtypeStruct((R, D), BF16),
        grid_spec=pltpu.PrefetchScalarGridSpec(
            num_scalar_prefetch=2,
            grid=(nb,),
            in_specs=[pl.BlockSpec((MOE_ROWS, D), lambda i, be, nu: (i, 0)),
                      pl.BlockSpec((1, D, F), lambda i, be, nu: (be[i], 0, 0)),
                      pl.BlockSpec((1, D, F), lambda i, be, nu: (be[i], 0, 0)),
                      pl.BlockSpec((1, F, D), lambda i, be, nu: (be[i], 0, 0))],
            out_specs=pl.BlockSpec((MOE_ROWS, D), lambda i, be, nu: (i, 0))),
        compiler_params=_cparams(("arbitrary",), 48),
        name="moe_experts",
    )(block_e, n_used, xg, wg, wu, wd)


def _combine_kernel(x_ref, y0_ref, y1_ref, w_ref, mod_ref, *rest, gate_idx, final):
    if final:
        fg_ref, o_ref = rest
    else:
        (o_ref,) = rest
    y = w_ref[:, 0:1] * y0_ref[...].astype(F32) + w_ref[:, 1:2] * y1_ref[...].astype(F32)
    x = x_ref[...] + mod_ref[0, gate_idx:gate_idx + 1, :] * y
    if final:
        ms = jnp.mean(x * x, axis=-1, keepdims=True)
        x = x * lax.rsqrt(ms + EPS) * fg_ref[...]
    o_ref[...] = x


def moe_combine(x, y0, y1, wts, mod, gate_idx, row_off, n_rows, final_g=None, tm=512):
    D = x.shape[1]
    off_b = row_off // tm
    per_seg = SEG_ROWS // tm
    final = final_g is not None
    row_spec = pl.BlockSpec((tm, D), lambda i: (i + off_b, 0))
    in_specs = [row_spec, row_spec, row_spec,
                pl.BlockSpec((tm, 2), lambda i: (i + off_b, 0)),
                pl.BlockSpec((1, 6, D), lambda i: ((i + off_b) // per_seg, 0, 0))]
    args = [x, y0, y1, wts, mod]
    if final:
        in_specs.append(pl.BlockSpec((1, D), lambda i: (0, 0)))
        args.append(final_g.reshape(1, D))
    return pl.pallas_call(
        functools.partial(_combine_kernel, gate_idx=gate_idx, final=final),
        out_shape=jax.ShapeDtypeStruct((n_rows, D), F32),
        grid=(n_rows // tm,),
        in_specs=in_specs,
        out_specs=pl.BlockSpec((tm, D), lambda i: (i, 0)),
        compiler_params=_cparams(("parallel",), 40),
        name="moe_combine",
    )(*args)


def moe_dispatch_plan(ids, n_tokens):
    n_assign = 2 * n_tokens
    n_blocks = n_assign // MOE_ROWS + N_EXPERTS
    flat_e = ids.T.reshape(n_assign)
    onehot = (flat_e[:, None] == jnp.arange(N_EXPERTS, dtype=I32)[None, :]).astype(I32)
    csum = jnp.cumsum(onehot, axis=0)
    rank = jnp.sum((csum - onehot) * onehot, axis=1)
    counts = csum[-1]
    padded = (counts + MOE_ROWS - 1) // MOE_ROWS * MOE_ROWS
    pend = jnp.cumsum(padded)
    dest = (pend - padded)[flat_e] + rank
    n_used = (pend[-1] // MOE_ROWS).astype(I32).reshape(1)
    block_start = jnp.arange(n_blocks, dtype=I32) * MOE_ROWS
    block_e = jnp.minimum(jnp.sum((pend[None, :] <= block_start[:, None]).astype(I32), axis=1), N_EXPERTS - 1)
    src_tok = jnp.zeros((n_blocks * MOE_ROWS,), I32).at[dest].set(jnp.arange(n_assign, dtype=I32) // 2)
    return dest.reshape(n_tokens, 2), src_tok, block_e, n_used


def channel_mixer(x, g, mod, w_router_t, b_router, wg, wu, wd, splits, final_g=None):
    N = x.shape[0]
    h, ids8, wts8 = moe_router(x, g, mod, w_router_t, b_router, shift_idx=3, scale_idx=4)
    dest, src_tok, block_e, n_used = moe_dispatch_plan(ids8[:2], N)
    xg = jnp.take(h, src_tok, axis=0)
    ybuf = moe_experts(xg, block_e, n_used, wg, wu, wd)
    y0 = jnp.take(ybuf, dest[:, 0], axis=0)
    y1 = jnp.take(ybuf, dest[:, 1], axis=0)
    wts = wts8[:2].T
    return [moe_combine(x, y0, y1, wts, mod, 5, off, n, final_g) for off, n in splits]


def kernel(x_prompt, x_sample, c, cache_a_k, cache_a_v, cache_b_k, cache_b_v, state_C, state_n, state_m,
           c_ctx, norm1_g, norm2_g, w_ada, b_ada, w_in_ab, w_out_ab, rel_bias_a, sink_b, w_in_c, b_gates_c,
           norm_c_g, w_out_c, w_router, b_router, w_gate_e, w_up_e, w_down_e, final_norm_g):
    D = D_MODEL
    Np, Nl = BATCH * SEQ, DEC_BATCH * DEC_SEQ
    N = Np + Nl
    H = C_HEADS
    x = jnp.concatenate([x_prompt.reshape(Np, D), x_sample.reshape(Nl, D)], axis=0)

    cvec = jnp.concatenate([c_ctx[None, :], c, jnp.zeros((8 - 1 - DEC_BATCH, D), F32)], axis=0)
    mod_all = ada_mod_all(cvec, w_ada, b_ada).reshape(DEPTH, 8, 6, D)
    w_router_t = w_router.T.astype(F32)

    outs = {}
    for l in range(DEPTH):
        mod = mod_all[l]
        j = l // 2
        if l % 2 == 0:
            w_in = w_in_ab[j]
            proj_ctx = norm_mod_matmul(x, 0, Np, norm1_g[l], mod, w_in, F32, 0, 1)
            proj_lat = norm_mod_matmul(x, Np, Nl, norm1_g[l], mod, w_in, BF16, 0, 1)
            o_ctx = ctx_attention(proj_ctx, sink_b[j], BATCH, SEQ)
            bias_mask = nat_bias_mask(rel_bias_a[j], DEC_SEQ // GRID_W)
            oa = nat_attention(proj_lat, cache_a_k[:, j].reshape(DEC_BATCH, PAST_LEN, A_WIDTH),
                               cache_a_v[:, j].reshape(DEC_BATCH, PAST_LEN, A_WIDTH), bias_mask,
                               DEC_BATCH, DEC_SEQ, A_HEADS)
            cos, sin = rope_tables(DEC_SEQ)
            qk_rot = rope_heads(proj_lat, 3 * A_HEADS, B_HEADS + B_KV_HEADS, cos, sin, DEC_SEQ)
            ob = swa_attention(qk_rot, proj_lat, 3 * A_HEADS + B_HEADS + B_KV_HEADS,
                               cache_b_k[:, j].reshape(DEC_BATCH, PAST_LEN, B_KV_WIDTH),
                               cache_b_v[:, j].reshape(DEC_BATCH, PAST_LEN, B_KV_WIDTH), sink_b[j],
                               DEC_BATCH, DEC_SEQ)
            o_all = jnp.concatenate([o_ctx, jnp.concatenate([oa, ob], axis=-1)], axis=0)
            x = matmul_gated_residual(o_all, w_out_ab[j].astype(BF16), x, mod, gate_idx=2)
            kv = proj_ctx[:, A_WIDTH:]
            outs.setdefault("ak", []).append(kv[:, :A_WIDTH].reshape(BATCH, SEQ, A_HEADS, HEAD_DIM))
            outs.setdefault("av", []).append(kv[:, A_WIDTH:2 * A_WIDTH].reshape(BATCH, SEQ, A_HEADS, HEAD_DIM))
            kvb = proj_ctx[:, 3 * A_WIDTH + B_Q_WIDTH:]
            outs.setdefault("bk", []).append(kvb[:, :B_KV_WIDTH].reshape(BATCH, SEQ, B_KV_HEADS, HEAD_DIM))
            outs.setdefault("bv", []).append(kvb[:, B_KV_WIDTH:].reshape(BATCH, SEQ, B_KV_HEADS, HEAD_DIM))
        else:
            n_main = 2 * C_QK_WIDTH + 2 * C_V_WIDTH
            w_main = w_in_c[j]
            w_gates = jnp.pad(w_in_c[j][:, n_main:], ((0, 0), (0, LANES - 4 * H)))
            proj, gates = norm_mod_matmul(x, 0, N, norm1_g[l], mod, w_main, BF16, 0, 1, w_aux=w_gates,
                                          n_out=n_main)
            gates_dir = jnp.pad(gates[:, :4 * H].reshape(N, 2, 2, H).transpose(1, 2, 0, 3),
                                ((0, 0), (0, 0), (0, 0), (0, LANES - H)))
            bg = jnp.pad(b_gates_c[j].astype(F32).reshape(2, 2, 1, H), ((0, 0), (0, 0), (0, 0), (0, LANES - H)))
            S = BATCH + DEC_BATCH
            C0 = jnp.concatenate([jnp.zeros((BATCH, 2, H, C_QK_DIM, C_V_DIM), F32), state_C[:, j]], axis=0)
            n0 = jnp.concatenate([jnp.zeros((BATCH, 2, H, C_QK_DIM), F32), state_n[:, j]], axis=0)
            m0 = jnp.concatenate([jnp.zeros((BATCH, 2, H), F32), state_m[:, j]], axis=0)
            S0 = jnp.concatenate([C0, jnp.broadcast_to(n0[..., None], n0.shape + (LANES,))], axis=-1)
            m0 = jnp.broadcast_to(m0[..., None], (S, 2, H, LANES))
            seq_chunks = [SEQ // C_CHUNK] * BATCH + [DEC_SEQ // C_CHUNK] * DEC_BATCH
            h_dir, Sf, mf = mlstm_bidir(proj, gates_dir, bg, S0, m0, seq_chunks)
            x = mlstm_out_residual(h_dir, proj, 2, norm_c_g[j], w_out_c[j].astype(BF16), x, mod, gate_idx=2)
            outs.setdefault("C", []).append(Sf[:BATCH, ..., :C_V_DIM])
            outs.setdefault("n", []).append(Sf[:BATCH, ..., C_V_DIM])
            outs.setdefault("m", []).append(mf[:BATCH, :, :, 0])
        last = l == DEPTH - 1
        pieces = channel_mixer(x, norm2_g[l], mod, w_router_t, b_router,
                               w_gate_e[l].astype(BF16), w_up_e[l].astype(BF16), w_down_e[l].astype(BF16),
                               [(0, Np), (Np, Nl)] if last else [(0, N)],
                               final_norm_g if last else None)
        if last:
            y_prompt = pieces[0].reshape(BATCH, SEQ, D)
            y_sample = pieces[1].reshape(DEC_BATCH, DEC_SEQ, D)
        else:
            x = pieces[0]

    return (y_prompt, y_sample,
            jnp.stack(outs["ak"], axis=1), jnp.stack(outs["av"], axis=1),
            jnp.stack(outs["bk"], axis=1), jnp.stack(outs["bv"], axis=1),
            jnp.stack(outs["C"], axis=1), jnp.stack(outs["n"], axis=1), jnp.stack(outs["m"], axis=1))
```

```python
import functools

import numpy as np
import jax
import jax.numpy as jnp
from jax import lax
from jax.experimental import pallas as pl
from jax.experimental.pallas import tpu as pltpu

F32 = jnp.float32
BF16 = jnp.bfloat16
I32 = jnp.int32

D_MODEL = 2048
BATCH = 16
SEQ = 256
DEPTH = 2
DEC_BATCH = 4
DEC_SEQ = 4096
PAST_LEN = 512
GRID_W = 64
HEAD_DIM = 128
A_HEADS = 8
NA_ROWS = 8
NA_COLS = 16
B_HEADS = 8
B_KV_HEADS = 2
B_WINDOW = 128
ROPE_THETA = 10000.0
C_HEADS = 8
C_QK_DIM = 128
C_V_DIM = 256
C_CHUNK = 128
N_EXPERTS = 16
N_GROUPS = 4
EXPERTS_PER_GROUP = N_EXPERTS // N_GROUPS
D_FF_EXPERT = 1024
EPS = 1e-6

A_WIDTH = A_HEADS * HEAD_DIM
B_Q_WIDTH = B_HEADS * HEAD_DIM
B_KV_WIDTH = B_KV_HEADS * HEAD_DIM
C_QK_WIDTH = C_HEADS * C_QK_DIM
C_V_WIDTH = C_HEADS * C_V_DIM

SEG_ROWS = 4096
NEG_BIG = -1e30
MOE_ROWS = 256
NORM_ROWS = 256
MIB = 1024 * 1024
LANES = 128

_NT = (((1,), (1,)), ((), ()))


def _cparams(sem, vmem_mib):
    return pltpu.CompilerParams(dimension_semantics=sem, vmem_limit_bytes=vmem_mib * MIB)


def _modulated_norm(x, g, mod, shift_idx, scale_idx):
    ms = jnp.mean(x * x, axis=-1, keepdims=True)
    y = x * lax.rsqrt(ms + EPS) * g
    return y * (1.0 + mod[scale_idx:scale_idx + 1, :]) + mod[shift_idx:shift_idx + 1, :]


def _ada_kernel(c_ref, w_ref, b_ref, o_ref):
    c = c_ref[...]
    s = (c * jax.nn.sigmoid(c)).astype(BF16)
    o_ref[0] = jnp.dot(s, w_ref[0].astype(BF16), preferred_element_type=F32) + b_ref[0]


def ada_mod_all(cvec8, w_ada, b_ada, tn=1024):
    L, D, D6 = w_ada.shape
    return pl.pallas_call(
        _ada_kernel,
        out_shape=jax.ShapeDtypeStruct((L, 8, D6), F32),
        grid=(L, D6 // tn),
        in_specs=[pl.BlockSpec((8, D), lambda l, j: (0, 0)),
                  pl.BlockSpec((1, D, tn), lambda l, j: (l, 0, j)),
                  pl.BlockSpec((1, 1, tn), lambda l, j: (l, 0, j))],
        out_specs=pl.BlockSpec((1, 8, tn), lambda l, j: (l, 0, j)),
        compiler_params=_cparams(("parallel", "parallel"), 40),
        name="ada_mod",
    )(cvec8, w_ada, b_ada.reshape(L, 1, D6))


def _nmm_kernel(x_ref, g_ref, mod_ref, w_ref, *rest, shift_idx, scale_idx, has_aux):
    if has_aux:
        waux_ref, o_ref, oaux_ref, h_scr = rest
    else:
        o_ref, h_scr = rest

    @pl.when(pl.program_id(1) == 0)
    def _():
        for r in range(x_ref.shape[0] // NORM_ROWS):
            rows = pl.ds(r * NORM_ROWS, NORM_ROWS)
            h = _modulated_norm(x_ref[rows, :], g_ref[...], mod_ref[0], shift_idx, scale_idx).astype(BF16)
            h_scr[rows, :] = h
            if has_aux:
                oaux_ref[rows, :] = jnp.dot(h, waux_ref[...].astype(BF16), preferred_element_type=F32)

    o_ref[...] = jnp.dot(h_scr[...], w_ref[...].astype(BF16), preferred_element_type=F32).astype(o_ref.dtype)


def norm_mod_matmul(x, row_off, n_rows, g, mod, w, out_dtype, shift_idx, scale_idx, w_aux=None, n_out=None,
                    seg_row_off=None, tm=1024, tn=512):
    D = x.shape[1]
    n_out = w.shape[1] if n_out is None else n_out
    off_b = row_off // tm
    seg_b = off_b if seg_row_off is None else seg_row_off // tm
    per_seg = SEG_ROWS // tm
    has_aux = w_aux is not None
    in_specs = [pl.BlockSpec((tm, D), lambda i, j: (i + off_b, 0)),
                pl.BlockSpec((1, D), lambda i, j: (0, 0)),
                pl.BlockSpec((1, 6, D), lambda i, j: ((i + seg_b) // per_seg, 0, 0)),
                pl.BlockSpec((D, tn), lambda i, j: (0, j))]
    out_shape = jax.ShapeDtypeStruct((n_rows, n_out), out_dtype)
    out_specs = pl.BlockSpec((tm, tn), lambda i, j: (i, j))
    args = [x, g.reshape(1, D), mod, w]
    if has_aux:
        n_aux = w_aux.shape[1]
        in_specs.append(pl.BlockSpec((D, n_aux), lambda i, j: (0, 0)))
        out_shape = (out_shape, jax.ShapeDtypeStruct((n_rows, n_aux), F32))
        out_specs = (out_specs, pl.BlockSpec((tm, n_aux), lambda i, j: (i, 0)))
        args.append(w_aux)
    return pl.pallas_call(
        functools.partial(_nmm_kernel, shift_idx=shift_idx, scale_idx=scale_idx, has_aux=has_aux),
        out_shape=out_shape,
        grid=(n_rows // tm, n_out // tn),
        in_specs=in_specs,
        out_specs=out_specs,
        scratch_shapes=[pltpu.VMEM((tm, D), BF16)],
        compiler_params=_cparams(("parallel", "arbitrary"), 52),
        name="norm_mod_matmul",
    )(*args)


def _attn_out_kernel(oc_ref, oa_ref, ob_ref, w_ref, xc_ref, xl_ref, mod_ref, o_ref, *, gate_idx, n_ctx_tiles):
    i = pl.program_id(0)
    gate = mod_ref[0, gate_idx:gate_idx + 1, :]

    @pl.when(i < n_ctx_tiles)
    def _():
        acc = jnp.dot(oc_ref[...], w_ref[...], preferred_element_type=F32)
        o_ref[...] = xc_ref[...] + gate * acc

    @pl.when(i >= n_ctx_tiles)
    def _():
        a = jnp.concatenate([oa_ref[...], ob_ref[...]], axis=-1)
        acc = jnp.dot(a, w_ref[...], preferred_element_type=F32)
        o_ref[...] = xl_ref[...] + gate * acc


def attn_out_residual(o_ctx, oa, ob, w, x_ctx, x_lat, mod, gate_idx, tm=256):
    n_ctx, K = o_ctx.shape
    n_lat = oa.shape[0]
    D = w.shape[1]
    nct = n_ctx // tm
    per_seg = SEG_ROWS // tm

    def ctx_map(i):
        return (jnp.minimum(i, nct - 1), 0)

    def lat_map(i):
        return (jnp.maximum(i - nct, 0), 0)

    return pl.pallas_call(
        functools.partial(_attn_out_kernel, gate_idx=gate_idx, n_ctx_tiles=nct),
        out_shape=jax.ShapeDtypeStruct((n_ctx + n_lat, D), F32),
        grid=((n_ctx + n_lat) // tm,),
        in_specs=[pl.BlockSpec((tm, K), ctx_map),
                  pl.BlockSpec((tm, oa.shape[1]), lat_map),
                  pl.BlockSpec((tm, ob.shape[1]), lat_map),
                  pl.BlockSpec((K, D), lambda i: (0, 0)),
                  pl.BlockSpec((tm, D), ctx_map),
                  pl.BlockSpec((tm, D), lat_map),
                  pl.BlockSpec((1, 6, D), lambda i: (i // per_seg, 0, 0))],
        out_specs=pl.BlockSpec((tm, D), lambda i: (i, 0)),
        compiler_params=_cparams(("arbitrary",), 48),
        name="attn_out_residual",
    )(o_ctx, oa, ob, w, x_ctx, x_lat, mod)


def _mlstm_out_kernel(hf_ref, hb_ref, o_ref, ng_ref, w_ref, x_ref, mod_ref, out_ref, *, gate_idx):
    hs = hf_ref[0] + hb_ref[0]
    parts = []
    for h in range(C_HEADS):
        sl = slice(h * C_V_DIM, (h + 1) * C_V_DIM)
        xs = hs[:, sl]
        ms = jnp.mean(xs * xs, axis=-1, keepdims=True)
        hn = xs * lax.rsqrt(ms + EPS) * ng_ref[:, sl]
        parts.append((jax.nn.sigmoid(o_ref[:, sl].astype(F32)) * hn).astype(BF16))
    a = jnp.concatenate(parts, axis=-1)
    acc = jnp.dot(a, w_ref[...], preferred_element_type=F32)
    out_ref[...] = x_ref[...] + mod_ref[0, gate_idx:gate_idx + 1, :] * acc


def mlstm_out_residual(h_dir, proj, o_col_block, norm_g, w, x, mod, gate_idx, tm=256):
    n, D = x.shape
    V = C_V_WIDTH
    per_seg = SEG_ROWS // tm
    return pl.pallas_call(
        functools.partial(_mlstm_out_kernel, gate_idx=gate_idx),
        out_shape=jax.ShapeDtypeStruct((n, D), F32),
        grid=(n // tm,),
        in_specs=[pl.BlockSpec((1, tm, V), lambda i: (0, i, 0)),
                  pl.BlockSpec((1, tm, V), lambda i: (1, i, 0)),
                  pl.BlockSpec((tm, V), lambda i: (i, o_col_block)),
                  pl.BlockSpec((1, V), lambda i: (0, 0)),
                  pl.BlockSpec((V, D), lambda i: (0, 0)),
                  pl.BlockSpec((tm, D), lambda i: (i, 0)),
                  pl.BlockSpec((1, 6, D), lambda i: (i // per_seg, 0, 0))],
        out_specs=pl.BlockSpec((tm, D), lambda i: (i, 0)),
        compiler_params=_cparams(("parallel",), 48),
        name="mlstm_out_residual",
    )(h_dir, h_dir, proj, norm_g.reshape(1, V), w, x, mod)


def _ctx_attn_kernel(sink_ref, q_ref, k_ref, v_ref, o_ref):
    h = pl.program_id(1)
    q = q_ref[...].astype(BF16)
    k = k_ref[...].astype(BF16)
    v = v_ref[...].astype(BF16)
    s = lax.dot_general(q, k, _NT, preferred_element_type=F32) * (HEAD_DIM ** -0.5)
    sk = sink_ref[h]
    m = jnp.maximum(jnp.max(s, axis=-1, keepdims=True), sk)
    p = jnp.exp(s - m)
    l = jnp.sum(p, axis=-1, keepdims=True) + jnp.exp(sk - m)
    o = jnp.dot(p.astype(BF16), v, preferred_element_type=F32) / l
    o_ref[...] = o.astype(o_ref.dtype)


def ctx_attention(proj, sink_b, n_batch, seq):
    n_heads = A_HEADS + B_HEADS
    group = B_HEADS // B_KV_HEADS
    qb0 = 3 * A_HEADS
    kb0 = qb0 + B_HEADS
    vb0 = kb0 + B_KV_HEADS
    sinks = jnp.concatenate([jnp.full((A_HEADS,), NEG_BIG, F32), sink_b.astype(F32)])

    def q_map(b, h, s):
        return (b, jnp.where(h < A_HEADS, h, qb0 + h - A_HEADS))

    def k_map(b, h, s):
        return (b, jnp.where(h < A_HEADS, A_HEADS + h, kb0 + (h - A_HEADS) // group))

    def v_map(b, h, s):
        return (b, jnp.where(h < A_HEADS, 2 * A_HEADS + h, vb0 + (h - A_HEADS) // group))

    blk = (seq, HEAD_DIM)
    return pl.pallas_call(
        _ctx_attn_kernel,
        out_shape=jax.ShapeDtypeStruct((n_batch * seq, n_heads * HEAD_DIM), BF16),
        grid_spec=pltpu.PrefetchScalarGridSpec(
            num_scalar_prefetch=1,
            grid=(n_batch, n_heads),
            in_specs=[pl.BlockSpec(blk, q_map), pl.BlockSpec(blk, k_map), pl.BlockSpec(blk, v_map)],
            out_specs=pl.BlockSpec(blk, lambda b, h, s: (b, h))),
        compiler_params=_cparams(("parallel", "parallel"), 32),
        name="ctx_attention",
    )(sinks, proj, proj, proj)


NAT_QROWS = 4


def nat_bias_mask(rel_bias, rows):
    W = GRID_W
    nb = rows // NAT_QROWS
    kh = min(NA_ROWS, rows)
    n_dr, n_dc = 2 * NA_ROWS - 1, 2 * NA_COLS - 1
    H = rel_bias.shape[0]
    cidx = np.clip(np.arange(W)[None, :] - np.arange(W)[:, None] + NA_COLS - 1, 0, n_dc - 1)
    onehot = jnp.asarray((cidx.reshape(1, W * W) == np.arange(n_dc)[:, None]).astype(np.float32))
    col_bias = jnp.dot(rel_bias.astype(F32).reshape(H * n_dr, n_dc), onehot,
                       precision=lax.Precision.HIGHEST).reshape(H, n_dr, W, W)
    q_rows = []
    for qi in range(NAT_QROWS):
        tiles = [col_bias[:, int(np.clip(NAT_QROWS * (kj - 1) + kjr - qi + NA_ROWS - 1, 0, n_dr - 1))]
                 for kj in range(3) for kjr in range(NAT_QROWS)]
        q_rows.append(jnp.concatenate(tiles, axis=-1))
    bias = jnp.concatenate(q_rows, axis=1)
    variants = []
    for g in (0, 1, nb - 1):
        i = np.arange(NAT_QROWS)[:, None, None, None, None]
        qc = np.arange(W)[None, :, None, None, None]
        j = np.arange(3)[None, None, :, None, None]
        jr = np.arange(NAT_QROWS)[None, None, None, :, None]
        kc = np.arange(W)[None, None, None, None, :]
        r = NAT_QROWS * g + i
        kblk = g - 1 + j
        kr = NAT_QROWS * kblk + jr
        rs = np.clip(r - kh // 2, 0, rows - kh)
        row_ok = (kblk >= 0) & (kblk < nb) & (kr >= rs) & (kr < rs + kh)
        cstart = np.clip(qc - NA_COLS // 2, 0, W - NA_COLS)
        col_ok = (kc >= cstart) & (kc < cstart + NA_COLS)
        ok = np.broadcast_to(row_ok & col_ok, (NAT_QROWS, W, 3, NAT_QROWS, W))
        n_q, n_k = NAT_QROWS * W, 3 * NAT_QROWS * W
        variants.append(jnp.where(jnp.asarray(ok.reshape(1, n_q, n_k)), bias, NEG_BIG))
    return jnp.stack(variants, axis=0)


def _nat_kernel(q_ref, k0_ref, k1_ref, k2_ref, v0_ref, v1_ref, v2_ref, kc_ref, vc_ref, bm_ref, o_ref):
    scale = HEAD_DIM ** -0.5
    q = q_ref[...]
    tq = q.shape[0]
    kc = kc_ref[0].astype(BF16)
    vc = vc_ref[0].astype(BF16)
    s_lat = [lax.dot_general(q, k_ref[...], _NT, preferred_element_type=F32) * scale
             + bm_ref[0, 0, :, j * tq:(j + 1) * tq]
             for j, k_ref in enumerate((k0_ref, k1_ref, k2_ref))]
    s_ctx = lax.dot_general(q, kc, _NT, preferred_element_type=F32) * scale
    m = jnp.max(s_ctx, axis=-1, keepdims=True)
    for s in s_lat:
        m = jnp.maximum(m, jnp.max(s, axis=-1, keepdims=True))
    p_ctx = jnp.exp(s_ctx - m)
    l = jnp.sum(p_ctx, axis=-1, keepdims=True)
    acc = jnp.dot(p_ctx.astype(BF16), vc, preferred_element_type=F32)
    for s, v_ref in zip(s_lat, (v0_ref, v1_ref, v2_ref)):
        p = jnp.exp(s - m)
        l = l + jnp.sum(p, axis=-1, keepdims=True)
        acc = acc + jnp.dot(p.astype(BF16), v_ref[...], preferred_element_type=F32)
    o_ref[...] = (acc / l).astype(o_ref.dtype)


def nat_attention(proj, cache_k, cache_v, bias_mask, n_batch, T, n_heads):
    tq = NAT_QROWS * GRID_W
    nb = T // tq
    P = cache_k.shape[1]

    def kv_map(col0, j):
        return lambda b, h, g: (b * nb + jnp.clip(g - 1 + j, 0, nb - 1), col0 + h)

    blk = (tq, HEAD_DIM)
    in_specs = [pl.BlockSpec(blk, lambda b, h, g: (b * nb + g, h))]
    in_specs += [pl.BlockSpec(blk, kv_map(n_heads, j)) for j in range(3)]
    in_specs += [pl.BlockSpec(blk, kv_map(2 * n_heads, j)) for j in range(3)]
    in_specs += [pl.BlockSpec((1, P, HEAD_DIM), lambda b, h, g: (b, 0, h))] * 2
    in_specs += [pl.BlockSpec((1, 1, tq, 3 * tq),
                              lambda b, h, g: (jnp.where(g == 0, 0, jnp.where(g == nb - 1, 2, 1)), h, 0, 0))]
    return pl.pallas_call(
        _nat_kernel,
        out_shape=jax.ShapeDtypeStruct((n_batch * T, n_heads * HEAD_DIM), BF16),
        grid=(n_batch, n_heads, nb),
        in_specs=in_specs,
        out_specs=pl.BlockSpec(blk, lambda b, h, g: (b * nb + g, h)),
        compiler_params=_cparams(("parallel", "parallel", "arbitrary"), 32),
        name="nat_attention",
    )(proj, proj, proj, proj, proj, proj, proj, cache_k, cache_v, bias_mask)


def rope_tables(T):
    t = jnp.arange(T)
    row = (t // GRID_W).astype(F32)
    col = (t % GRID_W).astype(F32)
    nf = HEAD_DIM // 4
    freqs = ROPE_THETA ** (-jnp.arange(nf, dtype=F32) / nf)
    ar = row[:, None] * freqs
    ac = col[:, None] * freqs
    cos = jnp.concatenate([jnp.cos(ar), jnp.cos(ar), jnp.cos(ac), jnp.cos(ac)], axis=-1)
    sin = jnp.concatenate([-jnp.sin(ar), jnp.sin(ar), -jnp.sin(ac), jnp.sin(ac)], axis=-1)
    return cos, sin


def _rope_kernel(q_ref, k_ref, cos_ref, sin_ref, o_ref):
    nf = HEAD_DIM // 4
    cos = cos_ref[...]
    sin = sin_ref[...]
    lane = lax.broadcasted_iota(I32, cos.shape, 1)
    first_half = (lane & nf) == 0
    col = 0
    for src in (q_ref, k_ref):
        for h in range(src.shape[1] // HEAD_DIM):
            x = src[:, h * HEAD_DIM:(h + 1) * HEAD_DIM].astype(F32)
            upper = pltpu.roll(x, HEAD_DIM - nf, 1)
            lower = pltpu.roll(x, nf, 1)
            partner = jnp.where(first_half, upper, lower)
            o_ref[:, col:col + HEAD_DIM] = (x * cos + partner * sin).astype(o_ref.dtype)
            col += HEAD_DIM


def rope_heads(proj, q_col0, k_col0, cos, sin, T, tm=512):
    n = proj.shape[0]
    per_seq = T // tm
    return pl.pallas_call(
        _rope_kernel,
        out_shape=jax.ShapeDtypeStruct((n, B_Q_WIDTH + B_KV_WIDTH), BF16),
        grid=(n // tm,),
        in_specs=[pl.BlockSpec((tm, B_Q_WIDTH), lambda i: (i, q_col0 // B_Q_WIDTH)),
                  pl.BlockSpec((tm, B_KV_WIDTH), lambda i: (i, k_col0 // B_KV_WIDTH)),
                  pl.BlockSpec((tm, HEAD_DIM), lambda i: (i % per_seq, 0)),
                  pl.BlockSpec((tm, HEAD_DIM), lambda i: (i % per_seq, 0))],
        out_specs=pl.BlockSpec((tm, B_Q_WIDTH + B_KV_WIDTH), lambda i: (i, 0)),
        compiler_params=_cparams(("parallel",), 32),
        name="rope_heads",
    )(proj, proj, cos, sin)


SWA_TQ = 2 * B_WINDOW


def _swa_kernel(sink_ref, q_ref, k0_ref, k1_ref, k2_ref, k3_ref, v0_ref, v1_ref, v2_ref, v3_ref,
                kc_ref, vc_ref, o_ref, *, T):
    scale = HEAD_DIM ** -0.5
    group = B_HEADS // B_KV_HEADS
    kvh = pl.program_id(1)
    n = pl.program_id(2)
    k = jnp.concatenate([k0_ref[...], k1_ref[...], k2_ref[...], k3_ref[...]], axis=0)
    v = jnp.concatenate([v0_ref[...], v1_ref[...], v2_ref[...], v3_ref[...]], axis=0)
    kc = kc_ref[0].astype(BF16)
    vc = vc_ref[0].astype(BF16)
    nk = k.shape[0]
    qpos = n * SWA_TQ + lax.broadcasted_iota(I32, (SWA_TQ, nk), 0)
    kpos = n * SWA_TQ - B_WINDOW + lax.broadcasted_iota(I32, (SWA_TQ, nk), 1)
    dist = jnp.abs(qpos - kpos)
    ok = jnp.where(kpos >= 0, jnp.where(kpos < T, dist, B_WINDOW + 1), B_WINDOW + 1) <= B_WINDOW
    outs = []
    for gi in range(group):
        q = q_ref[:, gi * HEAD_DIM:(gi + 1) * HEAD_DIM]
        s_lat = jnp.where(ok, lax.dot_general(q, k, _NT, preferred_element_type=F32) * scale, NEG_BIG)
        s_ctx = lax.dot_general(q, kc, _NT, preferred_element_type=F32) * scale
        sk = sink_ref[kvh * group + gi]
        m = jnp.maximum(jnp.maximum(jnp.max(s_lat, axis=-1, keepdims=True),
                                    jnp.max(s_ctx, axis=-1, keepdims=True)), sk)
        p_lat = jnp.exp(s_lat - m)
        p_ctx = jnp.exp(s_ctx - m)
        l = (jnp.sum(p_lat, axis=-1, keepdims=True) + jnp.sum(p_ctx, axis=-1, keepdims=True)
             + jnp.exp(sk - m))
        acc = (jnp.dot(p_lat.astype(BF16), v, preferred_element_type=F32)
               + jnp.dot(p_ctx.astype(BF16), vc, preferred_element_type=F32))
        outs.append((acc / l).astype(o_ref.dtype))
    o_ref[...] = jnp.concatenate(outs, axis=-1)


def swa_attention(qk_rot, proj, v_col0, cache_k, cache_v, sink, n_batch, T):
    group = B_HEADS // B_KV_HEADS
    nq = T // SWA_TQ
    nkb = T // B_WINDOW
    P = cache_k.shape[1]

    def kv_map(col0, j):
        return lambda b, kvh, n, s: (b * nkb + jnp.clip(2 * n - 1 + j, 0, nkb - 1), col0 + kvh)

    kblk = (B_WINDOW, HEAD_DIM)
    in_specs = [pl.BlockSpec((SWA_TQ, group * HEAD_DIM), lambda b, kvh, n, s: (b * nq + n, kvh))]
    in_specs += [pl.BlockSpec(kblk, kv_map(B_HEADS, j)) for j in range(4)]
    in_specs += [pl.BlockSpec(kblk, kv_map(v_col0, j)) for j in range(4)]
    in_specs += [pl.BlockSpec((1, P, HEAD_DIM), lambda b, kvh, n, s: (b, 0, kvh))] * 2
    return pl.pallas_call(
        functools.partial(_swa_kernel, T=T),
        out_shape=jax.ShapeDtypeStruct((n_batch * T, B_Q_WIDTH), BF16),
        grid_spec=pltpu.PrefetchScalarGridSpec(
            num_scalar_prefetch=1,
            grid=(n_batch, B_KV_HEADS, nq),
            in_specs=in_specs,
            out_specs=pl.BlockSpec((SWA_TQ, group * HEAD_DIM), lambda b, kvh, n, s: (b * nq + n, kvh))),
        compiler_params=_cparams(("parallel", "parallel", "arbitrary"), 32),
        name="swa_attention",
    )(sink.astype(F32), qk_rot, qk_rot, qk_rot, qk_rot, qk_rot, proj, proj, proj, proj, cache_k, cache_v)


def _mlstm_kernel(rowblk_ref, seq_ref, first_ref, last_ref,
                  q_ref, k_ref, v_ref, g_ref, bg_ref, S0_ref, m0_ref,
                  h_ref, Sf_ref, mf_ref, S_scr, m_scr):
    H, DK, DV, L, R = C_HEADS, C_QK_DIM, C_V_DIM, C_CHUNK, LANES
    d = pl.program_id(0)
    s = pl.program_id(1)
    scale = DK ** -0.5

    @pl.when(first_ref[s] == 1)
    def _():
        S_scr[...] = S0_ref[0, 0]
        m_scr[...] = m0_ref[0, 0]

    gi = g_ref[0, 0] + bg_ref[0, 0]
    lf = jax.nn.log_sigmoid(g_ref[0, 1] + bg_ref[0, 1])
    row = lax.broadcasted_iota(I32, (L, L), 0)
    col = lax.broadcasted_iota(I32, (L, L), 1)
    causal = (row - col) * (1 - 2 * d) >= 0
    b_all = jnp.dot(causal.astype(F32), lf, precision=lax.Precision.HIGHEST,
                    preferred_element_type=F32)
    a_row = (gi - b_all).T[0:H, :]
    b_row = b_all.T[0:H, :]
    lane = lax.broadcasted_iota(I32, (H, L), 1)
    cm = a_row
    k = 1
    while k < L:
        fwd = jnp.where(lane >= k, pltpu.roll(cm, k, 1), -jnp.inf)
        bwd = jnp.where(lane < L - k, pltpu.roll(cm, L - k, 1), -jnp.inf)
        cm = jnp.maximum(cm, jnp.where(d == 0, fwd, bwd))
        k *= 2
    m_all = m_scr[...]
    M_row = jnp.maximum(m_all, cm)
    wi_row = jnp.exp(m_all - M_row)
    emt_row = jnp.exp(-(b_row + M_row))
    M_last = jnp.where(d == 0, M_row[:, L - 1:L], M_row[:, 0:1])
    b_last = jnp.where(d == 0, b_row[:, L - 1:L], b_row[:, 0:1])
    ws_row = jnp.exp(a_row - M_last)
    wc_all = jnp.exp(m_all - M_last)
    m_scr[...] = jnp.broadcast_to(b_last + M_last, m_all.shape)
    cols = jnp.concatenate([M_row, wi_row, emt_row, jnp.zeros((L - 3 * H, L), F32)], axis=0).T
    ones = jnp.ones((L, R), BF16)
    for h in range(H):
        M_col = cols[:, h:h + 1]
        wi_col = cols[:, H + h:H + h + 1]
        emt_col = cols[:, 2 * H + h:2 * H + h + 1]
        qh = q_ref[:, h * DK:(h + 1) * DK]
        kh = k_ref[:, h * DK:(h + 1) * DK]
        vh = v_ref[:, h * DV:(h + 1) * DV]
        v_ext = jnp.concatenate([vh, ones], axis=-1)
        Sh = S_scr[h]

        w = jnp.where(causal, jnp.exp(a_row[h:h + 1, :] - M_col), 0.0)
        sqk = lax.dot_general(qh, kh, _NT, preferred_element_type=F32) * scale * w
        S_hi = Sh.astype(BF16)
        n_lo = (Sh[:, DV:] - S_hi[:, DV:].astype(F32)).astype(BF16)
        inter = jnp.dot(qh, jnp.concatenate([S_hi, n_lo], axis=-1),
                        preferred_element_type=F32) * scale
        qn = inter[:, DV:DV + 1] + inter[:, DV + R:DV + R + 1]
        num = jnp.dot(sqk.astype(BF16), vh, preferred_element_type=F32) + wi_col * inter[:, :DV]
        den = jnp.sum(sqk, axis=-1, keepdims=True) + wi_col * qn
        h_ref[0, :, h * DV:(h + 1) * DV] = num / jnp.maximum(jnp.abs(den), emt_col)

        kwT = kh.astype(F32).T * ws_row[h:h + 1, :]
        kwT_hi = kwT.astype(BF16)
        kwT_lo = (kwT - kwT_hi.astype(F32)).astype(BF16)
        upd = jnp.dot(kwT_hi, v_ext, preferred_element_type=F32)
        upd_n = upd[:, DV:] + jnp.dot(kwT_lo, ones, preferred_element_type=F32)
        wc = jnp.concatenate([wc_all[h:h + 1, :]] * (DV // R + 1), axis=-1)
        S_scr[h] = wc * Sh + jnp.concatenate([upd[:, :DV], upd_n], axis=-1)

    @pl.when(last_ref[s] == 1)
    def _():
        Sf_ref[0, 0] = S_scr[...]
        mf_ref[0, 0] = m_scr[...]


def mlstm_bidir(proj, gates_dir, b_gates_dir, S0, m0, seq_chunks):
    H, DK, DV, L = C_HEADS, C_QK_DIM, C_V_DIM, C_CHUNK
    DS = DV + LANES
    N = proj.shape[0]
    S = len(seq_chunks)
    rowblk, seq_id, first, last = [[], []], [], [], []
    base = 0
    for sq, nc in enumerate(seq_chunks):
        rowblk[0] += [base + c for c in range(nc)]
        rowblk[1] += [base + nc - 1 - c for c in range(nc)]
        seq_id += [sq] * nc
        first += [1] + [0] * (nc - 1)
        last += [0] * (nc - 1) + [1]
        base += nc
    n_steps = base
    rowblk = jnp.asarray(np.array(rowblk, np.int32).reshape(-1))
    tables = (rowblk, jnp.asarray(seq_id, I32), jnp.asarray(first, I32), jnp.asarray(last, I32))

    def row_map(colblk):
        return lambda d, s, rb, sq, fi, la: (rb[d * n_steps + s], colblk)

    def state_map(nd):
        return lambda d, s, rb, sq, fi, la: (sq[s], d) + (0,) * nd

    in_specs = [pl.BlockSpec((L, H * DK), row_map(0)),
                pl.BlockSpec((L, H * DK), row_map(1)),
                pl.BlockSpec((L, H * DV), row_map(1)),
                pl.BlockSpec((1, 2, L, LANES), lambda d, s, rb, sq, fi, la: (d, 0, rb[d * n_steps + s], 0)),
                pl.BlockSpec((1, 2, 1, LANES), lambda d, s, rb, sq, fi, la: (d, 0, 0, 0)),
                pl.BlockSpec((1, 1, H, DK, DS), state_map(3)),
                pl.BlockSpec((1, 1, H, LANES), state_map(2))]
    out_specs = (pl.BlockSpec((1, L, H * DV), lambda d, s, rb, sq, fi, la: (d, rb[d * n_steps + s], 0)),
                 pl.BlockSpec((1, 1, H, DK, DS), state_map(3)),
                 pl.BlockSpec((1, 1, H, LANES), state_map(2)))
    out_shape = (jax.ShapeDtypeStruct((2, N, H * DV), F32),
                 jax.ShapeDtypeStruct((S, 2, H, DK, DS), F32),
                 jax.ShapeDtypeStruct((S, 2, H, LANES), F32))
    return pl.pallas_call(
        _mlstm_kernel,
        out_shape=out_shape,
        grid_spec=pltpu.PrefetchScalarGridSpec(
            num_scalar_prefetch=4,
            grid=(2, n_steps),
            in_specs=in_specs,
            out_specs=out_specs,
            scratch_shapes=[pltpu.VMEM((H, DK, DS), F32), pltpu.VMEM((H, LANES), F32)]),
        compiler_params=_cparams(("arbitrary", "arbitrary"), 40),
        name="mlstm_bidir",
    )(*tables, proj, proj, proj, gates_dir, b_gates_dir, S0, m0)


def _top2_of4(vals):
    m1, i1 = vals[0], jnp.zeros(vals[0].shape, I32)
    for j in range(1, 4):
        better = vals[j] > m1
        m1 = jnp.where(better, vals[j], m1)
        i1 = jnp.where(better, j, i1)
    m2, i2 = jnp.full(vals[0].shape, -jnp.inf, F32), jnp.zeros(vals[0].shape, I32)
    for j in range(4):
        cand = jnp.where(i1 == j, -jnp.inf, vals[j])
        better = cand > m2
        m2 = jnp.where(better, cand, m2)
        i2 = jnp.where(better, j, i2)
    return m1, i1, m2, i2


def _router_kernel(x_ref, g_ref, mod_ref, wr_ref, br_ref, h_ref, ids_ref, wts_ref, *, shift_idx, scale_idx):
    h = _modulated_norm(x_ref[...], g_ref[...], mod_ref[0], shift_idx, scale_idx)
    h_ref[...] = h.astype(BF16)
    logits = lax.dot_general(wr_ref[...], h, _NT, precision=lax.Precision.HIGHEST,
                             preferred_element_type=F32)
    aff = jax.nn.sigmoid(logits)
    sel = aff + br_ref[...]
    aff_rows = [aff[e:e + 1, :] for e in range(N_EXPERTS)]
    sel_rows = [sel[e:e + 1, :] for e in range(N_EXPERTS)]
    tops = [_top2_of4(sel_rows[4 * gidx:4 * gidx + 4]) for gidx in range(N_GROUPS)]
    best = tops[0][0] + tops[0][2]
    grp = jnp.zeros(best.shape, I32)
    i1, i2 = tops[0][1], tops[0][3]
    for gidx in range(1, N_GROUPS):
        score = tops[gidx][0] + tops[gidx][2]
        better = score > best
        best = jnp.where(better, score, best)
        grp = jnp.where(better, gidx, grp)
        i1 = jnp.where(better, tops[gidx][1], i1)
        i2 = jnp.where(better, tops[gidx][3], i2)
    e1 = grp * EXPERTS_PER_GROUP + i1
    e2 = grp * EXPERTS_PER_GROUP + i2
    w1 = jnp.zeros(best.shape, F32)
    w2 = jnp.zeros(best.shape, F32)
    for e in range(N_EXPERTS):
        w1 = jnp.where(e1 == e, aff_rows[e], w1)
        w2 = jnp.where(e2 == e, aff_rows[e], w2)
    tot = w1 + w2
    ids_ref[...] = jnp.zeros(ids_ref.shape, I32)
    wts_ref[...] = jnp.zeros(wts_ref.shape, F32)
    ids_ref[0:1, :] = e1
    ids_ref[1:2, :] = e2
    wts_ref[0:1, :] = w1 / tot
    wts_ref[1:2, :] = w2 / tot


def moe_router(x, row_off, n_rows, g, mod, w_router_t, b_router, shift_idx, scale_idx, tm=512):
    D = x.shape[1]
    off_b = row_off // tm
    per_seg = SEG_ROWS // tm
    return pl.pallas_call(
        functools.partial(_router_kernel, shift_idx=shift_idx, scale_idx=scale_idx),
        out_shape=(jax.ShapeDtypeStruct((n_rows, D), BF16),
                   jax.ShapeDtypeStruct((8, n_rows), I32),
                   jax.ShapeDtypeStruct((8, n_rows), F32)),
        grid=(n_rows // tm,),
        in_specs=[pl.BlockSpec((tm, D), lambda i: (i + off_b, 0)),
                  pl.BlockSpec((1, D), lambda i: (0, 0)),
                  pl.BlockSpec((1, 6, D), lambda i: ((i + off_b) // per_seg, 0, 0)),
                  pl.BlockSpec((N_EXPERTS, D), lambda i: (0, 0)),
                  pl.BlockSpec((N_EXPERTS, 1), lambda i: (0, 0))],
        out_specs=(pl.BlockSpec((tm, D), lambda i: (i, 0)),
                   pl.BlockSpec((8, tm), lambda i: (0, i)),
                   pl.BlockSpec((8, tm), lambda i: (0, i))),
        compiler_params=_cparams(("parallel",), 40),
        name="moe_router",
    )(x, g.reshape(1, D), mod, w_router_t, b_router.reshape(N_EXPERTS, 1).astype(F32))


MOE_CAST_ROWS = 128


def _moe_expert_kernel(be_ref, first_ref, next_ref, nu_ref, x_ref, wg_hbm, wu_hbm, wd_hbm, y_ref,
                       wg_st, wu_st, wd_st, wg_bf, wu_bf, wd_bf, sems, *, e_off):
    i = pl.program_id(0)

    def weight_copies(e):
        return (pltpu.make_async_copy(wg_hbm.at[e_off + e], wg_st, sems.at[0]),
                pltpu.make_async_copy(wu_hbm.at[e_off + e], wu_st, sems.at[1]),
                pltpu.make_async_copy(wd_hbm.at[e_off + e], wd_st, sems.at[2]))

    @pl.when(i == 0)
    def _():
        for cp in weight_copies(be_ref[0]):
            cp.start()

    @pl.when(first_ref[i] == 1)
    def _():
        for cp in weight_copies(be_ref[i]):
            cp.wait()
        for st, bf in ((wg_st, wg_bf), (wu_st, wu_bf), (wd_st, wd_bf)):
            def cast_rows(c, carry, st=st, bf=bf):
                r = pl.multiple_of(c * MOE_CAST_ROWS, MOE_CAST_ROWS)
                bf[pl.ds(r, MOE_CAST_ROWS), :] = st[pl.ds(r, MOE_CAST_ROWS), :].astype(BF16)
                return carry
            lax.fori_loop(0, st.shape[0] // MOE_CAST_ROWS, cast_rows, 0)

        @pl.when(next_ref[i] >= 0)
        def _():
            for cp in weight_copies(next_ref[i]):
                cp.start()

    @pl.when(i < nu_ref[0])
    def _():
        x = x_ref[...]
        gate = jnp.dot(x, wg_bf[...], preferred_element_type=F32)
        up = jnp.dot(x, wu_bf[...], preferred_element_type=F32)
        act = (gate * jax.nn.sigmoid(gate) * up).astype(BF16)
        y_ref[...] = jnp.dot(act, wd_bf[...], preferred_element_type=F32).astype(y_ref.dtype)

    @pl.when(i >= nu_ref[0])
    def _():
        y_ref[...] = jnp.zeros(y_ref.shape, y_ref.dtype)


def moe_experts(xg, plan, wg, wu, wd, e_off):
    R, D = xg.shape
    F = wg.shape[2]
    nb = R // MOE_ROWS
    hbm = pl.BlockSpec(memory_space=pl.ANY)
    return pl.pallas_call(
        functools.partial(_moe_expert_kernel, e_off=e_off),
        out_shape=jax.ShapeDtypeStruct((R, D), BF16),
        grid_spec=pltpu.PrefetchScalarGridSpec(
            num_scalar_prefetch=4,
            grid=(nb,),
            in_specs=[pl.BlockSpec((MOE_ROWS, D), lambda i, *_: (i, 0)), hbm, hbm, hbm],
            out_specs=pl.BlockSpec((MOE_ROWS, D), lambda i, *_: (i, 0)),
            scratch_shapes=[pltpu.VMEM((D, F), F32), pltpu.VMEM((D, F), F32), pltpu.VMEM((F, D), F32),
                            pltpu.VMEM((D, F), BF16), pltpu.VMEM((D, F), BF16), pltpu.VMEM((F, D), BF16),
                            pltpu.SemaphoreType.DMA((3,))]),
        compiler_params=_cparams(("arbitrary",), 56),
        name="moe_experts",
    )(plan["block_e"], plan["first"], plan["next_e"], plan["n_used"], xg, wg, wu, wd)


def _combine_kernel(x_ref, mod_ref, *rest, gate_idx, final, tile_off, chunk_tiles):
    rest = list(rest)
    fg_ref = rest.pop(0) if final else None
    o_ref = rest.pop()
    t = pl.program_id(0) + tile_off
    for c, (lo, n) in enumerate(chunk_tiles):
        y0_ref, y1_ref, w_ref = rest[3 * c:3 * c + 3]

        @pl.when((t >= lo) & (t < lo + n))
        def _(y0_ref=y0_ref, y1_ref=y1_ref, w_ref=w_ref):
            y = w_ref[:, 0:1] * y0_ref[...].astype(F32) + w_ref[:, 1:2] * y1_ref[...].astype(F32)
            x = x_ref[...] + mod_ref[0, gate_idx:gate_idx + 1, :] * y
            if final:
                ms = jnp.mean(x * x, axis=-1, keepdims=True)
                x = x * lax.rsqrt(ms + EPS) * fg_ref[...]
            o_ref[...] = x


def moe_combine(x, row_off, n_rows, parts, mod, gate_idx, final_g=None, tm=512):
    D = x.shape[1]
    xb = row_off // tm
    per_seg = SEG_ROWS // tm
    final = final_g is not None
    in_specs = [pl.BlockSpec((tm, D), lambda i: (i + xb, 0)),
                pl.BlockSpec((1, 6, D), lambda i: ((i + xb) // per_seg, 0, 0))]
    args = [x, mod]
    if final:
        in_specs.append(pl.BlockSpec((1, D), lambda i: (0, 0)))
        args.append(final_g.reshape(1, D))
    chunk_tiles = []
    for c_off, c_n, y0, y1, wts in parts:
        lo, n = c_off // tm, c_n // tm
        chunk_tiles.append((lo, n))

        def chunk_map(i, lo=lo, n=n):
            return (jnp.clip(i + xb - lo, 0, n - 1), 0)

        in_specs += [pl.BlockSpec((tm, D), chunk_map), pl.BlockSpec((tm, D), chunk_map),
                     pl.BlockSpec((tm, 2), chunk_map)]
        args += [y0, y1, wts]
    return pl.pallas_call(
        functools.partial(_combine_kernel, gate_idx=gate_idx, final=final, tile_off=xb,
                          chunk_tiles=tuple(chunk_tiles)),
        out_shape=jax.ShapeDtypeStruct((n_rows, D), F32),
        grid=(n_rows // tm,),
        in_specs=in_specs,
        out_specs=pl.BlockSpec((tm, D), lambda i: (i, 0)),
        compiler_params=_cparams(("arbitrary",), 48),
        name="moe_combine",
    )(*args)


def moe_dispatch_plan(ids, n_tokens):
    n_assign = 2 * n_tokens
    n_blocks = n_assign // MOE_ROWS + N_EXPERTS
    experts = jnp.arange(N_EXPERTS, dtype=I32)
    flat_e = ids.T.reshape(n_assign)
    onehot = (flat_e[:, None] == experts[None, :]).astype(I32)
    csum = jnp.cumsum(onehot, axis=0)
    rank = jnp.sum((csum - onehot) * onehot, axis=1)
    counts = csum[-1]
    padded = (counts + MOE_ROWS - 1) // MOE_ROWS * MOE_ROWS
    pend = jnp.cumsum(padded)
    dest = (pend - padded)[flat_e] + rank
    n_used = (pend[-1] // MOE_ROWS).astype(I32)
    block_idx = jnp.arange(n_blocks, dtype=I32)
    block_e = jnp.minimum(jnp.sum((pend[None, :] <= block_idx[:, None] * MOE_ROWS).astype(I32), axis=1),
                          N_EXPERTS - 1)
    prev_e = jnp.concatenate([jnp.full((1,), -1, I32), block_e[:-1]])
    first = ((block_idx < n_used) & (block_e != prev_e)).astype(I32)
    later = (experts[None, :] > experts[:, None]) & (counts[None, :] > 0)
    next_of = jnp.min(jnp.where(later, experts[None, :], N_EXPERTS), axis=1)
    next_of = jnp.where(next_of == N_EXPERTS, -1, next_of)
    next_e = jnp.sum(jnp.where(block_e[:, None] == experts[None, :], next_of[None, :], 0), axis=1).astype(I32)
    src_tok = jnp.zeros((n_blocks * MOE_ROWS,), I32).at[dest].set(jnp.arange(n_assign, dtype=I32) // 2)
    plan = {"block_e": block_e, "first": first, "next_e": next_e, "n_used": n_used.reshape(1)}
    return dest.reshape(n_tokens, 2), src_tok, plan


def channel_mixer(x, g, mod, w_router_t, b_router, wg, wu, wd, e_off, chunks, outputs, final_g=None):
    parts = []
    for c_off, c_n in chunks:
        h, ids8, wts8 = moe_router(x, c_off, c_n, g, mod, w_router_t, b_router, shift_idx=3, scale_idx=4)
        dest, src_tok, plan = moe_dispatch_plan(ids8[:2], c_n)
        xg = jnp.take(h, src_tok, axis=0, mode="clip")
        ybuf = moe_experts(xg, plan, wg, wu, wd, e_off)
        y0 = jnp.take(ybuf, dest[:, 0], axis=0, mode="clip")
        y1 = jnp.take(ybuf, dest[:, 1], axis=0, mode="clip")
        parts.append((c_off, c_n, y0, y1, wts8[:2].T))
    return [moe_combine(x, o_off, o_n, parts, mod, 5, final_g) for o_off, o_n in outputs]


def kernel(x_prompt, x_sample, c, cache_a_k, cache_a_v, cache_b_k, cache_b_v, state_C, state_n, state_m,
           c_ctx, norm1_g, norm2_g, w_ada, b_ada, w_in_ab, w_out_ab, rel_bias_a, sink_b, w_in_c, b_gates_c,
           norm_c_g, w_out_c, w_router, b_router, w_gate_e, w_up_e, w_down_e, final_norm_g):
    D = D_MODEL
    Np, Nl = BATCH * SEQ, DEC_BATCH * DEC_SEQ
    N = Np + Nl
    H = C_HEADS
    x = None
    x_ctx, x_lat = x_prompt.reshape(Np, D), x_sample.reshape(Nl, D)
    moe_chunks = [(0, N // 2), (N // 2, N // 2)]
    n_e = w_gate_e.shape[1]
    wg_all = w_gate_e.reshape(DEPTH * n_e, D, D_FF_EXPERT)
    wu_all = w_up_e.reshape(DEPTH * n_e, D, D_FF_EXPERT)
    wd_all = w_down_e.reshape(DEPTH * n_e, D_FF_EXPERT, D)

    cvec = jnp.concatenate([c_ctx[None, :], c, jnp.zeros((8 - 1 - DEC_BATCH, D), F32)], axis=0)
    mod_all = ada_mod_all(cvec, w_ada, b_ada).reshape(DEPTH, 8, 6, D)
    w_router_t = w_router.T.astype(F32)

    outs = {}
    for l in range(DEPTH):
        mod = mod_all[l]
        j = l // 2
        if l % 2 == 0:
            w_in = w_in_ab[j]
            if x is not None:
                x_ctx, x_lat = x[:Np], x[Np:]
            proj_ctx = norm_mod_matmul(x_ctx, 0, Np, norm1_g[l], mod, w_in, F32, 0, 1, seg_row_off=0)
            proj_lat = norm_mod_matmul(x_lat, 0, Nl, norm1_g[l], mod, w_in, BF16, 0, 1, seg_row_off=Np)
            o_ctx = ctx_attention(proj_ctx, sink_b[j], BATCH, SEQ)
            bias_mask = nat_bias_mask(rel_bias_a[j], DEC_SEQ // GRID_W)
            oa = nat_attention(proj_lat, cache_a_k[:, j].reshape(DEC_BATCH, PAST_LEN, A_WIDTH),
                               cache_a_v[:, j].reshape(DEC_BATCH, PAST_LEN, A_WIDTH), bias_mask,
                               DEC_BATCH, DEC_SEQ, A_HEADS)
            cos, sin = rope_tables(DEC_SEQ)
            qk_rot = rope_heads(proj_lat, 3 * A_WIDTH, 3 * A_WIDTH + B_Q_WIDTH, cos, sin, DEC_SEQ)
            ob = swa_attention(qk_rot, proj_lat, 3 * A_HEADS + B_HEADS + B_KV_HEADS,
                               cache_b_k[:, j].reshape(DEC_BATCH, PAST_LEN, B_KV_WIDTH),
                               cache_b_v[:, j].reshape(DEC_BATCH, PAST_LEN, B_KV_WIDTH), sink_b[j],
                               DEC_BATCH, DEC_SEQ)
            x = attn_out_residual(o_ctx, oa, ob, w_out_ab[j].astype(BF16), x_ctx, x_lat, mod, gate_idx=2)
            kv = proj_ctx[:, A_WIDTH:]
            outs.setdefault("ak", []).append(kv[:, :A_WIDTH].reshape(BATCH, SEQ, A_HEADS, HEAD_DIM))
            outs.setdefault("av", []).append(kv[:, A_WIDTH:2 * A_WIDTH].reshape(BATCH, SEQ, A_HEADS, HEAD_DIM))
            kvb = proj_ctx[:, 3 * A_WIDTH + B_Q_WIDTH:]
            outs.setdefault("bk", []).append(kvb[:, :B_KV_WIDTH].reshape(BATCH, SEQ, B_KV_HEADS, HEAD_DIM))
            outs.setdefault("bv", []).append(kvb[:, B_KV_WIDTH:].reshape(BATCH, SEQ, B_KV_HEADS, HEAD_DIM))
        else:
            n_main = 2 * C_QK_WIDTH + 2 * C_V_WIDTH
            w_main = w_in_c[j]
            w_gates = jnp.pad(w_in_c[j][:, n_main:], ((0, 0), (0, LANES - 4 * H)))
            proj, gates = norm_mod_matmul(x, 0, N, norm1_g[l], mod, w_main, BF16, 0, 1, w_aux=w_gates,
                                          n_out=n_main)
            gates_dir = jnp.pad(gates[:, :4 * H].reshape(N, 2, 2, H).transpose(1, 2, 0, 3),
                                ((0, 0), (0, 0), (0, 0), (0, LANES - H)))
            bg = jnp.pad(b_gates_c[j].astype(F32).reshape(2, 2, 1, H), ((0, 0), (0, 0), (0, 0), (0, LANES - H)))
            S = BATCH + DEC_BATCH
            C0 = jnp.concatenate([jnp.zeros((BATCH, 2, H, C_QK_DIM, C_V_DIM), F32), state_C[:, j]], axis=0)
            n0 = jnp.concatenate([jnp.zeros((BATCH, 2, H, C_QK_DIM), F32), state_n[:, j]], axis=0)
            m0 = jnp.concatenate([jnp.zeros((BATCH, 2, H), F32), state_m[:, j]], axis=0)
            S0 = jnp.concatenate([C0, jnp.broadcast_to(n0[..., None], n0.shape + (LANES,))], axis=-1)
            m0 = jnp.broadcast_to(m0[..., None], (S, 2, H, LANES))
            seq_chunks = [SEQ // C_CHUNK] * BATCH + [DEC_SEQ // C_CHUNK] * DEC_BATCH
            h_dir, Sf, mf = mlstm_bidir(proj, gates_dir, bg, S0, m0, seq_chunks)
            x = mlstm_out_residual(h_dir, proj, 2, norm_c_g[j], w_out_c[j].astype(BF16), x, mod, gate_idx=2)
            outs.setdefault("C", []).append(Sf[:BATCH, ..., :C_V_DIM])
            outs.setdefault("n", []).append(Sf[:BATCH, ..., C_V_DIM])
            outs.setdefault("m", []).append(mf[:BATCH, :, :, 0])
        last = l == DEPTH - 1
        pieces = channel_mixer(x, norm2_g[l], mod, w_router_t, b_router, wg_all, wu_all, wd_all, l * n_e,
                               moe_chunks, [(0, Np), (Np, Nl)] if last else [(0, N)],
                               final_norm_g if last else None)
        if last:
            y_prompt = pieces[0].reshape(BATCH, SEQ, D)
            y_sample = pieces[1].reshape(DEC_BATCH, DEC_SEQ, D)
        else:
            x = pieces[0]

    return (y_prompt, y_sample,
            jnp.stack(outs["ak"], axis=1), jnp.stack(outs["av"], axis=1),
            jnp.stack(outs["bk"], axis=1), jnp.stack(outs["bv"], axis=1),
            jnp.stack(outs["C"], axis=1), jnp.stack(outs["n"], axis=1), jnp.stack(outs["m"], axis=1))
```

```python
import functools

import numpy as np
import jax
import jax.numpy as jnp
from jax import lax
from jax.experimental import pallas as pl
from jax.experimental.pallas import tpu as pltpu

F32 = jnp.float32
BF16 = jnp.bfloat16
I32 = jnp.int32

D_MODEL = 2048
BATCH = 16
SEQ = 256
DEPTH = 2
DEC_BATCH = 4
DEC_SEQ = 4096
PAST_LEN = 512
GRID_W = 64
HEAD_DIM = 128
A_HEADS = 8
NA_ROWS = 8
NA_COLS = 16
B_HEADS = 8
B_KV_HEADS = 2
B_WINDOW = 128
ROPE_THETA = 10000.0
C_HEADS = 8
C_QK_DIM = 128
C_V_DIM = 256
C_CHUNK = 128
N_EXPERTS = 16
N_GROUPS = 4
EXPERTS_PER_GROUP = N_EXPERTS // N_GROUPS
D_FF_EXPERT = 1024
EPS = 1e-6

A_WIDTH = A_HEADS * HEAD_DIM
B_Q_WIDTH = B_HEADS * HEAD_DIM
B_KV_WIDTH = B_KV_HEADS * HEAD_DIM
C_QK_WIDTH = C_HEADS * C_QK_DIM
C_V_WIDTH = C_HEADS * C_V_DIM

SEG_ROWS = 4096
NEG_BIG = -1e30
MOE_ROWS = 256
NORM_ROWS = 256
MIB = 1024 * 1024
LANES = 128

_NT = (((1,), (1,)), ((), ()))


def _cparams(sem, vmem_mib):
    return pltpu.CompilerParams(dimension_semantics=sem, vmem_limit_bytes=vmem_mib * MIB)


def _modulated_norm(x, g, mod, shift_idx, scale_idx):
    ms = jnp.mean(x * x, axis=-1, keepdims=True)
    y = x * lax.rsqrt(ms + EPS) * g
    return y * (1.0 + mod[scale_idx:scale_idx + 1, :]) + mod[shift_idx:shift_idx + 1, :]


def _hi_lo(x):
    hi = x.astype(BF16)
    return hi, (x - hi.astype(F32)).astype(BF16)


def _dot3(a, b, dims=None):
    a_hi, a_lo = _hi_lo(a)
    b_hi, b_lo = _hi_lo(b)
    dims = (((a.ndim - 1,), (0,)), ((), ())) if dims is None else dims
    dot = functools.partial(lax.dot_general, dimension_numbers=dims, preferred_element_type=F32)
    return dot(a_hi, b_hi) + dot(a_lo, b_hi) + dot(a_hi, b_lo)


def _ada_kernel(c_ref, w_ref, b_ref, o_ref):
    c = c_ref[...]
    o_ref[0] = _dot3(c * jax.nn.sigmoid(c), w_ref[0]) + b_ref[0]


def ada_mod_all(cvec8, w_ada, b_ada, tn=1024):
    L, D, D6 = w_ada.shape
    return pl.pallas_call(
        _ada_kernel,
        out_shape=jax.ShapeDtypeStruct((L, 8, D6), F32),
        grid=(L, D6 // tn),
        in_specs=[pl.BlockSpec((8, D), lambda l, j: (0, 0)),
                  pl.BlockSpec((1, D, tn), lambda l, j: (l, 0, j)),
                  pl.BlockSpec((1, 1, tn), lambda l, j: (l, 0, j))],
        out_specs=pl.BlockSpec((1, 8, tn), lambda l, j: (l, 0, j)),
        compiler_params=_cparams(("parallel", "parallel"), 40),
        name="ada_mod",
    )(cvec8, w_ada, b_ada.reshape(L, 1, D6))


def _nmm_kernel(x_ref, g_ref, mod_ref, w_ref, *rest, shift_idx, scale_idx, has_aux, precise):
    rest = list(rest)
    waux_ref = rest.pop(0) if has_aux else None
    o_ref = rest.pop(0)
    oaux_ref = rest.pop(0) if has_aux else None
    h_scr = rest.pop(0)
    hlo_scr = rest.pop(0) if precise else None

    @pl.when(pl.program_id(1) == 0)
    def _():
        for r in range(x_ref.shape[0] // NORM_ROWS):
            rows = pl.ds(r * NORM_ROWS, NORM_ROWS)
            hf = _modulated_norm(x_ref[rows, :], g_ref[...], mod_ref[0], shift_idx, scale_idx)
            h = hf.astype(BF16)
            h_scr[rows, :] = h
            if precise:
                hlo_scr[rows, :] = (hf - h.astype(F32)).astype(BF16)
            if has_aux:
                oaux_ref[rows, :] = jnp.dot(h, waux_ref[...].astype(BF16), preferred_element_type=F32)

    if precise:
        w_hi, w_lo = _hi_lo(w_ref[...])
        acc = (jnp.dot(h_scr[...], w_hi, preferred_element_type=F32)
               + jnp.dot(hlo_scr[...], w_hi, preferred_element_type=F32)
               + jnp.dot(h_scr[...], w_lo, preferred_element_type=F32))
    else:
        acc = jnp.dot(h_scr[...], w_ref[...].astype(BF16), preferred_element_type=F32)
    o_ref[...] = acc.astype(o_ref.dtype)


def norm_mod_matmul(x, row_off, n_rows, g, mod, w, out_dtype, shift_idx, scale_idx, w_aux=None, n_out=None,
                    seg_row_off=None, precise=False, tm=1024, tn=512):
    D = x.shape[1]
    n_out = w.shape[1] if n_out is None else n_out
    off_b = row_off // tm
    seg_b = off_b if seg_row_off is None else seg_row_off // tm
    per_seg = SEG_ROWS // tm
    has_aux = w_aux is not None
    in_specs = [pl.BlockSpec((tm, D), lambda i, j: (i + off_b, 0)),
                pl.BlockSpec((1, D), lambda i, j: (0, 0)),
                pl.BlockSpec((1, 6, D), lambda i, j: ((i + seg_b) // per_seg, 0, 0)),
                pl.BlockSpec((D, tn), lambda i, j: (0, j))]
    out_shape = jax.ShapeDtypeStruct((n_rows, n_out), out_dtype)
    out_specs = pl.BlockSpec((tm, tn), lambda i, j: (i, j))
    args = [x, g.reshape(1, D), mod, w]
    if has_aux:
        n_aux = w_aux.shape[1]
        in_specs.append(pl.BlockSpec((D, n_aux), lambda i, j: (0, 0)))
        out_shape = (out_shape, jax.ShapeDtypeStruct((n_rows, n_aux), F32))
        out_specs = (out_specs, pl.BlockSpec((tm, n_aux), lambda i, j: (i, 0)))
        args.append(w_aux)
    return pl.pallas_call(
        functools.partial(_nmm_kernel, shift_idx=shift_idx, scale_idx=scale_idx, has_aux=has_aux,
                          precise=precise),
        out_shape=out_shape,
        grid=(n_rows // tm, n_out // tn),
        in_specs=in_specs,
        out_specs=out_specs,
        scratch_shapes=[pltpu.VMEM((tm, D), BF16)] * (2 if precise else 1),
        compiler_params=_cparams(("parallel", "arbitrary"), 52),
        name="norm_mod_matmul",
    )(*args)


def _attn_out_kernel(oc_ref, oa_ref, ob_ref, w_ref, xc_ref, xl_ref, mod_ref, o_ref, *, gate_idx, n_ctx_tiles):
    i = pl.program_id(0)
    gate = mod_ref[0, gate_idx:gate_idx + 1, :]

    @pl.when(i < n_ctx_tiles)
    def _():
        oc_hi, oc_lo = _hi_lo(oc_ref[...])
        acc = (jnp.dot(oc_hi, w_ref[0], preferred_element_type=F32)
               + jnp.dot(oc_lo, w_ref[0], preferred_element_type=F32)
               + jnp.dot(oc_hi, w_ref[1], preferred_element_type=F32))
        o_ref[...] = xc_ref[...] + gate * acc

    @pl.when(i >= n_ctx_tiles)
    def _():
        a = jnp.concatenate([oa_ref[...], ob_ref[...]], axis=-1)
        acc = jnp.dot(a, w_ref[0], preferred_element_type=F32)
        o_ref[...] = xl_ref[...] + gate * acc


def attn_out_residual(o_ctx, oa, ob, w, x_ctx, x_lat, mod, gate_idx, tm=256):
    n_ctx, K = o_ctx.shape
    n_lat = oa.shape[0]
    D = w.shape[2]
    nct = n_ctx // tm
    per_seg = SEG_ROWS // tm

    def ctx_map(i):
        return (jnp.minimum(i, nct - 1), 0)

    def lat_map(i):
        return (jnp.maximum(i - nct, 0), 0)

    return pl.pallas_call(
        functools.partial(_attn_out_kernel, gate_idx=gate_idx, n_ctx_tiles=nct),
        out_shape=jax.ShapeDtypeStruct((n_ctx + n_lat, D), F32),
        grid=((n_ctx + n_lat) // tm,),
        in_specs=[pl.BlockSpec((tm, K), ctx_map),
                  pl.BlockSpec((tm, oa.shape[1]), lat_map),
                  pl.BlockSpec((tm, ob.shape[1]), lat_map),
                  pl.BlockSpec((2, K, D), lambda i: (0, 0, 0), pipeline_mode=pl.Buffered(1)),
                  pl.BlockSpec((tm, D), ctx_map),
                  pl.BlockSpec((tm, D), lat_map),
                  pl.BlockSpec((1, 6, D), lambda i: (i // per_seg, 0, 0))],
        out_specs=pl.BlockSpec((tm, D), lambda i: (i, 0)),
        compiler_params=_cparams(("arbitrary",), 48),
        name="attn_out_residual",
    )(o_ctx, oa, ob, w, x_ctx, x_lat, mod)


def _mlstm_out_kernel(hf_ref, hb_ref, o_ref, ng_ref, w_ref, x_ref, mod_ref, out_ref, *, gate_idx):
    hs = hf_ref[...] + hb_ref[...]
    parts = []
    for h in range(C_HEADS):
        sl = slice(h * C_V_DIM, (h + 1) * C_V_DIM)
        xs = hs[:, sl]
        ms = jnp.mean(xs * xs, axis=-1, keepdims=True)
        hn = xs * lax.rsqrt(ms + EPS) * ng_ref[:, sl]
        parts.append((jax.nn.sigmoid(o_ref[:, sl].astype(F32)) * hn).astype(BF16))
    a = jnp.concatenate(parts, axis=-1)
    acc = jnp.dot(a, w_ref[...], preferred_element_type=F32)
    out_ref[...] = x_ref[...] + mod_ref[0, gate_idx:gate_idx + 1, :] * acc


def mlstm_out_residual(h_dir, proj, o_col_block, norm_g, w, x, mod, gate_idx, tm=256):
    n, D = x.shape
    V = C_V_WIDTH
    per_seg = SEG_ROWS // tm
    return pl.pallas_call(
        functools.partial(_mlstm_out_kernel, gate_idx=gate_idx),
        out_shape=jax.ShapeDtypeStruct((n, D), F32),
        grid=(n // tm,),
        in_specs=[pl.BlockSpec((tm, V), lambda i: (i, 0)),
                  pl.BlockSpec((tm, V), lambda i: (i, 0)),
                  pl.BlockSpec((tm, V), lambda i: (i, o_col_block)),
                  pl.BlockSpec((1, V), lambda i: (0, 0)),
                  pl.BlockSpec((V, D), lambda i: (0, 0)),
                  pl.BlockSpec((tm, D), lambda i: (i, 0)),
                  pl.BlockSpec((1, 6, D), lambda i: (i // per_seg, 0, 0))],
        out_specs=pl.BlockSpec((tm, D), lambda i: (i, 0)),
        compiler_params=_cparams(("parallel",), 48),
        name="mlstm_out_residual",
    )(h_dir[0], h_dir[1], proj, norm_g.reshape(1, V), w, x, mod)


def _ctx_attn_kernel(sink_ref, q_ref, k_ref, v_ref, o_ref):
    h = pl.program_id(1)
    s = _dot3(q_ref[...], k_ref[...], _NT) * (HEAD_DIM ** -0.5)
    sk = sink_ref[h]
    m = jnp.maximum(jnp.max(s, axis=-1, keepdims=True), sk)
    p = jnp.exp(s - m)
    l = jnp.sum(p, axis=-1, keepdims=True) + jnp.exp(sk - m)
    o_ref[...] = (_dot3(p, v_ref[...]) / l).astype(o_ref.dtype)


def ctx_attention(proj, sink_b, n_batch, seq):
    n_heads = A_HEADS + B_HEADS
    group = B_HEADS // B_KV_HEADS
    qb0 = 3 * A_HEADS
    kb0 = qb0 + B_HEADS
    vb0 = kb0 + B_KV_HEADS
    sinks = jnp.concatenate([jnp.full((A_HEADS,), NEG_BIG, F32), sink_b.astype(F32)])

    def q_map(b, h, s):
        return (b, jnp.where(h < A_HEADS, h, qb0 + h - A_HEADS))

    def k_map(b, h, s):
        return (b, jnp.where(h < A_HEADS, A_HEADS + h, kb0 + (h - A_HEADS) // group))

    def v_map(b, h, s):
        return (b, jnp.where(h < A_HEADS, 2 * A_HEADS + h, vb0 + (h - A_HEADS) // group))

    blk = (seq, HEAD_DIM)
    return pl.pallas_call(
        _ctx_attn_kernel,
        out_shape=jax.ShapeDtypeStruct((n_batch * seq, n_heads * HEAD_DIM), F32),
        grid_spec=pltpu.PrefetchScalarGridSpec(
            num_scalar_prefetch=1,
            grid=(n_batch, n_heads),
            in_specs=[pl.BlockSpec(blk, q_map), pl.BlockSpec(blk, k_map), pl.BlockSpec(blk, v_map)],
            out_specs=pl.BlockSpec(blk, lambda b, h, s: (b, h))),
        compiler_params=_cparams(("parallel", "parallel"), 32),
        name="ctx_attention",
    )(sinks, proj, proj, proj)


NAT_QROWS = 4


def nat_bias_mask(rel_bias, rows):
    W = GRID_W
    nb = rows // NAT_QROWS
    kh = min(NA_ROWS, rows)
    n_dr, n_dc = 2 * NA_ROWS - 1, 2 * NA_COLS - 1
    H = rel_bias.shape[0]
    cidx = np.clip(np.arange(W)[None, :] - np.arange(W)[:, None] + NA_COLS - 1, 0, n_dc - 1)
    onehot = jnp.asarray((cidx.reshape(1, W * W) == np.arange(n_dc)[:, None]).astype(np.float32))
    col_bias = jnp.dot(rel_bias.astype(F32).reshape(H * n_dr, n_dc), onehot,
                       precision=lax.Precision.HIGHEST).reshape(H, n_dr, W, W)
    q_rows = []
    for qi in range(NAT_QROWS):
        tiles = [col_bias[:, int(np.clip(NAT_QROWS * (kj - 1) + kjr - qi + NA_ROWS - 1, 0, n_dr - 1))]
                 for kj in range(3) for kjr in range(NAT_QROWS)]
        q_rows.append(jnp.concatenate(tiles, axis=-1))
    bias = jnp.concatenate(q_rows, axis=1)
    variants = []
    for g in (0, 1, nb - 1):
        i = np.arange(NAT_QROWS)[:, None, None, None, None]
        qc = np.arange(W)[None, :, None, None, None]
        j = np.arange(3)[None, None, :, None, None]
        jr = np.arange(NAT_QROWS)[None, None, None, :, None]
        kc = np.arange(W)[None, None, None, None, :]
        r = NAT_QROWS * g + i
        kblk = g - 1 + j
        kr = NAT_QROWS * kblk + jr
        rs = np.clip(r - kh // 2, 0, rows - kh)
        row_ok = (kblk >= 0) & (kblk < nb) & (kr >= rs) & (kr < rs + kh)
        cstart = np.clip(qc - NA_COLS // 2, 0, W - NA_COLS)
        col_ok = (kc >= cstart) & (kc < cstart + NA_COLS)
        ok = np.broadcast_to(row_ok & col_ok, (NAT_QROWS, W, 3, NAT_QROWS, W))
        n_q, n_k = NAT_QROWS * W, 3 * NAT_QROWS * W
        variants.append(jnp.where(jnp.asarray(ok.reshape(1, n_q, n_k)), bias, NEG_BIG))
    return jnp.stack(variants, axis=0)


NAT_HEADS_PER_STEP = 4


def _nat_kernel(q_ref, k0_ref, k1_ref, k2_ref, v0_ref, v1_ref, v2_ref, kc_ref, vc_ref, bm_ref, o_ref):
    scale = HEAD_DIM ** -0.5
    tq = q_ref.shape[0]
    for h in range(q_ref.shape[1] // HEAD_DIM):
        cs = slice(h * HEAD_DIM, (h + 1) * HEAD_DIM)
        q = q_ref[:, cs]
        s_lat = [lax.dot_general(q, k_ref[:, cs], _NT, preferred_element_type=F32) * scale
                 + bm_ref[0, h, :, j * tq:(j + 1) * tq]
                 for j, k_ref in enumerate((k0_ref, k1_ref, k2_ref))]
        s_ctx = lax.dot_general(q, kc_ref[0, :, cs], _NT, preferred_element_type=F32) * scale
        m = jnp.max(s_ctx, axis=-1, keepdims=True)
        for s in s_lat:
            m = jnp.maximum(m, jnp.max(s, axis=-1, keepdims=True))
        p_ctx = jnp.exp(s_ctx - m)
        l = jnp.sum(p_ctx, axis=-1, keepdims=True)
        acc = jnp.dot(p_ctx.astype(BF16), vc_ref[0, :, cs], preferred_element_type=F32)
        for s, v_ref in zip(s_lat, (v0_ref, v1_ref, v2_ref)):
            p = jnp.exp(s - m)
            l = l + jnp.sum(p, axis=-1, keepdims=True)
            acc = acc + jnp.dot(p.astype(BF16), v_ref[:, cs], preferred_element_type=F32)
        o_ref[:, cs] = (acc / l).astype(o_ref.dtype)


def nat_attention(proj, cache_k, cache_v, bias_mask, n_batch, T, n_heads):
    tq = NAT_QROWS * GRID_W
    nb = T // tq
    P = cache_k.shape[1]
    hs = min(NAT_HEADS_PER_STEP, n_heads)
    ng = n_heads // hs

    def kv_map(col0, j):
        return lambda b, h, g: (b * nb + jnp.clip(g - 1 + j, 0, nb - 1), col0 + h)

    blk = (tq, hs * HEAD_DIM)
    in_specs = [pl.BlockSpec(blk, lambda b, h, g: (b * nb + g, h))]
    in_specs += [pl.BlockSpec(blk, kv_map(ng, j)) for j in range(3)]
    in_specs += [pl.BlockSpec(blk, kv_map(2 * ng, j)) for j in range(3)]
    in_specs += [pl.BlockSpec((1, P, hs * HEAD_DIM), lambda b, h, g: (b, 0, h))] * 2
    in_specs += [pl.BlockSpec((1, hs, tq, 3 * tq),
                              lambda b, h, g: (jnp.where(g == 0, 0, jnp.where(g == nb - 1, 2, 1)), h, 0, 0))]
    return pl.pallas_call(
        _nat_kernel,
        out_shape=jax.ShapeDtypeStruct((n_batch * T, n_heads * HEAD_DIM), BF16),
        grid=(n_batch, ng, nb),
        in_specs=in_specs,
        out_specs=pl.BlockSpec(blk, lambda b, h, g: (b * nb + g, h)),
        compiler_params=_cparams(("parallel", "parallel", "arbitrary"), 40),
        name="nat_attention",
    )(proj, proj, proj, proj, proj, proj, proj, cache_k, cache_v, bias_mask)


def rope_tables(T):
    t = jnp.arange(T)
    row = (t // GRID_W).astype(F32)
    col = (t % GRID_W).astype(F32)
    nf = HEAD_DIM // 4
    freqs = ROPE_THETA ** (-jnp.arange(nf, dtype=F32) / nf)
    ar = row[:, None] * freqs
    ac = col[:, None] * freqs
    cos = jnp.concatenate([jnp.cos(ar), jnp.cos(ar), jnp.cos(ac), jnp.cos(ac)], axis=-1)
    sin = jnp.concatenate([-jnp.sin(ar), jnp.sin(ar), -jnp.sin(ac), jnp.sin(ac)], axis=-1)
    return cos, sin


def _rope_kernel(q_ref, k_ref, cos_ref, sin_ref, o_ref):
    nf = HEAD_DIM // 4
    cos = cos_ref[...]
    sin = sin_ref[...]
    lane = lax.broadcasted_iota(I32, cos.shape, 1)
    first_half = (lane & nf) == 0
    col = 0
    for src in (q_ref, k_ref):
        for h in range(src.shape[1] // HEAD_DIM):
            x = src[:, h * HEAD_DIM:(h + 1) * HEAD_DIM].astype(F32)
            upper = pltpu.roll(x, HEAD_DIM - nf, 1)
            lower = pltpu.roll(x, nf, 1)
            partner = jnp.where(first_half, upper, lower)
            o_ref[:, col:col + HEAD_DIM] = (x * cos + partner * sin).astype(o_ref.dtype)
            col += HEAD_DIM


def rope_heads(proj, q_col0, k_col0, cos, sin, T, tm=512):
    n = proj.shape[0]
    per_seq = T // tm
    return pl.pallas_call(
        _rope_kernel,
        out_shape=jax.ShapeDtypeStruct((n, B_Q_WIDTH + B_KV_WIDTH), BF16),
        grid=(n // tm,),
        in_specs=[pl.BlockSpec((tm, B_Q_WIDTH), lambda i: (i, q_col0 // B_Q_WIDTH)),
                  pl.BlockSpec((tm, B_KV_WIDTH), lambda i: (i, k_col0 // B_KV_WIDTH)),
                  pl.BlockSpec((tm, HEAD_DIM), lambda i: (i % per_seq, 0)),
                  pl.BlockSpec((tm, HEAD_DIM), lambda i: (i % per_seq, 0))],
        out_specs=pl.BlockSpec((tm, B_Q_WIDTH + B_KV_WIDTH), lambda i: (i, 0)),
        compiler_params=_cparams(("parallel",), 32),
        name="rope_heads",
    )(proj, proj, cos, sin)


SWA_TQ = 2 * B_WINDOW


def _swa_kernel(sink_ref, q_ref, k0_ref, k1_ref, k2_ref, k3_ref, v0_ref, v1_ref, v2_ref, v3_ref,
                kc_ref, vc_ref, o_ref, *, T):
    scale = HEAD_DIM ** -0.5
    group = B_HEADS // B_KV_HEADS
    kvh = pl.program_id(1)
    n = pl.program_id(2)
    k = jnp.concatenate([k0_ref[...], k1_ref[...], k2_ref[...], k3_ref[...]], axis=0)
    v = jnp.concatenate([v0_ref[...], v1_ref[...], v2_ref[...], v3_ref[...]], axis=0)
    kc = kc_ref[0].astype(BF16)
    vc = vc_ref[0].astype(BF16)
    nk = k.shape[0]
    qpos = n * SWA_TQ + lax.broadcasted_iota(I32, (SWA_TQ, nk), 0)
    kpos = n * SWA_TQ - B_WINDOW + lax.broadcasted_iota(I32, (SWA_TQ, nk), 1)
    dist = jnp.abs(qpos - kpos)
    ok = jnp.where(kpos >= 0, jnp.where(kpos < T, dist, B_WINDOW + 1), B_WINDOW + 1) <= B_WINDOW
    outs = []
    for gi in range(group):
        q = q_ref[:, gi * HEAD_DIM:(gi + 1) * HEAD_DIM]
        s_lat = jnp.where(ok, lax.dot_general(q, k, _NT, preferred_element_type=F32) * scale, NEG_BIG)
        s_ctx = lax.dot_general(q, kc, _NT, preferred_element_type=F32) * scale
        sk = sink_ref[kvh * group + gi]
        m = jnp.maximum(jnp.maximum(jnp.max(s_lat, axis=-1, keepdims=True),
                                    jnp.max(s_ctx, axis=-1, keepdims=True)), sk)
        p_lat = jnp.exp(s_lat - m)
        p_ctx = jnp.exp(s_ctx - m)
        l = (jnp.sum(p_lat, axis=-1, keepdims=True) + jnp.sum(p_ctx, axis=-1, keepdims=True)
             + jnp.exp(sk - m))
        acc = (jnp.dot(p_lat.astype(BF16), v, preferred_element_type=F32)
               + jnp.dot(p_ctx.astype(BF16), vc, preferred_element_type=F32))
        outs.append((acc / l).astype(o_ref.dtype))
    o_ref[...] = jnp.concatenate(outs, axis=-1)


def swa_attention(qk_rot, proj, v_col0, cache_k, cache_v, sink, n_batch, T):
    group = B_HEADS // B_KV_HEADS
    nq = T // SWA_TQ
    nkb = T // B_WINDOW
    P = cache_k.shape[1]

    def kv_map(col0, j):
        return lambda b, kvh, n, s: (b * nkb + jnp.clip(2 * n - 1 + j, 0, nkb - 1), col0 + kvh)

    kblk = (B_WINDOW, HEAD_DIM)
    in_specs = [pl.BlockSpec((SWA_TQ, group * HEAD_DIM), lambda b, kvh, n, s: (b * nq + n, kvh))]
    in_specs += [pl.BlockSpec(kblk, kv_map(B_HEADS, j)) for j in range(4)]
    in_specs += [pl.BlockSpec(kblk, kv_map(v_col0, j)) for j in range(4)]
    in_specs += [pl.BlockSpec((1, P, HEAD_DIM), lambda b, kvh, n, s: (b, 0, kvh))] * 2
    return pl.pallas_call(
        functools.partial(_swa_kernel, T=T),
        out_shape=jax.ShapeDtypeStruct((n_batch * T, B_Q_WIDTH), BF16),
        grid_spec=pltpu.PrefetchScalarGridSpec(
            num_scalar_prefetch=1,
            grid=(n_batch, B_KV_HEADS, nq),
            in_specs=in_specs,
            out_specs=pl.BlockSpec((SWA_TQ, group * HEAD_DIM), lambda b, kvh, n, s: (b * nq + n, kvh))),
        compiler_params=_cparams(("parallel", "parallel", "arbitrary"), 32),
        name="swa_attention",
    )(sink.astype(F32), qk_rot, qk_rot, qk_rot, qk_rot, qk_rot, proj, proj, proj, proj, cache_k, cache_v)


def _mlstm_kernel(rowblk_ref, seq_ref, first_ref, last_ref, *refs):
    s = pl.program_id(1)
    for d in range(2):
        @pl.when(pl.program_id(0) == d)
        def _(d=d):
            _mlstm_chunk(d, first_ref[s] == 1, last_ref[s] == 1, *refs)


def _mlstm_chunk(d, is_first, is_last, q_ref, k_ref, v_ref, g_ref, bg_ref, S0_ref, m0_ref,
                 h_ref, Sf_ref, mf_ref, S_scr, m_scr):
    H, DK, DV, L, R = C_HEADS, C_QK_DIM, C_V_DIM, C_CHUNK, LANES
    scale = DK ** -0.5

    @pl.when(is_first)
    def _():
        S_scr[...] = S0_ref[0, 0]
        m_scr[...] = m0_ref[0, 0]

    gi = g_ref[0, 0] + bg_ref[0, 0]
    lf = jax.nn.log_sigmoid(g_ref[0, 1] + bg_ref[0, 1])
    row = lax.broadcasted_iota(I32, (L, L), 0)
    col = lax.broadcasted_iota(I32, (L, L), 1)
    causal = col <= row if d == 0 else col >= row
    b_all = jnp.dot(causal.astype(F32), lf, precision=lax.Precision.HIGHEST,
                    preferred_element_type=F32)
    a_row = (gi - b_all).T[0:H, :]
    b_row = b_all.T[0:H, :]
    lane = lax.broadcasted_iota(I32, (H, L), 1)
    cm = a_row
    k = 1
    while k < L:
        if d == 0:
            shifted = jnp.where(lane >= k, pltpu.roll(cm, k, 1), -jnp.inf)
        else:
            shifted = jnp.where(lane < L - k, pltpu.roll(cm, L - k, 1), -jnp.inf)
        cm = jnp.maximum(cm, shifted)
        k *= 2
    end = L - 1 if d == 0 else 0
    m_all = m_scr[...]
    M_row = jnp.maximum(m_all, cm)
    wi_row = jnp.exp(m_all - M_row)
    emt_row = jnp.exp(-(b_row + M_row))
    M_last = M_row[:, end:end + 1]
    b_last = b_row[:, end:end + 1]
    ws_row = jnp.exp(a_row - M_last)
    wc_all = jnp.exp(m_all - M_last)
    m_scr[...] = jnp.broadcast_to(b_last + M_last, m_all.shape)
    cols = jnp.concatenate([M_row, wi_row, emt_row, jnp.zeros((L - 3 * H, L), F32)], axis=0).T
    ones = jnp.ones((L, R), BF16)
    for h in range(H):
        M_col = cols[:, h:h + 1]
        wi_col = cols[:, H + h:H + h + 1]
        emt_col = cols[:, 2 * H + h:2 * H + h + 1]
        qh = q_ref[:, h * DK:(h + 1) * DK]
        kh = k_ref[:, h * DK:(h + 1) * DK]
        vh = v_ref[:, h * DV:(h + 1) * DV]
        v_ext = jnp.concatenate([vh, ones], axis=-1)
        Sh = S_scr[h]

        w = jnp.where(causal, jnp.exp(a_row[h:h + 1, :] - M_col), 0.0)
        sqk = lax.dot_general(qh, kh, _NT, preferred_element_type=F32) * scale * w
        S_hi = Sh.astype(BF16)
        n_lo = (Sh[:, DV:] - S_hi[:, DV:].astype(F32)).astype(BF16)
        inter = jnp.dot(qh, jnp.concatenate([S_hi, n_lo], axis=-1),
                        preferred_element_type=F32) * scale
        qn = inter[:, DV:DV + 1] + inter[:, DV + R:DV + R + 1]
        num = jnp.dot(sqk.astype(BF16), vh, preferred_element_type=F32) + wi_col * inter[:, :DV]
        den = jnp.sum(sqk, axis=-1, keepdims=True) + wi_col * qn
        h_ref[:, h * DV:(h + 1) * DV] = num / jnp.maximum(jnp.abs(den), emt_col)

        kwT = kh.astype(F32).T * ws_row[h:h + 1, :]
        kwT_hi = kwT.astype(BF16)
        kwT_lo = (kwT - kwT_hi.astype(F32)).astype(BF16)
        upd = jnp.dot(kwT_hi, v_ext, preferred_element_type=F32)
        upd_n = upd[:, DV:] + jnp.dot(kwT_lo, ones, preferred_element_type=F32)
        wc = jnp.concatenate([wc_all[h:h + 1, :]] * (DV // R + 1), axis=-1)
        S_scr[h] = wc * Sh + jnp.concatenate([upd[:, :DV], upd_n], axis=-1)

    @pl.when(is_last)
    def _():
        Sf_ref[0, 0] = S_scr[...]
        mf_ref[0, 0] = m_scr[...]


def mlstm_bidir(proj, gates_dir, b_gates_dir, S0, m0, seq_chunks):
    H, DK, DV, L = C_HEADS, C_QK_DIM, C_V_DIM, C_CHUNK
    DS = DV + LANES
    N = proj.shape[0]
    S = len(seq_chunks)
    rowblk, seq_id, first, last = [[], []], [], [], []
    base = 0
    for sq, nc in enumerate(seq_chunks):
        rowblk[0] += [base + c for c in range(nc)]
        rowblk[1] += [base + nc - 1 - c for c in range(nc)]
        seq_id += [sq] * nc
        first += [1] + [0] * (nc - 1)
        last += [0] * (nc - 1) + [1]
        base += nc
    n_steps = base
    rowblk = jnp.asarray(np.array(rowblk, np.int32).reshape(-1))
    tables = (rowblk, jnp.asarray(seq_id, I32), jnp.asarray(first, I32), jnp.asarray(last, I32))

    def row_map(colblk):
        return lambda d, s, rb, sq, fi, la: (rb[d * n_steps + s], colblk)

    def state_map(nd):
        return lambda d, s, rb, sq, fi, la: (sq[s], d) + (0,) * nd

    in_specs = [pl.BlockSpec((L, H * DK), row_map(0)),
                pl.BlockSpec((L, H * DK), row_map(1)),
                pl.BlockSpec((L, H * DV), row_map(1)),
                pl.BlockSpec((1, 2, L, LANES), lambda d, s, rb, sq, fi, la: (d, 0, rb[d * n_steps + s], 0)),
                pl.BlockSpec((1, 2, 1, LANES), lambda d, s, rb, sq, fi, la: (d, 0, 0, 0)),
                pl.BlockSpec((1, 1, H, DK, DS), state_map(3)),
                pl.BlockSpec((1, 1, H, LANES), state_map(2))]
    out_specs = (pl.BlockSpec((None, L, H * DV), lambda d, s, rb, sq, fi, la: (d, rb[d * n_steps + s], 0)),
                 pl.BlockSpec((1, 1, H, DK, DS), state_map(3)),
                 pl.BlockSpec((1, 1, H, LANES), state_map(2)))
    out_shape = (jax.ShapeDtypeStruct((2, N, H * DV), F32),
                 jax.ShapeDtypeStruct((S, 2, H, DK, DS), F32),
                 jax.ShapeDtypeStruct((S, 2, H, LANES), F32))
    return pl.pallas_call(
        _mlstm_kernel,
        out_shape=out_shape,
        grid_spec=pltpu.PrefetchScalarGridSpec(
            num_scalar_prefetch=4,
            grid=(2, n_steps),
            in_specs=in_specs,
            out_specs=out_specs,
            scratch_shapes=[pltpu.VMEM((H, DK, DS), F32), pltpu.VMEM((H, LANES), F32)]),
        compiler_params=_cparams(("arbitrary", "arbitrary"), 40),
        name="mlstm_bidir",
    )(*tables, proj, proj, proj, gates_dir, b_gates_dir, S0, m0)


def _top2_of4(vals):
    m1, i1 = vals[0], jnp.zeros(vals[0].shape, I32)
    for j in range(1, 4):
        better = vals[j] > m1
        m1 = jnp.where(better, vals[j], m1)
        i1 = jnp.where(better, j, i1)
    m2, i2 = jnp.full(vals[0].shape, -jnp.inf, F32), jnp.zeros(vals[0].shape, I32)
    for j in range(4):
        cand = jnp.where(i1 == j, -jnp.inf, vals[j])
        better = cand > m2
        m2 = jnp.where(better, cand, m2)
        i2 = jnp.where(better, j, i2)
    return m1, i1, m2, i2


def _router_kernel(x_ref, g_ref, mod_ref, wr_ref, br_ref, h_ref, ids_ref, wts_ref, *, shift_idx, scale_idx):
    h = _modulated_norm(x_ref[...], g_ref[...], mod_ref[0], shift_idx, scale_idx)
    h_hi = h.astype(BF16)
    h_ref[...] = h_hi
    h_lo = (h - h_hi.astype(F32)).astype(BF16)
    w_hi = wr_ref[0]
    w_lo = wr_ref[1]
    logits = (lax.dot_general(w_hi, h_hi, _NT, preferred_element_type=F32)
              + lax.dot_general(w_lo, h_hi, _NT, preferred_element_type=F32)
              + lax.dot_general(w_hi, h_lo, _NT, preferred_element_type=F32))
    aff = jax.nn.sigmoid(logits)
    sel = aff + br_ref[...]
    aff_rows = [aff[e:e + 1, :] for e in range(N_EXPERTS)]
    sel_rows = [sel[e:e + 1, :] for e in range(N_EXPERTS)]
    tops = [_top2_of4(sel_rows[4 * gidx:4 * gidx + 4]) for gidx in range(N_GROUPS)]
    best = tops[0][0] + tops[0][2]
    grp = jnp.zeros(best.shape, I32)
    i1, i2 = tops[0][1], tops[0][3]
    for gidx in range(1, N_GROUPS):
        score = tops[gidx][0] + tops[gidx][2]
        better = score > best
        best = jnp.where(better, score, best)
        grp = jnp.where(better, gidx, grp)
        i1 = jnp.where(better, tops[gidx][1], i1)
        i2 = jnp.where(better, tops[gidx][3], i2)
    e1 = grp * EXPERTS_PER_GROUP + i1
    e2 = grp * EXPERTS_PER_GROUP + i2
    w1 = jnp.zeros(best.shape, F32)
    w2 = jnp.zeros(best.shape, F32)
    for e in range(N_EXPERTS):
        w1 = jnp.where(e1 == e, aff_rows[e], w1)
        w2 = jnp.where(e2 == e, aff_rows[e], w2)
    tot = w1 + w2
    ids_ref[...] = jnp.zeros(ids_ref.shape, I32)
    wts_ref[...] = jnp.zeros(wts_ref.shape, F32)
    ids_ref[0:1, :] = e1
    ids_ref[1:2, :] = e2
    wts_ref[0:1, :] = w1 / tot
    wts_ref[1:2, :] = w2 / tot


def router_weight_pair(w_router):
    w_t = w_router.T.astype(F32)
    w_hi = w_t.astype(BF16)
    return jnp.stack([w_hi, (w_t - w_hi.astype(F32)).astype(BF16)], axis=0)


def moe_router(x, row_off, n_rows, g, mod, w_router_t, b_router, shift_idx, scale_idx, tm=512):
    D = x.shape[1]
    off_b = row_off // tm
    per_seg = SEG_ROWS // tm
    return pl.pallas_call(
        functools.partial(_router_kernel, shift_idx=shift_idx, scale_idx=scale_idx),
        out_shape=(jax.ShapeDtypeStruct((n_rows, D), BF16),
                   jax.ShapeDtypeStruct((8, n_rows), I32),
                   jax.ShapeDtypeStruct((8, n_rows), F32)),
        grid=(n_rows // tm,),
        in_specs=[pl.BlockSpec((tm, D), lambda i: (i + off_b, 0)),
                  pl.BlockSpec((1, D), lambda i: (0, 0)),
                  pl.BlockSpec((1, 6, D), lambda i: ((i + off_b) // per_seg, 0, 0)),
                  pl.BlockSpec((2, N_EXPERTS, D), lambda i: (0, 0, 0)),
                  pl.BlockSpec((N_EXPERTS, 1), lambda i: (0, 0))],
        out_specs=(pl.BlockSpec((tm, D), lambda i: (i, 0)),
                   pl.BlockSpec((8, tm), lambda i: (0, i)),
                   pl.BlockSpec((8, tm), lambda i: (0, i))),
        compiler_params=_cparams(("parallel",), 40),
        name="moe_router",
    )(x, g.reshape(1, D), mod, w_router_t, b_router.reshape(N_EXPERTS, 1).astype(F32))


MOE_CAST_ROWS = 128


def _moe_expert_kernel(be_ref, first_ref, next_ref, nu_ref, x_ref, wg_hbm, wu_hbm, wd_hbm, y_ref,
                       wg_st, wu_st, wd_st, wg_bf, wu_bf, wd_bf, sems, *, e_off):
    i = pl.program_id(0)

    def weight_copies(e):
        return (pltpu.make_async_copy(wg_hbm.at[e_off + e], wg_st, sems.at[0]),
                pltpu.make_async_copy(wu_hbm.at[e_off + e], wu_st, sems.at[1]),
                pltpu.make_async_copy(wd_hbm.at[e_off + e], wd_st, sems.at[2]))

    @pl.when(i == 0)
    def _():
        for cp in weight_copies(be_ref[0]):
            cp.start()

    @pl.when(first_ref[i] == 1)
    def _():
        for cp in weight_copies(be_ref[i]):
            cp.wait()
        for st, bf in ((wg_st, wg_bf), (wu_st, wu_bf), (wd_st, wd_bf)):
            def cast_rows(c, carry, st=st, bf=bf):
                r = pl.multiple_of(c * MOE_CAST_ROWS, MOE_CAST_ROWS)
                bf[pl.ds(r, MOE_CAST_ROWS), :] = st[pl.ds(r, MOE_CAST_ROWS), :].astype(BF16)
                return carry
            lax.fori_loop(0, st.shape[0] // MOE_CAST_ROWS, cast_rows, 0)

        @pl.when(next_ref[i] >= 0)
        def _():
            for cp in weight_copies(next_ref[i]):
                cp.start()

    @pl.when(i < nu_ref[0])
    def _():
        x = x_ref[...]
        gate = jnp.dot(x, wg_bf[...], preferred_element_type=F32)
        up = jnp.dot(x, wu_bf[...], preferred_element_type=F32)
        act = (gate * jax.nn.sigmoid(gate) * up).astype(BF16)
        y_ref[...] = jnp.dot(act, wd_bf[...], preferred_element_type=F32).astype(y_ref.dtype)

    @pl.when(i >= nu_ref[0])
    def _():
        y_ref[...] = jnp.zeros(y_ref.shape, y_ref.dtype)


def moe_experts(xg, plan, wg, wu, wd, e_off):
    R, D = xg.shape
    F = wg.shape[2]
    nb = R // MOE_ROWS
    hbm = pl.BlockSpec(memory_space=pl.ANY)
    return pl.pallas_call(
        functools.partial(_moe_expert_kernel, e_off=e_off),
        out_shape=jax.ShapeDtypeStruct((R, D), BF16),
        grid_spec=pltpu.PrefetchScalarGridSpec(
            num_scalar_prefetch=4,
            grid=(nb,),
            in_specs=[pl.BlockSpec((MOE_ROWS, D), lambda i, *_: (i, 0)), hbm, hbm, hbm],
            out_specs=pl.BlockSpec((MOE_ROWS, D), lambda i, *_: (i, 0)),
            scratch_shapes=[pltpu.VMEM((D, F), F32), pltpu.VMEM((D, F), F32), pltpu.VMEM((F, D), F32),
                            pltpu.VMEM((D, F), BF16), pltpu.VMEM((D, F), BF16), pltpu.VMEM((F, D), BF16),
                            pltpu.SemaphoreType.DMA((3,))]),
        compiler_params=_cparams(("arbitrary",), 56),
        name="moe_experts",
    )(plan["block_e"], plan["first"], plan["next_e"], plan["n_used"], xg, wg, wu, wd)


def _combine_kernel(x_ref, mod_ref, *rest, gate_idx, final, tile_off, chunk_tiles):
    rest = list(rest)
    fg_ref = rest.pop(0) if final else None
    o_ref = rest.pop()
    t = pl.program_id(0) + tile_off
    for c, (lo, n) in enumerate(chunk_tiles):
        y0_ref, y1_ref, w_ref = rest[3 * c:3 * c + 3]

        @pl.when((t >= lo) & (t < lo + n))
        def _(y0_ref=y0_ref, y1_ref=y1_ref, w_ref=w_ref):
            y = w_ref[:, 0:1] * y0_ref[...].astype(F32) + w_ref[:, 1:2] * y1_ref[...].astype(F32)
            x = x_ref[...] + mod_ref[0, gate_idx:gate_idx + 1, :] * y
            if final:
                ms = jnp.mean(x * x, axis=-1, keepdims=True)
                x = x * lax.rsqrt(ms + EPS) * fg_ref[...]
            o_ref[...] = x


def moe_combine(x, row_off, n_rows, parts, mod, gate_idx, final_g=None, tm=512):
    D = x.shape[1]
    xb = row_off // tm
    per_seg = SEG_ROWS // tm
    final = final_g is not None
    in_specs = [pl.BlockSpec((tm, D), lambda i: (i + xb, 0)),
                pl.BlockSpec((1, 6, D), lambda i: ((i + xb) // per_seg, 0, 0))]
    args = [x, mod]
    if final:
        in_specs.append(pl.BlockSpec((1, D), lambda i: (0, 0)))
        args.append(final_g.reshape(1, D))
    chunk_tiles = []
    for c_off, c_n, y0, y1, wts in parts:
        lo, n = c_off // tm, c_n // tm
        chunk_tiles.append((lo, n))

        def chunk_map(i, lo=lo, n=n):
            return (jnp.clip(i + xb - lo, 0, n - 1), 0)

        in_specs += [pl.BlockSpec((tm, D), chunk_map), pl.BlockSpec((tm, D), chunk_map),
                     pl.BlockSpec((tm, 2), chunk_map)]
        args += [y0, y1, wts]
    return pl.pallas_call(
        functools.partial(_combine_kernel, gate_idx=gate_idx, final=final, tile_off=xb,
                          chunk_tiles=tuple(chunk_tiles)),
        out_shape=jax.ShapeDtypeStruct((n_rows, D), F32),
        grid=(n_rows // tm,),
        in_specs=in_specs,
        out_specs=pl.BlockSpec((tm, D), lambda i: (i, 0)),
        compiler_params=_cparams(("arbitrary",), 48),
        name="moe_combine",
    )(*args)


def moe_dispatch_plan(ids, n_tokens):
    n_assign = 2 * n_tokens
    n_blocks = n_assign // MOE_ROWS + N_EXPERTS
    experts = jnp.arange(N_EXPERTS, dtype=I32)
    flat_e = ids.T.reshape(n_assign)
    onehot = (flat_e[:, None] == experts[None, :]).astype(I32)
    csum = jnp.cumsum(onehot, axis=0)
    rank = jnp.sum((csum - onehot) * onehot, axis=1)
    counts = csum[-1]
    padded = (counts + MOE_ROWS - 1) // MOE_ROWS * MOE_ROWS
    pend = jnp.cumsum(padded)
    dest = (pend - padded)[flat_e] + rank
    n_used = (pend[-1] // MOE_ROWS).astype(I32)
    block_idx = jnp.arange(n_blocks, dtype=I32)
    block_e = jnp.minimum(jnp.sum((pend[None, :] <= block_idx[:, None] * MOE_ROWS).astype(I32), axis=1),
                          N_EXPERTS - 1)
    prev_e = jnp.concatenate([jnp.full((1,), -1, I32), block_e[:-1]])
    first = ((block_idx < n_used) & (block_e != prev_e)).astype(I32)
    later = (experts[None, :] > experts[:, None]) & (counts[None, :] > 0)
    next_of = jnp.min(jnp.where(later, experts[None, :], N_EXPERTS), axis=1)
    next_of = jnp.where(next_of == N_EXPERTS, -1, next_of)
    next_e = jnp.sum(jnp.where(block_e[:, None] == experts[None, :], next_of[None, :], 0), axis=1).astype(I32)
    src_tok = jnp.zeros((n_blocks * MOE_ROWS,), I32).at[dest].set(jnp.arange(n_assign, dtype=I32) // 2)
    plan = {"block_e": block_e, "first": first, "next_e": next_e, "n_used": n_used.reshape(1)}
    return dest.reshape(n_tokens, 2), src_tok, plan


def channel_mixer(x, g, mod, w_router_t, b_router, wg, wu, wd, e_off, chunks, outputs, final_g=None):
    parts = []
    for c_off, c_n in chunks:
        h, ids8, wts8 = moe_router(x, c_off, c_n, g, mod, w_router_t, b_router, shift_idx=3, scale_idx=4)
        dest, src_tok, plan = moe_dispatch_plan(ids8[:2], c_n)
        xg = jnp.take(h, src_tok, axis=0)
        ybuf = moe_experts(xg, plan, wg, wu, wd, e_off)
        y0 = ybuf.at[dest[:, 0]].get(mode="promise_in_bounds")
        y1 = ybuf.at[dest[:, 1]].get(mode="promise_in_bounds")
        parts.append((c_off, c_n, y0, y1, wts8[:2].T))
    return [moe_combine(x, o_off, o_n, parts, mod, 5, final_g) for o_off, o_n in outputs]


def kernel(x_prompt, x_sample, c, cache_a_k, cache_a_v, cache_b_k, cache_b_v, state_C, state_n, state_m,
           c_ctx, norm1_g, norm2_g, w_ada, b_ada, w_in_ab, w_out_ab, rel_bias_a, sink_b, w_in_c, b_gates_c,
           norm_c_g, w_out_c, w_router, b_router, w_gate_e, w_up_e, w_down_e, final_norm_g):
    D = D_MODEL
    Np, Nl = BATCH * SEQ, DEC_BATCH * DEC_SEQ
    N = Np + Nl
    H = C_HEADS
    x = None
    x_ctx, x_lat = x_prompt.reshape(Np, D), x_sample.reshape(Nl, D)
    moe_chunks = [(0, N // 2), (N // 2, N // 2)]
    n_e = w_gate_e.shape[1]
    wg_all = w_gate_e.reshape(DEPTH * n_e, D, D_FF_EXPERT)
    wu_all = w_up_e.reshape(DEPTH * n_e, D, D_FF_EXPERT)
    wd_all = w_down_e.reshape(DEPTH * n_e, D_FF_EXPERT, D)

    cvec = jnp.concatenate([c_ctx[None, :], c, jnp.zeros((8 - 1 - DEC_BATCH, D), F32)], axis=0)
    mod_all = ada_mod_all(cvec, w_ada, b_ada).reshape(DEPTH, 8, 6, D)
    w_router_t = router_weight_pair(w_router)

    outs = {}
    for l in range(DEPTH):
        mod = mod_all[l]
        j = l // 2
        if l % 2 == 0:
            w_in = w_in_ab[j]
            if x is not None:
                x_ctx, x_lat = x[:Np], x[Np:]
            proj_ctx = norm_mod_matmul(x_ctx, 0, Np, norm1_g[l], mod, w_in, F32, 0, 1, seg_row_off=0,
                                       precise=True, tm=512)
            proj_lat = norm_mod_matmul(x_lat, 0, Nl, norm1_g[l], mod, w_in, BF16, 0, 1, seg_row_off=Np)
            o_ctx = ctx_attention(proj_ctx, sink_b[j], BATCH, SEQ)
            bias_mask = nat_bias_mask(rel_bias_a[j], DEC_SEQ // GRID_W)
            oa = nat_attention(proj_lat, cache_a_k[:, j].reshape(DEC_BATCH, PAST_LEN, A_WIDTH).astype(BF16),
                               cache_a_v[:, j].reshape(DEC_BATCH, PAST_LEN, A_WIDTH).astype(BF16), bias_mask,
                               DEC_BATCH, DEC_SEQ, A_HEADS)
            cos, sin = rope_tables(DEC_SEQ)
            qk_rot = rope_heads(proj_lat, 3 * A_WIDTH, 3 * A_WIDTH + B_Q_WIDTH, cos, sin, DEC_SEQ)
            ob = swa_attention(qk_rot, proj_lat, 3 * A_HEADS + B_HEADS + B_KV_HEADS,
                               cache_b_k[:, j].reshape(DEC_BATCH, PAST_LEN, B_KV_WIDTH),
                               cache_b_v[:, j].reshape(DEC_BATCH, PAST_LEN, B_KV_WIDTH), sink_b[j],
                               DEC_BATCH, DEC_SEQ)
            x = attn_out_residual(o_ctx, oa, ob, jnp.stack(_hi_lo(w_out_ab[j]), axis=0), x_ctx, x_lat, mod,
                                  gate_idx=2)
            kv = proj_ctx[:, A_WIDTH:]
            outs.setdefault("ak", []).append(kv[:, :A_WIDTH].reshape(BATCH, SEQ, A_HEADS, HEAD_DIM))
            outs.setdefault("av", []).append(kv[:, A_WIDTH:2 * A_WIDTH].reshape(BATCH, SEQ, A_HEADS, HEAD_DIM))
            kvb = proj_ctx[:, 3 * A_WIDTH + B_Q_WIDTH:]
            outs.setdefault("bk", []).append(kvb[:, :B_KV_WIDTH].reshape(BATCH, SEQ, B_KV_HEADS, HEAD_DIM))
            outs.setdefault("bv", []).append(kvb[:, B_KV_WIDTH:].reshape(BATCH, SEQ, B_KV_HEADS, HEAD_DIM))
        else:
            n_main = 2 * C_QK_WIDTH + 2 * C_V_WIDTH
            w_main = w_in_c[j]
            w_gates = jnp.pad(w_in_c[j][:, n_main:], ((0, 0), (0, LANES - 4 * H)))
            proj, gates = norm_mod_matmul(x, 0, N, norm1_g[l], mod, w_main, BF16, 0, 1, w_aux=w_gates,
                                          n_out=n_main)
            gates_dir = jnp.pad(gates[:, :4 * H].reshape(N, 2, 2, H).transpose(1, 2, 0, 3),
                                ((0, 0), (0, 0), (0, 0), (0, LANES - H)))
            bg = jnp.pad(b_gates_c[j].astype(F32).reshape(2, 2, 1, H), ((0, 0), (0, 0), (0, 0), (0, LANES - H)))
            S = BATCH + DEC_BATCH
            C0 = jnp.concatenate([jnp.zeros((BATCH, 2, H, C_QK_DIM, C_V_DIM), F32), state_C[:, j]], axis=0)
            n0 = jnp.concatenate([jnp.zeros((BATCH, 2, H, C_QK_DIM), F32), state_n[:, j]], axis=0)
            m0 = jnp.concatenate([jnp.zeros((BATCH, 2, H), F32), state_m[:, j]], axis=0)
            S0 = jnp.concatenate([C0, jnp.broadcast_to(n0[..., None], n0.shape + (LANES,))], axis=-1)
            m0 = jnp.broadcast_to(m0[..., None], (S, 2, H, LANES))
            seq_chunks = [SEQ // C_CHUNK] * BATCH + [DEC_SEQ // C_CHUNK] * DEC_BATCH
            h_dir, Sf, mf = mlstm_bidir(proj, gates_dir, bg, S0, m0, seq_chunks)
            x = mlstm_out_residual(h_dir, proj, 2, norm_c_g[j], w_out_c[j].astype(BF16), x, mod, gate_idx=2)
            outs.setdefault("C", []).append(Sf[:BATCH, ..., :C_V_DIM])
            outs.setdefault("n", []).append(Sf[:BATCH, ..., C_V_DIM])
            outs.setdefault("m", []).append(mf[:BATCH, :, :, 0])
        last = l == DEPTH - 1
        pieces = channel_mixer(x, norm2_g[l], mod, w_router_t, b_router, wg_all, wu_all, wd_all, l * n_e,
                               moe_chunks, [(0, Np), (Np, Nl)] if last else [(0, N)],
                               final_norm_g if last else None)
        if last:
            y_prompt = pieces[0].reshape(BATCH, SEQ, D)
            y_sample = pieces[1].reshape(DEC_BATCH, DEC_SEQ, D)
        else:
            x = pieces[0]

    return (y_prompt, y_sample,
            jnp.stack(outs["ak"], axis=1), jnp.stack(outs["av"], axis=1),
            jnp.stack(outs["bk"], axis=1), jnp.stack(outs["bv"], axis=1),
            jnp.stack(outs["C"], axis=1), jnp.stack(outs["n"], axis=1), jnp.stack(outs["m"], axis=1))
```

```python
import functools

import numpy as np
import jax
import jax.numpy as jnp
from jax import lax
from jax.experimental import pallas as pl
from jax.experimental.pallas import tpu as pltpu

F32 = jnp.float32
BF16 = jnp.bfloat16
I32 = jnp.int32

D_MODEL = 2048
BATCH = 16
SEQ = 256
DEPTH = 2
DEC_BATCH = 4
DEC_SEQ = 4096
PAST_LEN = 512
GRID_W = 64
HEAD_DIM = 128
A_HEADS = 8
NA_ROWS = 8
NA_COLS = 16
B_HEADS = 8
B_KV_HEADS = 2
B_WINDOW = 128
ROPE_THETA = 10000.0
C_HEADS = 8
C_QK_DIM = 128
C_V_DIM = 256
C_CHUNK = 128
N_EXPERTS = 16
N_GROUPS = 4
EXPERTS_PER_GROUP = N_EXPERTS // N_GROUPS
D_FF_EXPERT = 1024
EPS = 1e-6

A_WIDTH = A_HEADS * HEAD_DIM
B_Q_WIDTH = B_HEADS * HEAD_DIM
B_KV_WIDTH = B_KV_HEADS * HEAD_DIM
C_QK_WIDTH = C_HEADS * C_QK_DIM
C_V_WIDTH = C_HEADS * C_V_DIM

SEG_ROWS = 4096
NEG_BIG = -1e30
MOE_ROWS = 256
NORM_ROWS = 256
MIB = 1024 * 1024
LANES = 128

_NT = (((1,), (1,)), ((), ()))


def _cparams(sem, vmem_mib):
    return pltpu.CompilerParams(dimension_semantics=sem, vmem_limit_bytes=vmem_mib * MIB)


def _modulated_norm(x, g, mod, shift_idx, scale_idx):
    ms = jnp.mean(x * x, axis=-1, keepdims=True)
    y = x * lax.rsqrt(ms + EPS) * g
    return y * (1.0 + mod[scale_idx:scale_idx + 1, :]) + mod[shift_idx:shift_idx + 1, :]


def _hi_lo(x):
    hi = x.astype(BF16)
    return hi, (x - hi.astype(F32)).astype(BF16)


def _dot3(a, b, dims=None):
    a_hi, a_lo = _hi_lo(a)
    b_hi, b_lo = _hi_lo(b)
    dims = (((a.ndim - 1,), (0,)), ((), ())) if dims is None else dims
    dot = functools.partial(lax.dot_general, dimension_numbers=dims, preferred_element_type=F32)
    return dot(a_hi, b_hi) + dot(a_lo, b_hi) + dot(a_hi, b_lo)


def _ada_kernel(c_ref, w_ref, b_ref, o_ref):
    c = c_ref[...]
    o_ref[0] = _dot3(c * jax.nn.sigmoid(c), w_ref[0]) + b_ref[0]


def ada_mod_all(cvec8, w_ada, b_ada, tn=1024):
    L, D, D6 = w_ada.shape
    return pl.pallas_call(
        _ada_kernel,
        out_shape=jax.ShapeDtypeStruct((L, 8, D6), F32),
        grid=(L, D6 // tn),
        in_specs=[pl.BlockSpec((8, D), lambda l, j: (0, 0)),
                  pl.BlockSpec((1, D, tn), lambda l, j: (l, 0, j)),
                  pl.BlockSpec((1, 1, tn), lambda l, j: (l, 0, j))],
        out_specs=pl.BlockSpec((1, 8, tn), lambda l, j: (l, 0, j)),
        compiler_params=_cparams(("parallel", "parallel"), 40),
        name="ada_mod",
    )(cvec8, w_ada, b_ada.reshape(L, 1, D6))


def _nmm_kernel(x_ref, g_ref, mod_ref, w_ref, *rest, shift_idx, scale_idx, has_aux, precise):
    rest = list(rest)
    waux_ref = rest.pop(0) if has_aux else None
    o_ref = rest.pop(0)
    oaux_ref = rest.pop(0) if has_aux else None
    h_scr = rest.pop(0)
    hlo_scr = rest.pop(0) if precise else None

    @pl.when(pl.program_id(1) == 0)
    def _():
        for r in range(x_ref.shape[0] // NORM_ROWS):
            rows = pl.ds(r * NORM_ROWS, NORM_ROWS)
            hf = _modulated_norm(x_ref[rows, :], g_ref[...], mod_ref[0], shift_idx, scale_idx)
            h = hf.astype(BF16)
            h_scr[rows, :] = h
            if precise:
                hlo_scr[rows, :] = (hf - h.astype(F32)).astype(BF16)
            if has_aux:
                oaux_ref[rows, :] = jnp.dot(h, waux_ref[...].astype(BF16), preferred_element_type=F32)

    if precise:
        w_hi, w_lo = _hi_lo(w_ref[...])
        acc = (jnp.dot(h_scr[...], w_hi, preferred_element_type=F32)
               + jnp.dot(hlo_scr[...], w_hi, preferred_element_type=F32)
               + jnp.dot(h_scr[...], w_lo, preferred_element_type=F32))
    else:
        acc = jnp.dot(h_scr[...], w_ref[...].astype(BF16), preferred_element_type=F32)
    o_ref[...] = acc.astype(o_ref.dtype)


def norm_mod_matmul(x, row_off, n_rows, g, mod, w, out_dtype, shift_idx, scale_idx, w_aux=None, n_out=None,
                    seg_row_off=None, precise=False, tm=1024, tn=512):
    D = x.shape[1]
    n_out = w.shape[1] if n_out is None else n_out
    off_b = row_off // tm
    seg_b = off_b if seg_row_off is None else seg_row_off // tm
    per_seg = SEG_ROWS // tm
    has_aux = w_aux is not None
    in_specs = [pl.BlockSpec((tm, D), lambda i, j: (i + off_b, 0)),
                pl.BlockSpec((1, D), lambda i, j: (0, 0)),
                pl.BlockSpec((1, 6, D), lambda i, j: ((i + seg_b) // per_seg, 0, 0)),
                pl.BlockSpec((D, tn), lambda i, j: (0, j))]
    out_shape = jax.ShapeDtypeStruct((n_rows, n_out), out_dtype)
    out_specs = pl.BlockSpec((tm, tn), lambda i, j: (i, j))
    args = [x, g.reshape(1, D), mod, w]
    if has_aux:
        n_aux = w_aux.shape[1]
        in_specs.append(pl.BlockSpec((D, n_aux), lambda i, j: (0, 0)))
        out_shape = (out_shape, jax.ShapeDtypeStruct((n_rows, n_aux), F32))
        out_specs = (out_specs, pl.BlockSpec((tm, n_aux), lambda i, j: (i, 0)))
        args.append(w_aux)
    return pl.pallas_call(
        functools.partial(_nmm_kernel, shift_idx=shift_idx, scale_idx=scale_idx, has_aux=has_aux,
                          precise=precise),
        out_shape=out_shape,
        grid=(n_rows // tm, n_out // tn),
        in_specs=in_specs,
        out_specs=out_specs,
        scratch_shapes=[pltpu.VMEM((tm, D), BF16)] * (2 if precise else 1),
        compiler_params=_cparams(("parallel", "arbitrary"), 52),
        name="norm_mod_matmul",
    )(*args)


def _attn_out_kernel(oc_ref, oa_ref, ob_ref, w_ref, xc_ref, xl_ref, mod_ref, o_ref, *, gate_idx, n_ctx_tiles):
    i = pl.program_id(0)
    gate = mod_ref[0, gate_idx:gate_idx + 1, :]

    @pl.when(i < n_ctx_tiles)
    def _():
        oc_hi, oc_lo = _hi_lo(oc_ref[...])
        acc = (jnp.dot(oc_hi, w_ref[0], preferred_element_type=F32)
               + jnp.dot(oc_lo, w_ref[0], preferred_element_type=F32)
               + jnp.dot(oc_hi, w_ref[1], preferred_element_type=F32))
        o_ref[...] = xc_ref[...] + gate * acc

    @pl.when(i >= n_ctx_tiles)
    def _():
        a = jnp.concatenate([oa_ref[...], ob_ref[...]], axis=-1)
        acc = jnp.dot(a, w_ref[0], preferred_element_type=F32)
        o_ref[...] = xl_ref[...] + gate * acc


def attn_out_residual(o_ctx, oa, ob, w, x_ctx, x_lat, mod, gate_idx, tm=256):
    n_ctx, K = o_ctx.shape
    n_lat = oa.shape[0]
    D = w.shape[2]
    nct = n_ctx // tm
    per_seg = SEG_ROWS // tm

    def ctx_map(i):
        return (jnp.minimum(i, nct - 1), 0)

    def lat_map(i):
        return (jnp.maximum(i - nct, 0), 0)

    return pl.pallas_call(
        functools.partial(_attn_out_kernel, gate_idx=gate_idx, n_ctx_tiles=nct),
        out_shape=jax.ShapeDtypeStruct((n_ctx + n_lat, D), F32),
        grid=((n_ctx + n_lat) // tm,),
        in_specs=[pl.BlockSpec((tm, K), ctx_map),
                  pl.BlockSpec((tm, oa.shape[1]), lat_map),
                  pl.BlockSpec((tm, ob.shape[1]), lat_map),
                  pl.BlockSpec((2, K, D), lambda i: (0, 0, 0), pipeline_mode=pl.Buffered(1)),
                  pl.BlockSpec((tm, D), ctx_map),
                  pl.BlockSpec((tm, D), lat_map),
                  pl.BlockSpec((1, 6, D), lambda i: (i // per_seg, 0, 0))],
        out_specs=pl.BlockSpec((tm, D), lambda i: (i, 0)),
        compiler_params=_cparams(("arbitrary",), 48),
        name="attn_out_residual",
    )(o_ctx, oa, ob, w, x_ctx, x_lat, mod)


def _mlstm_out_kernel(hf_ref, hb_ref, o_ref, ng_ref, w_ref, x_ref, mod_ref, out_ref, *, gate_idx):
    hs = hf_ref[...] + hb_ref[...]
    parts = []
    for h in range(C_HEADS):
        sl = slice(h * C_V_DIM, (h + 1) * C_V_DIM)
        xs = hs[:, sl]
        ms = jnp.mean(xs * xs, axis=-1, keepdims=True)
        hn = xs * lax.rsqrt(ms + EPS) * ng_ref[:, sl]
        parts.append((jax.nn.sigmoid(o_ref[:, sl].astype(F32)) * hn).astype(BF16))
    a = jnp.concatenate(parts, axis=-1)
    acc = jnp.dot(a, w_ref[...], preferred_element_type=F32)
    out_ref[...] = x_ref[...] + mod_ref[0, gate_idx:gate_idx + 1, :] * acc


def mlstm_out_residual(h_dir, proj, o_col_block, norm_g, w, x, mod, gate_idx, tm=256):
    n, D = x.shape
    V = C_V_WIDTH
    per_seg = SEG_ROWS // tm
    return pl.pallas_call(
        functools.partial(_mlstm_out_kernel, gate_idx=gate_idx),
        out_shape=jax.ShapeDtypeStruct((n, D), F32),
        grid=(n // tm,),
        in_specs=[pl.BlockSpec((None, tm, V), lambda i: (0, i, 0)),
                  pl.BlockSpec((None, tm, V), lambda i: (1, i, 0)),
                  pl.BlockSpec((tm, V), lambda i: (i, o_col_block)),
                  pl.BlockSpec((1, V), lambda i: (0, 0)),
                  pl.BlockSpec((V, D), lambda i: (0, 0)),
                  pl.BlockSpec((tm, D), lambda i: (i, 0)),
                  pl.BlockSpec((1, 6, D), lambda i: (i // per_seg, 0, 0))],
        out_specs=pl.BlockSpec((tm, D), lambda i: (i, 0)),
        compiler_params=_cparams(("parallel",), 48),
        name="mlstm_out_residual",
    )(h_dir, h_dir, proj, norm_g.reshape(1, V), w, x, mod)


def _ctx_attn_kernel(sink_ref, q_ref, k_ref, v_ref, o_ref):
    h = pl.program_id(1)
    s = _dot3(q_ref[...], k_ref[...], _NT) * (HEAD_DIM ** -0.5)
    sk = sink_ref[h]
    m = jnp.maximum(jnp.max(s, axis=-1, keepdims=True), sk)
    p = jnp.exp(s - m)
    l = jnp.sum(p, axis=-1, keepdims=True) + jnp.exp(sk - m)
    o_ref[...] = (_dot3(p, v_ref[...]) / l).astype(o_ref.dtype)


def ctx_attention(proj, sink_b, n_batch, seq):
    n_heads = A_HEADS + B_HEADS
    group = B_HEADS // B_KV_HEADS
    qb0 = 3 * A_HEADS
    kb0 = qb0 + B_HEADS
    vb0 = kb0 + B_KV_HEADS
    sinks = jnp.concatenate([jnp.full((A_HEADS,), NEG_BIG, F32), sink_b.astype(F32)])

    def q_map(b, h, s):
        return (b, jnp.where(h < A_HEADS, h, qb0 + h - A_HEADS))

    def k_map(b, h, s):
        return (b, jnp.where(h < A_HEADS, A_HEADS + h, kb0 + (h - A_HEADS) // group))

    def v_map(b, h, s):
        return (b, jnp.where(h < A_HEADS, 2 * A_HEADS + h, vb0 + (h - A_HEADS) // group))

    blk = (seq, HEAD_DIM)
    return pl.pallas_call(
        _ctx_attn_kernel,
        out_shape=jax.ShapeDtypeStruct((n_batch * seq, n_heads * HEAD_DIM), F32),
        grid_spec=pltpu.PrefetchScalarGridSpec(
            num_scalar_prefetch=1,
            grid=(n_batch, n_heads),
            in_specs=[pl.BlockSpec(blk, q_map), pl.BlockSpec(blk, k_map), pl.BlockSpec(blk, v_map)],
            out_specs=pl.BlockSpec(blk, lambda b, h, s: (b, h))),
        compiler_params=_cparams(("parallel", "parallel"), 32),
        name="ctx_attention",
    )(sinks, proj, proj, proj)


NAT_QROWS = 4


def nat_bias_mask(rel_bias, rows):
    W = GRID_W
    nb = rows // NAT_QROWS
    kh = min(NA_ROWS, rows)
    n_dr, n_dc = 2 * NA_ROWS - 1, 2 * NA_COLS - 1
    H = rel_bias.shape[0]
    cidx = np.clip(np.arange(W)[None, :] - np.arange(W)[:, None] + NA_COLS - 1, 0, n_dc - 1)
    onehot = jnp.asarray((cidx.reshape(1, W * W) == np.arange(n_dc)[:, None]).astype(np.float32))
    col_bias = jnp.dot(rel_bias.astype(F32).reshape(H * n_dr, n_dc), onehot,
                       precision=lax.Precision.HIGHEST).reshape(H, n_dr, W, W)
    q_rows = []
    for qi in range(NAT_QROWS):
        tiles = [col_bias[:, int(np.clip(NAT_QROWS * (kj - 1) + kjr - qi + NA_ROWS - 1, 0, n_dr - 1))]
                 for kj in range(3) for kjr in range(NAT_QROWS)]
        q_rows.append(jnp.concatenate(tiles, axis=-1))
    bias = jnp.concatenate(q_rows, axis=1)
    variants = []
    for g in (0, 1, nb - 1):
        i = np.arange(NAT_QROWS)[:, None, None, None, None]
        qc = np.arange(W)[None, :, None, None, None]
        j = np.arange(3)[None, None, :, None, None]
        jr = np.arange(NAT_QROWS)[None, None, None, :, None]
        kc = np.arange(W)[None, None, None, None, :]
        r = NAT_QROWS * g + i
        kblk = g - 1 + j
        kr = NAT_QROWS * kblk + jr
        rs = np.clip(r - kh // 2, 0, rows - kh)
        row_ok = (kblk >= 0) & (kblk < nb) & (kr >= rs) & (kr < rs + kh)
        cstart = np.clip(qc - NA_COLS // 2, 0, W - NA_COLS)
        col_ok = (kc >= cstart) & (kc < cstart + NA_COLS)
        ok = np.broadcast_to(row_ok & col_ok, (NAT_QROWS, W, 3, NAT_QROWS, W))
        n_q, n_k = NAT_QROWS * W, 3 * NAT_QROWS * W
        variants.append(jnp.where(jnp.asarray(ok.reshape(1, n_q, n_k)), bias, NEG_BIG))
    return jnp.stack(variants, axis=0)


NAT_HEADS_PER_STEP = 4


def _nat_kernel(q_ref, k0_ref, k1_ref, k2_ref, v0_ref, v1_ref, v2_ref, kc_ref, vc_ref, bm_ref, o_ref):
    scale = HEAD_DIM ** -0.5
    tq = q_ref.shape[0]
    for h in range(q_ref.shape[1] // HEAD_DIM):
        cs = slice(h * HEAD_DIM, (h + 1) * HEAD_DIM)
        q = q_ref[:, cs]
        s_lat = [lax.dot_general(q, k_ref[:, cs], _NT, preferred_element_type=F32) * scale
                 + bm_ref[0, h, :, j * tq:(j + 1) * tq]
                 for j, k_ref in enumerate((k0_ref, k1_ref, k2_ref))]
        s_ctx = lax.dot_general(q, kc_ref[0, :, cs], _NT, preferred_element_type=F32) * scale
        m = jnp.max(s_ctx, axis=-1, keepdims=True)
        for s in s_lat:
            m = jnp.maximum(m, jnp.max(s, axis=-1, keepdims=True))
        p_ctx = jnp.exp(s_ctx - m)
        l = jnp.sum(p_ctx, axis=-1, keepdims=True)
        acc = jnp.dot(p_ctx.astype(BF16), vc_ref[0, :, cs], preferred_element_type=F32)
        for s, v_ref in zip(s_lat, (v0_ref, v1_ref, v2_ref)):
            p = jnp.exp(s - m)
            l = l + jnp.sum(p, axis=-1, keepdims=True)
            acc = acc + jnp.dot(p.astype(BF16), v_ref[:, cs], preferred_element_type=F32)
        o_ref[:, cs] = (acc / l).astype(o_ref.dtype)


def nat_attention(proj, cache_k, cache_v, bias_mask, n_batch, T, n_heads):
    tq = NAT_QROWS * GRID_W
    nb = T // tq
    P = cache_k.shape[1]
    hs = min(NAT_HEADS_PER_STEP, n_heads)
    ng = n_heads // hs

    def kv_map(col0, j):
        return lambda b, h, g: (b * nb + jnp.clip(g - 1 + j, 0, nb - 1), col0 + h)

    blk = (tq, hs * HEAD_DIM)
    in_specs = [pl.BlockSpec(blk, lambda b, h, g: (b * nb + g, h))]
    in_specs += [pl.BlockSpec(blk, kv_map(ng, j)) for j in range(3)]
    in_specs += [pl.BlockSpec(blk, kv_map(2 * ng, j)) for j in range(3)]
    in_specs += [pl.BlockSpec((1, P, hs * HEAD_DIM), lambda b, h, g: (b, 0, h))] * 2
    in_specs += [pl.BlockSpec((1, hs, tq, 3 * tq),
                              lambda b, h, g: (jnp.where(g == 0, 0, jnp.where(g == nb - 1, 2, 1)), h, 0, 0))]
    return pl.pallas_call(
        _nat_kernel,
        out_shape=jax.ShapeDtypeStruct((n_batch * T, n_heads * HEAD_DIM), BF16),
        grid=(n_batch, ng, nb),
        in_specs=in_specs,
        out_specs=pl.BlockSpec(blk, lambda b, h, g: (b * nb + g, h)),
        compiler_params=_cparams(("parallel", "parallel", "arbitrary"), 40),
        name="nat_attention",
    )(proj, proj, proj, proj, proj, proj, proj, cache_k, cache_v, bias_mask)


def rope_tables(T):
    t = jnp.arange(T)
    row = (t // GRID_W).astype(F32)
    col = (t % GRID_W).astype(F32)
    nf = HEAD_DIM // 4
    freqs = ROPE_THETA ** (-jnp.arange(nf, dtype=F32) / nf)
    ar = row[:, None] * freqs
    ac = col[:, None] * freqs
    cos = jnp.concatenate([jnp.cos(ar), jnp.cos(ar), jnp.cos(ac), jnp.cos(ac)], axis=-1)
    sin = jnp.concatenate([-jnp.sin(ar), jnp.sin(ar), -jnp.sin(ac), jnp.sin(ac)], axis=-1)
    return cos, sin


def _rope_kernel(q_ref, k_ref, cos_ref, sin_ref, o_ref):
    nf = HEAD_DIM // 4
    cos = cos_ref[...]
    sin = sin_ref[...]
    lane = lax.broadcasted_iota(I32, cos.shape, 1)
    first_half = (lane & nf) == 0
    col = 0
    for src in (q_ref, k_ref):
        for h in range(src.shape[1] // HEAD_DIM):
            x = src[:, h * HEAD_DIM:(h + 1) * HEAD_DIM].astype(F32)
            upper = pltpu.roll(x, HEAD_DIM - nf, 1)
            lower = pltpu.roll(x, nf, 1)
            partner = jnp.where(first_half, upper, lower)
            o_ref[:, col:col + HEAD_DIM] = (x * cos + partner * sin).astype(o_ref.dtype)
            col += HEAD_DIM


def rope_heads(proj, q_col0, k_col0, cos, sin, T, tm=512):
    n = proj.shape[0]
    per_seq = T // tm
    return pl.pallas_call(
        _rope_kernel,
        out_shape=jax.ShapeDtypeStruct((n, B_Q_WIDTH + B_KV_WIDTH), BF16),
        grid=(n // tm,),
        in_specs=[pl.BlockSpec((tm, B_Q_WIDTH), lambda i: (i, q_col0 // B_Q_WIDTH)),
                  pl.BlockSpec((tm, B_KV_WIDTH), lambda i: (i, k_col0 // B_KV_WIDTH)),
                  pl.BlockSpec((tm, HEAD_DIM), lambda i: (i % per_seq, 0)),
                  pl.BlockSpec((tm, HEAD_DIM), lambda i: (i % per_seq, 0))],
        out_specs=pl.BlockSpec((tm, B_Q_WIDTH + B_KV_WIDTH), lambda i: (i, 0)),
        compiler_params=_cparams(("parallel",), 32),
        name="rope_heads",
    )(proj, proj, cos, sin)


SWA_TQ = 2 * B_WINDOW


def _swa_kernel(sink_ref, q_ref, k0_ref, k1_ref, k2_ref, k3_ref, v0_ref, v1_ref, v2_ref, v3_ref,
                kc_ref, vc_ref, o_ref, *, T):
    scale = HEAD_DIM ** -0.5
    group = B_HEADS // B_KV_HEADS
    kvh = pl.program_id(1)
    n = pl.program_id(2)
    k = jnp.concatenate([k0_ref[...], k1_ref[...], k2_ref[...], k3_ref[...]], axis=0)
    v = jnp.concatenate([v0_ref[...], v1_ref[...], v2_ref[...], v3_ref[...]], axis=0)
    kc = kc_ref[0].astype(BF16)
    vc = vc_ref[0].astype(BF16)
    nk = k.shape[0]
    qpos = n * SWA_TQ + lax.broadcasted_iota(I32, (SWA_TQ, nk), 0)
    kpos = n * SWA_TQ - B_WINDOW + lax.broadcasted_iota(I32, (SWA_TQ, nk), 1)
    dist = jnp.abs(qpos - kpos)
    ok = jnp.where(kpos >= 0, jnp.where(kpos < T, dist, B_WINDOW + 1), B_WINDOW + 1) <= B_WINDOW
    outs = []
    for gi in range(group):
        q = q_ref[:, gi * HEAD_DIM:(gi + 1) * HEAD_DIM]
        s_lat = jnp.where(ok, lax.dot_general(q, k, _NT, preferred_element_type=F32) * scale, NEG_BIG)
        s_ctx = lax.dot_general(q, kc, _NT, preferred_element_type=F32) * scale
        sk = sink_ref[kvh * group + gi]
        m = jnp.maximum(jnp.maximum(jnp.max(s_lat, axis=-1, keepdims=True),
                                    jnp.max(s_ctx, axis=-1, keepdims=True)), sk)
        p_lat = jnp.exp(s_lat - m)
        p_ctx = jnp.exp(s_ctx - m)
        l = (jnp.sum(p_lat, axis=-1, keepdims=True) + jnp.sum(p_ctx, axis=-1, keepdims=True)
             + jnp.exp(sk - m))
        acc = (jnp.dot(p_lat.astype(BF16), v, preferred_element_type=F32)
               + jnp.dot(p_ctx.astype(BF16), vc, preferred_element_type=F32))
        outs.append((acc / l).astype(o_ref.dtype))
    o_ref[...] = jnp.concatenate(outs, axis=-1)


def swa_attention(qk_rot, proj, v_col0, cache_k, cache_v, sink, n_batch, T):
    group = B_HEADS // B_KV_HEADS
    nq = T // SWA_TQ
    nkb = T // B_WINDOW
    P = cache_k.shape[1]

    def kv_map(col0, j):
        return lambda b, kvh, n, s: (b * nkb + jnp.clip(2 * n - 1 + j, 0, nkb - 1), col0 + kvh)

    kblk = (B_WINDOW, HEAD_DIM)
    in_specs = [pl.BlockSpec((SWA_TQ, group * HEAD_DIM), lambda b, kvh, n, s: (b * nq + n, kvh))]
    in_specs += [pl.BlockSpec(kblk, kv_map(B_HEADS, j)) for j in range(4)]
    in_specs += [pl.BlockSpec(kblk, kv_map(v_col0, j)) for j in range(4)]
    in_specs += [pl.BlockSpec((1, P, HEAD_DIM), lambda b, kvh, n, s: (b, 0, kvh))] * 2
    return pl.pallas_call(
        functools.partial(_swa_kernel, T=T),
        out_shape=jax.ShapeDtypeStruct((n_batch * T, B_Q_WIDTH), BF16),
        grid_spec=pltpu.PrefetchScalarGridSpec(
            num_scalar_prefetch=1,
            grid=(n_batch, B_KV_HEADS, nq),
            in_specs=in_specs,
            out_specs=pl.BlockSpec((SWA_TQ, group * HEAD_DIM), lambda b, kvh, n, s: (b * nq + n, kvh))),
        compiler_params=_cparams(("parallel", "parallel", "arbitrary"), 32),
        name="swa_attention",
    )(sink.astype(F32), qk_rot, qk_rot, qk_rot, qk_rot, qk_rot, proj, proj, proj, proj, cache_k, cache_v)


def _mlstm_kernel(rowblk_ref, seq_ref, first_ref, last_ref, *refs):
    s = pl.program_id(1)
    for d in range(2):
        @pl.when(pl.program_id(0) == d)
        def _(d=d):
            _mlstm_chunk(d, first_ref[s] == 1, last_ref[s] == 1, *refs)


def _mlstm_chunk(d, is_first, is_last, q_ref, k_ref, v_ref, g_ref, bg_ref, S0_ref, m0_ref,
                 h_ref, Sf_ref, mf_ref, S_scr, m_scr):
    H, DK, DV, L, R = C_HEADS, C_QK_DIM, C_V_DIM, C_CHUNK, LANES
    scale = DK ** -0.5

    @pl.when(is_first)
    def _():
        S_scr[...] = S0_ref[0, 0]
        m_scr[...] = m0_ref[0, 0]

    gi = g_ref[0, 0] + bg_ref[0, 0]
    lf = jax.nn.log_sigmoid(g_ref[0, 1] + bg_ref[0, 1])
    row = lax.broadcasted_iota(I32, (L, L), 0)
    col = lax.broadcasted_iota(I32, (L, L), 1)
    causal = col <= row if d == 0 else col >= row
    b_all = jnp.dot(causal.astype(F32), lf, precision=lax.Precision.HIGHEST,
                    preferred_element_type=F32)
    a_row = (gi - b_all).T[0:H, :]
    b_row = b_all.T[0:H, :]
    lane = lax.broadcasted_iota(I32, (H, L), 1)
    cm = a_row
    k = 1
    while k < L:
        if d == 0:
            shifted = jnp.where(lane >= k, pltpu.roll(cm, k, 1), -jnp.inf)
        else:
            shifted = jnp.where(lane < L - k, pltpu.roll(cm, L - k, 1), -jnp.inf)
        cm = jnp.maximum(cm, shifted)
        k *= 2
    end = L - 1 if d == 0 else 0
    m_all = m_scr[...]
    M_row = jnp.maximum(m_all, cm)
    wi_row = jnp.exp(m_all - M_row)
    emt_row = jnp.exp(-(b_row + M_row))
    M_last = M_row[:, end:end + 1]
    b_last = b_row[:, end:end + 1]
    ws_row = jnp.exp(a_row - M_last)
    wc_all = jnp.exp(m_all - M_last)
    m_scr[...] = jnp.broadcast_to(b_last + M_last, m_all.shape)
    cols = jnp.concatenate([M_row, wi_row, emt_row, jnp.zeros((L - 3 * H, L), F32)], axis=0).T
    ones = jnp.ones((L, R), BF16)
    for h in range(H):
        M_col = cols[:, h:h + 1]
        wi_col = cols[:, H + h:H + h + 1]
        emt_col = cols[:, 2 * H + h:2 * H + h + 1]
        qh = q_ref[:, h * DK:(h + 1) * DK]
        kh = k_ref[:, h * DK:(h + 1) * DK]
        vh = v_ref[:, h * DV:(h + 1) * DV]
        v_ext = jnp.concatenate([vh, ones], axis=-1)
        Sh = S_scr[h]

        w = jnp.where(causal, jnp.exp(a_row[h:h + 1, :] - M_col), 0.0)
        sqk = lax.dot_general(qh, kh, _NT, preferred_element_type=F32) * scale * w
        S_hi = Sh.astype(BF16)
        n_lo = (Sh[:, DV:] - S_hi[:, DV:].astype(F32)).astype(BF16)
        inter = jnp.dot(qh, jnp.concatenate([S_hi, n_lo], axis=-1),
                        preferred_element_type=F32) * scale
        qn = inter[:, DV:DV + 1] + inter[:, DV + R:DV + R + 1]
        num = jnp.dot(sqk.astype(BF16), vh, preferred_element_type=F32) + wi_col * inter[:, :DV]
        den = jnp.sum(sqk, axis=-1, keepdims=True) + wi_col * qn
        h_ref[:, h * DV:(h + 1) * DV] = num / jnp.maximum(jnp.abs(den), emt_col)

        kwT = kh.astype(F32).T * ws_row[h:h + 1, :]
        kwT_hi = kwT.astype(BF16)
        kwT_lo = (kwT - kwT_hi.astype(F32)).astype(BF16)
        upd = jnp.dot(kwT_hi, v_ext, preferred_element_type=F32)
        upd_n = upd[:, DV:] + jnp.dot(kwT_lo, ones, preferred_element_type=F32)
        wc = jnp.concatenate([wc_all[h:h + 1, :]] * (DV // R + 1), axis=-1)
        S_scr[h] = wc * Sh + jnp.concatenate([upd[:, :DV], upd_n], axis=-1)

    @pl.when(is_last)
    def _():
        Sf_ref[0, 0] = S_scr[...]
        mf_ref[0, 0] = m_scr[...]


def mlstm_bidir(proj, gates_dir, b_gates_dir, S0, m0, seq_chunks):
    H, DK, DV, L = C_HEADS, C_QK_DIM, C_V_DIM, C_CHUNK
    DS = DV + LANES
    N = proj.shape[0]
    S = len(seq_chunks)
    rowblk, seq_id, first, last = [[], []], [], [], []
    base = 0
    for sq, nc in enumerate(seq_chunks):
        rowblk[0] += [base + c for c in range(nc)]
        rowblk[1] += [base + nc - 1 - c for c in range(nc)]
        seq_id += [sq] * nc
        first += [1] + [0] * (nc - 1)
        last += [0] * (nc - 1) + [1]
        base += nc
    n_steps = base
    rowblk = jnp.asarray(np.array(rowblk, np.int32).reshape(-1))
    tables = (rowblk, jnp.asarray(seq_id, I32), jnp.asarray(first, I32), jnp.asarray(last, I32))

    def row_map(colblk):
        return lambda d, s, rb, sq, fi, la: (rb[d * n_steps + s], colblk)

    def state_map(nd):
        return lambda d, s, rb, sq, fi, la: (sq[s], d) + (0,) * nd

    in_specs = [pl.BlockSpec((L, H * DK), row_map(0)),
                pl.BlockSpec((L, H * DK), row_map(1)),
                pl.BlockSpec((L, H * DV), row_map(1)),
                pl.BlockSpec((1, 2, L, LANES), lambda d, s, rb, sq, fi, la: (d, 0, rb[d * n_steps + s], 0)),
                pl.BlockSpec((1, 2, 1, LANES), lambda d, s, rb, sq, fi, la: (d, 0, 0, 0)),
                pl.BlockSpec((1, 1, H, DK, DS), state_map(3)),
                pl.BlockSpec((1, 1, H, LANES), state_map(2))]
    out_specs = (pl.BlockSpec((None, L, H * DV), lambda d, s, rb, sq, fi, la: (d, rb[d * n_steps + s], 0)),
                 pl.BlockSpec((1, 1, H, DK, DS), state_map(3)),
                 pl.BlockSpec((1, 1, H, LANES), state_map(2)))
    out_shape = (jax.ShapeDtypeStruct((2, N, H * DV), F32),
                 jax.ShapeDtypeStruct((S, 2, H, DK, DS), F32),
                 jax.ShapeDtypeStruct((S, 2, H, LANES), F32))
    return pl.pallas_call(
        _mlstm_kernel,
        out_shape=out_shape,
        grid_spec=pltpu.PrefetchScalarGridSpec(
            num_scalar_prefetch=4,
            grid=(2, n_steps),
            in_specs=in_specs,
            out_specs=out_specs,
            scratch_shapes=[pltpu.VMEM((H, DK, DS), F32), pltpu.VMEM((H, LANES), F32)]),
        compiler_params=_cparams(("arbitrary", "arbitrary"), 40),
        name="mlstm_bidir",
    )(*tables, proj, proj, proj, gates_dir, b_gates_dir, S0, m0)


def _top2_of4(vals):
    m1, i1 = vals[0], jnp.zeros(vals[0].shape, I32)
    for j in range(1, 4):
        better = vals[j] > m1
        m1 = jnp.where(better, vals[j], m1)
        i1 = jnp.where(better, j, i1)
    m2, i2 = jnp.full(vals[0].shape, -jnp.inf, F32), jnp.zeros(vals[0].shape, I32)
    for j in range(4):
        cand = jnp.where(i1 == j, -jnp.inf, vals[j])
        better = cand > m2
        m2 = jnp.where(better, cand, m2)
        i2 = jnp.where(better, j, i2)
    return m1, i1, m2, i2


def _router_kernel(x_ref, g_ref, mod_ref, wr_ref, br_ref, h_ref, ids_ref, wts_ref, *, shift_idx, scale_idx):
    h = _modulated_norm(x_ref[...], g_ref[...], mod_ref[0], shift_idx, scale_idx)
    h_hi = h.astype(BF16)
    h_ref[...] = h_hi
    h_lo = (h - h_hi.astype(F32)).astype(BF16)
    w_hi = wr_ref[0]
    w_lo = wr_ref[1]
    logits = (lax.dot_general(w_hi, h_hi, _NT, preferred_element_type=F32)
              + lax.dot_general(w_lo, h_hi, _NT, preferred_element_type=F32)
              + lax.dot_general(w_hi, h_lo, _NT, preferred_element_type=F32))
    aff = jax.nn.sigmoid(logits)
    sel = aff + br_ref[...]
    aff_rows = [aff[e:e + 1, :] for e in range(N_EXPERTS)]
    sel_rows = [sel[e:e + 1, :] for e in range(N_EXPERTS)]
    tops = [_top2_of4(sel_rows[4 * gidx:4 * gidx + 4]) for gidx in range(N_GROUPS)]
    best = tops[0][0] + tops[0][2]
    grp = jnp.zeros(best.shape, I32)
    i1, i2 = tops[0][1], tops[0][3]
    for gidx in range(1, N_GROUPS):
        score = tops[gidx][0] + tops[gidx][2]
        better = score > best
        best = jnp.where(better, score, best)
        grp = jnp.where(better, gidx, grp)
        i1 = jnp.where(better, tops[gidx][1], i1)
        i2 = jnp.where(better, tops[gidx][3], i2)
    e1 = grp * EXPERTS_PER_GROUP + i1
    e2 = grp * EXPERTS_PER_GROUP + i2
    w1 = jnp.zeros(best.shape, F32)
    w2 = jnp.zeros(best.shape, F32)
    for e in range(N_EXPERTS):
        w1 = jnp.where(e1 == e, aff_rows[e], w1)
        w2 = jnp.where(e2 == e, aff_rows[e], w2)
    tot = w1 + w2
    ids_ref[...] = jnp.zeros(ids_ref.shape, I32)
    wts_ref[...] = jnp.zeros(wts_ref.shape, F32)
    ids_ref[0:1, :] = e1
    ids_ref[1:2, :] = e2
    wts_ref[0:1, :] = w1 / tot
    wts_ref[1:2, :] = w2 / tot


def router_weight_pair(w_router):
    w_t = w_router.T.astype(F32)
    w_hi = w_t.astype(BF16)
    return jnp.stack([w_hi, (w_t - w_hi.astype(F32)).astype(BF16)], axis=0)


def moe_router(x, row_off, n_rows, g, mod, w_router_t, b_router, shift_idx, scale_idx, tm=512):
    D = x.shape[1]
    off_b = row_off // tm
    per_seg = SEG_ROWS // tm
    return pl.pallas_call(
        functools.partial(_router_kernel, shift_idx=shift_idx, scale_idx=scale_idx),
        out_shape=(jax.ShapeDtypeStruct((n_rows, D), BF16),
                   jax.ShapeDtypeStruct((8, n_rows), I32),
                   jax.ShapeDtypeStruct((8, n_rows), F32)),
        grid=(n_rows // tm,),
        in_specs=[pl.BlockSpec((tm, D), lambda i: (i + off_b, 0)),
                  pl.BlockSpec((1, D), lambda i: (0, 0)),
                  pl.BlockSpec((1, 6, D), lambda i: ((i + off_b) // per_seg, 0, 0)),
                  pl.BlockSpec((2, N_EXPERTS, D), lambda i: (0, 0, 0)),
                  pl.BlockSpec((N_EXPERTS, 1), lambda i: (0, 0))],
        out_specs=(pl.BlockSpec((tm, D), lambda i: (i, 0)),
                   pl.BlockSpec((8, tm), lambda i: (0, i)),
                   pl.BlockSpec((8, tm), lambda i: (0, i))),
        compiler_params=_cparams(("parallel",), 40),
        name="moe_router",
    )(x, g.reshape(1, D), mod, w_router_t, b_router.reshape(N_EXPERTS, 1).astype(F32))


MOE_CAST_ROWS = 128


def _moe_expert_kernel(be_ref, first_ref, next_ref, nu_ref, x_ref, wg_hbm, wu_hbm, wd_hbm, y_ref,
                       wg_st, wu_st, wd_st, wg_bf, wu_bf, wd_bf, sems, *, e_off):
    i = pl.program_id(0)

    def weight_copies(e):
        return (pltpu.make_async_copy(wg_hbm.at[e_off + e], wg_st, sems.at[0]),
                pltpu.make_async_copy(wu_hbm.at[e_off + e], wu_st, sems.at[1]),
                pltpu.make_async_copy(wd_hbm.at[e_off + e], wd_st, sems.at[2]))

    @pl.when(i == 0)
    def _():
        for cp in weight_copies(be_ref[0]):
            cp.start()

    @pl.when(first_ref[i] == 1)
    def _():
        for cp in weight_copies(be_ref[i]):
            cp.wait()
        for st, bf in ((wg_st, wg_bf), (wu_st, wu_bf), (wd_st, wd_bf)):
            def cast_rows(c, carry, st=st, bf=bf):
                r = pl.multiple_of(c * MOE_CAST_ROWS, MOE_CAST_ROWS)
                bf[pl.ds(r, MOE_CAST_ROWS), :] = st[pl.ds(r, MOE_CAST_ROWS), :].astype(BF16)
                return carry
            lax.fori_loop(0, st.shape[0] // MOE_CAST_ROWS, cast_rows, 0)

        @pl.when(next_ref[i] >= 0)
        def _():
            for cp in weight_copies(next_ref[i]):
                cp.start()

    @pl.when(i < nu_ref[0])
    def _():
        x = x_ref[...]
        gate = jnp.dot(x, wg_bf[...], preferred_element_type=F32)
        up = jnp.dot(x, wu_bf[...], preferred_element_type=F32)
        act = (gate * jax.nn.sigmoid(gate) * up).astype(BF16)
        y_ref[...] = jnp.dot(act, wd_bf[...], preferred_element_type=F32).astype(y_ref.dtype)

    @pl.when(i >= nu_ref[0])
    def _():
        y_ref[...] = jnp.zeros(y_ref.shape, y_ref.dtype)


def moe_experts(xg, plan, wg, wu, wd, e_off):
    R, D = xg.shape
    F = wg.shape[2]
    nb = R // MOE_ROWS
    hbm = pl.BlockSpec(memory_space=pl.ANY)
    return pl.pallas_call(
        functools.partial(_moe_expert_kernel, e_off=e_off),
        out_shape=jax.ShapeDtypeStruct((R, D), BF16),
        grid_spec=pltpu.PrefetchScalarGridSpec(
            num_scalar_prefetch=4,
            grid=(nb,),
            in_specs=[pl.BlockSpec((MOE_ROWS, D), lambda i, *_: (i, 0)), hbm, hbm, hbm],
            out_specs=pl.BlockSpec((MOE_ROWS, D), lambda i, *_: (i, 0)),
            scratch_shapes=[pltpu.VMEM((D, F), F32), pltpu.VMEM((D, F), F32), pltpu.VMEM((F, D), F32),
                            pltpu.VMEM((D, F), BF16), pltpu.VMEM((D, F), BF16), pltpu.VMEM((F, D), BF16),
                            pltpu.SemaphoreType.DMA((3,))]),
        compiler_params=_cparams(("arbitrary",), 56),
        name="moe_experts",
    )(plan["block_e"], plan["first"], plan["next_e"], plan["n_used"], xg, wg, wu, wd)


def _combine_kernel(x_ref, mod_ref, *rest, gate_idx, final, tile_off, chunk_tiles):
    rest = list(rest)
    fg_ref = rest.pop(0) if final else None
    o_ref = rest.pop()
    t = pl.program_id(0) + tile_off
    for c, (lo, n) in enumerate(chunk_tiles):
        y0_ref, y1_ref, w_ref = rest[3 * c:3 * c + 3]

        @pl.when((t >= lo) & (t < lo + n))
        def _(y0_ref=y0_ref, y1_ref=y1_ref, w_ref=w_ref):
            y = w_ref[:, 0:1] * y0_ref[...].astype(F32) + w_ref[:, 1:2] * y1_ref[...].astype(F32)
            x = x_ref[...] + mod_ref[0, gate_idx:gate_idx + 1, :] * y
            if final:
                ms = jnp.mean(x * x, axis=-1, keepdims=True)
                x = x * lax.rsqrt(ms + EPS) * fg_ref[...]
            o_ref[...] = x


def moe_combine(x, row_off, n_rows, parts, mod, gate_idx, final_g=None, tm=512):
    D = x.shape[1]
    xb = row_off // tm
    per_seg = SEG_ROWS // tm
    final = final_g is not None
    in_specs = [pl.BlockSpec((tm, D), lambda i: (i + xb, 0)),
                pl.BlockSpec((1, 6, D), lambda i: ((i + xb) // per_seg, 0, 0))]
    args = [x, mod]
    if final:
        in_specs.append(pl.BlockSpec((1, D), lambda i: (0, 0)))
        args.append(final_g.reshape(1, D))
    chunk_tiles = []
    for c_off, c_n, y0, y1, wts in parts:
        lo, n = c_off // tm, c_n // tm
        chunk_tiles.append((lo, n))

        def chunk_map(i, lo=lo, n=n):
            return (jnp.clip(i + xb - lo, 0, n - 1), 0)

        in_specs += [pl.BlockSpec((tm, D), chunk_map), pl.BlockSpec((tm, D), chunk_map),
                     pl.BlockSpec((tm, 2), chunk_map)]
        args += [y0, y1, wts]
    return pl.pallas_call(
        functools.partial(_combine_kernel, gate_idx=gate_idx, final=final, tile_off=xb,
                          chunk_tiles=tuple(chunk_tiles)),
        out_shape=jax.ShapeDtypeStruct((n_rows, D), F32),
        grid=(n_rows // tm,),
        in_specs=in_specs,
        out_specs=pl.BlockSpec((tm, D), lambda i: (i, 0)),
        compiler_params=_cparams(("arbitrary",), 48),
        name="moe_combine",
    )(*args)


def moe_dispatch_plan(ids, n_tokens):
    n_assign = 2 * n_tokens
    n_blocks = n_assign // MOE_ROWS + N_EXPERTS
    experts = jnp.arange(N_EXPERTS, dtype=I32)
    flat_e = ids.T.reshape(n_assign)
    onehot = (flat_e[:, None] == experts[None, :]).astype(I32)
    csum = jnp.cumsum(onehot, axis=0)
    rank = jnp.sum((csum - onehot) * onehot, axis=1)
    counts = csum[-1]
    padded = (counts + MOE_ROWS - 1) // MOE_ROWS * MOE_ROWS
    pend = jnp.cumsum(padded)
    dest = (pend - padded)[flat_e] + rank
    n_used = (pend[-1] // MOE_ROWS).astype(I32)
    block_idx = jnp.arange(n_blocks, dtype=I32)
    block_e = jnp.minimum(jnp.sum((pend[None, :] <= block_idx[:, None] * MOE_ROWS).astype(I32), axis=1),
                          N_EXPERTS - 1)
    prev_e = jnp.concatenate([jnp.full((1,), -1, I32), block_e[:-1]])
    first = ((block_idx < n_used) & (block_e != prev_e)).astype(I32)
    later = (experts[None, :] > experts[:, None]) & (counts[None, :] > 0)
    next_of = jnp.min(jnp.where(later, experts[None, :], N_EXPERTS), axis=1)
    next_of = jnp.where(next_of == N_EXPERTS, -1, next_of)
    next_e = jnp.sum(jnp.where(block_e[:, None] == experts[None, :], next_of[None, :], 0), axis=1).astype(I32)
    src_tok = jnp.zeros((n_blocks * MOE_ROWS,), I32).at[dest].set(jnp.arange(n_assign, dtype=I32) // 2)
    plan = {"block_e": block_e, "first": first, "next_e": next_e, "n_used": n_used.reshape(1)}
    return dest.reshape(n_tokens, 2), src_tok, plan


def channel_mixer(x, g, mod, w_router_t, b_router, wg, wu, wd, e_off, chunks, outputs, final_g=None):
    parts = []
    for c_off, c_n in chunks:
        h, ids8, wts8 = moe_router(x, c_off, c_n, g, mod, w_router_t, b_router, shift_idx=3, scale_idx=4)
        dest, src_tok, plan = moe_dispatch_plan(ids8[:2], c_n)
        xg = h.at[src_tok].get(mode="promise_in_bounds")
        ybuf = moe_experts(xg, plan, wg, wu, wd, e_off)
        y0 = ybuf.at[dest[:, 0]].get(mode="promise_in_bounds")
        y1 = ybuf.at[dest[:, 1]].get(mode="promise_in_bounds")
        parts.append((c_off, c_n, y0, y1, wts8[:2].T))
    return [moe_combine(x, o_off, o_n, parts, mod, 5, final_g) for o_off, o_n in outputs]


def kernel(x_prompt, x_sample, c, cache_a_k, cache_a_v, cache_b_k, cache_b_v, state_C, state_n, state_m,
           c_ctx, norm1_g, norm2_g, w_ada, b_ada, w_in_ab, w_out_ab, rel_bias_a, sink_b, w_in_c, b_gates_c,
           norm_c_g, w_out_c, w_router, b_router, w_gate_e, w_up_e, w_down_e, final_norm_g):
    D = D_MODEL
    Np, Nl = BATCH * SEQ, DEC_BATCH * DEC_SEQ
    N = Np + Nl
    H = C_HEADS
    x = None
    x_ctx, x_lat = x_prompt.reshape(Np, D), x_sample.reshape(Nl, D)
    moe_chunks = [(0, N)]
    n_e = w_gate_e.shape[1]
    wg_all = w_gate_e.reshape(DEPTH * n_e, D, D_FF_EXPERT)
    wu_all = w_up_e.reshape(DEPTH * n_e, D, D_FF_EXPERT)
    wd_all = w_down_e.reshape(DEPTH * n_e, D_FF_EXPERT, D)

    cvec = jnp.concatenate([c_ctx[None, :], c, jnp.zeros((8 - 1 - DEC_BATCH, D), F32)], axis=0)
    mod_all = ada_mod_all(cvec, w_ada, b_ada).reshape(DEPTH, 8, 6, D)
    w_router_t = router_weight_pair(w_router)

    outs = {}
    for l in range(DEPTH):
        mod = mod_all[l]
        j = l // 2
        if l % 2 == 0:
            w_in = w_in_ab[j]
            if x is not None:
                x_ctx, x_lat = x[:Np], x[Np:]
            proj_ctx = norm_mod_matmul(x_ctx, 0, Np, norm1_g[l], mod, w_in, F32, 0, 1, seg_row_off=0,
                                       precise=True, tm=512)
            proj_lat = norm_mod_matmul(x_lat, 0, Nl, norm1_g[l], mod, w_in, BF16, 0, 1, seg_row_off=Np)
            o_ctx = ctx_attention(proj_ctx, sink_b[j], BATCH, SEQ)
            bias_mask = nat_bias_mask(rel_bias_a[j], DEC_SEQ // GRID_W)
            oa = nat_attention(proj_lat, cache_a_k[:, j].reshape(DEC_BATCH, PAST_LEN, A_WIDTH).astype(BF16),
                               cache_a_v[:, j].reshape(DEC_BATCH, PAST_LEN, A_WIDTH).astype(BF16), bias_mask,
                               DEC_BATCH, DEC_SEQ, A_HEADS)
            cos, sin = rope_tables(DEC_SEQ)
            qk_rot = rope_heads(proj_lat, 3 * A_WIDTH, 3 * A_WIDTH + B_Q_WIDTH, cos, sin, DEC_SEQ)
            ob = swa_attention(qk_rot, proj_lat, 3 * A_HEADS + B_HEADS + B_KV_HEADS,
                               cache_b_k[:, j].reshape(DEC_BATCH, PAST_LEN, B_KV_WIDTH),
                               cache_b_v[:, j].reshape(DEC_BATCH, PAST_LEN, B_KV_WIDTH), sink_b[j],
                               DEC_BATCH, DEC_SEQ)
            x = attn_out_residual(o_ctx, oa, ob, jnp.stack(_hi_lo(w_out_ab[j]), axis=0), x_ctx, x_lat, mod,
                                  gate_idx=2)
            kv = proj_ctx[:, A_WIDTH:]
            outs.setdefault("ak", []).append(kv[:, :A_WIDTH].reshape(BATCH, SEQ, A_HEADS, HEAD_DIM))
            outs.setdefault("av", []).append(kv[:, A_WIDTH:2 * A_WIDTH].reshape(BATCH, SEQ, A_HEADS, HEAD_DIM))
            kvb = proj_ctx[:, 3 * A_WIDTH + B_Q_WIDTH:]
            outs.setdefault("bk", []).append(kvb[:, :B_KV_WIDTH].reshape(BATCH, SEQ, B_KV_HEADS, HEAD_DIM))
            outs.setdefault("bv", []).append(kvb[:, B_KV_WIDTH:].reshape(BATCH, SEQ, B_KV_HEADS, HEAD_DIM))
        else:
            n_main = 2 * C_QK_WIDTH + 2 * C_V_WIDTH
            w_main = w_in_c[j]
            w_gates = jnp.pad(w_in_c[j][:, n_main:], ((0, 0), (0, LANES - 4 * H)))
            proj, gates = norm_mod_matmul(x, 0, N, norm1_g[l], mod, w_main, BF16, 0, 1, w_aux=w_gates,
                                          n_out=n_main)
            gates_dir = jnp.pad(gates[:, :4 * H].reshape(N, 2, 2, H).transpose(1, 2, 0, 3),
                                ((0, 0), (0, 0), (0, 0), (0, LANES - H)))
            bg = jnp.pad(b_gates_c[j].astype(F32).reshape(2, 2, 1, H), ((0, 0), (0, 0), (0, 0), (0, LANES - H)))
            S = BATCH + DEC_BATCH
            C0 = jnp.concatenate([jnp.zeros((BATCH, 2, H, C_QK_DIM, C_V_DIM), F32), state_C[:, j]], axis=0)
            n0 = jnp.concatenate([jnp.zeros((BATCH, 2, H, C_QK_DIM), F32), state_n[:, j]], axis=0)
            m0 = jnp.concatenate([jnp.zeros((BATCH, 2, H), F32), state_m[:, j]], axis=0)
            S0 = jnp.concatenate([C0, jnp.broadcast_to(n0[..., None], n0.shape + (LANES,))], axis=-1)
            m0 = jnp.broadcast_to(m0[..., None], (S, 2, H, LANES))
            seq_chunks = [SEQ // C_CHUNK] * BATCH + [DEC_SEQ // C_CHUNK] * DEC_BATCH
            h_dir, Sf, mf = mlstm_bidir(proj, gates_dir, bg, S0, m0, seq_chunks)
            x = mlstm_out_residual(h_dir, proj, 2, norm_c_g[j], w_out_c[j].astype(BF16), x, mod, gate_idx=2)
            outs.setdefault("C", []).append(Sf[:BATCH, ..., :C_V_DIM])
            outs.setdefault("n", []).append(Sf[:BATCH, ..., C_V_DIM])
            outs.setdefault("m", []).append(mf[:BATCH, :, :, 0])
        last = l == DEPTH - 1
        pieces = channel_mixer(x, norm2_g[l], mod, w_router_t, b_router, wg_all, wu_all, wd_all, l * n_e,
                               moe_chunks, [(0, Np), (Np, Nl)] if last else [(0, N)],
                               final_norm_g if last else None)
        if last:
            y_prompt = pieces[0].reshape(BATCH, SEQ, D)
            y_sample = pieces[1].reshape(DEC_BATCH, DEC_SEQ, D)
        else:
            x = pieces[0]

    return (y_prompt, y_sample,
            jnp.stack(outs["ak"], axis=1), jnp.stack(outs["av"], axis=1),
            jnp.stack(outs["bk"], axis=1), jnp.stack(outs["bv"], axis=1),
            jnp.stack(outs["C"], axis=1), jnp.stack(outs["n"], axis=1), jnp.stack(outs["m"], axis=1))
```

```python
import functools

import numpy as np
import jax
import jax.numpy as jnp
from jax import lax
from jax.experimental import pallas as pl
from jax.experimental.pallas import tpu as pltpu

F32 = jnp.float32
BF16 = jnp.bfloat16
I32 = jnp.int32

D_MODEL = 2048
BATCH = 16
SEQ = 256
DEPTH = 2
DEC_BATCH = 4
DEC_SEQ = 4096
PAST_LEN = 512
GRID_W = 64
HEAD_DIM = 128
A_HEADS = 8
NA_ROWS = 8
NA_COLS = 16
B_HEADS = 8
B_KV_HEADS = 2
B_WINDOW = 128
ROPE_THETA = 10000.0
C_HEADS = 8
C_QK_DIM = 128
C_V_DIM = 256
C_CHUNK = 128
N_EXPERTS = 16
N_GROUPS = 4
EXPERTS_PER_GROUP = N_EXPERTS // N_GROUPS
D_FF_EXPERT = 1024
EPS = 1e-6

A_WIDTH = A_HEADS * HEAD_DIM
B_Q_WIDTH = B_HEADS * HEAD_DIM
B_KV_WIDTH = B_KV_HEADS * HEAD_DIM
C_QK_WIDTH = C_HEADS * C_QK_DIM
C_V_WIDTH = C_HEADS * C_V_DIM

SEG_ROWS = 4096
NEG_BIG = -1e30
MOE_ROWS = 256
NORM_ROWS = 256
MIB = 1024 * 1024
LANES = 128

_NT = (((1,), (1,)), ((), ()))


def _cparams(sem, vmem_mib):
    return pltpu.CompilerParams(dimension_semantics=sem, vmem_limit_bytes=vmem_mib * MIB)


def _modulated_norm(x, g, mod, shift_idx, scale_idx):
    ms = jnp.mean(x * x, axis=-1, keepdims=True)
    y = x * lax.rsqrt(ms + EPS) * g
    return y * (1.0 + mod[scale_idx:scale_idx + 1, :]) + mod[shift_idx:shift_idx + 1, :]


def _hi_lo(x):
    hi = x.astype(BF16)
    return hi, (x - hi.astype(F32)).astype(BF16)


def _dot3(a, b, dims=None):
    a_hi, a_lo = _hi_lo(a)
    b_hi, b_lo = _hi_lo(b)
    dims = (((a.ndim - 1,), (0,)), ((), ())) if dims is None else dims
    dot = functools.partial(lax.dot_general, dimension_numbers=dims, preferred_element_type=F32)
    return dot(a_hi, b_hi) + dot(a_lo, b_hi) + dot(a_hi, b_lo)


def _ada_kernel(c_ref, w_ref, b_ref, o_ref):
    c = c_ref[...]
    o_ref[0] = _dot3(c * jax.nn.sigmoid(c), w_ref[0]) + b_ref[0]


def ada_mod_all(cvec8, w_ada, b_ada, tn=1024):
    L, D, D6 = w_ada.shape
    return pl.pallas_call(
        _ada_kernel,
        out_shape=jax.ShapeDtypeStruct((L, 8, D6), F32),
        grid=(L, D6 // tn),
        in_specs=[pl.BlockSpec((8, D), lambda l, j: (0, 0)),
                  pl.BlockSpec((1, D, tn), lambda l, j: (l, 0, j)),
                  pl.BlockSpec((1, 1, tn), lambda l, j: (l, 0, j))],
        out_specs=pl.BlockSpec((1, 8, tn), lambda l, j: (l, 0, j)),
        compiler_params=_cparams(("parallel", "parallel"), 40),
        name="ada_mod",
    )(cvec8, w_ada, b_ada.reshape(L, 1, D6))


def _nmm_kernel(x_ref, g_ref, mod_ref, w_ref, *rest, shift_idx, scale_idx, has_aux, precise):
    rest = list(rest)
    waux_ref = rest.pop(0) if has_aux else None
    o_ref = rest.pop(0)
    oaux_ref = rest.pop(0) if has_aux else None
    h_scr = rest.pop(0)
    hlo_scr = rest.pop(0) if precise else None

    @pl.when(pl.program_id(1) == 0)
    def _():
        for r in range(x_ref.shape[0] // NORM_ROWS):
            rows = pl.ds(r * NORM_ROWS, NORM_ROWS)
            hf = _modulated_norm(x_ref[rows, :], g_ref[...], mod_ref[0], shift_idx, scale_idx)
            h = hf.astype(BF16)
            h_scr[rows, :] = h
            if precise:
                hlo_scr[rows, :] = (hf - h.astype(F32)).astype(BF16)
            if has_aux:
                oaux_ref[rows, :] = jnp.dot(h, waux_ref[...].astype(BF16), preferred_element_type=F32)

    if precise:
        w_hi, w_lo = _hi_lo(w_ref[...])
        acc = (jnp.dot(h_scr[...], w_hi, preferred_element_type=F32)
               + jnp.dot(hlo_scr[...], w_hi, preferred_element_type=F32)
               + jnp.dot(h_scr[...], w_lo, preferred_element_type=F32))
    else:
        acc = jnp.dot(h_scr[...], w_ref[...].astype(BF16), preferred_element_type=F32)
    o_ref[...] = acc.astype(o_ref.dtype)


def norm_mod_matmul(x, row_off, n_rows, g, mod, w, out_dtype, shift_idx, scale_idx, w_aux=None, n_out=None,
                    seg_row_off=None, precise=False, tm=1024, tn=512):
    D = x.shape[1]
    n_out = w.shape[1] if n_out is None else n_out
    off_b = row_off // tm
    seg_b = off_b if seg_row_off is None else seg_row_off // tm
    per_seg = SEG_ROWS // tm
    has_aux = w_aux is not None
    in_specs = [pl.BlockSpec((tm, D), lambda i, j: (i + off_b, 0)),
                pl.BlockSpec((1, D), lambda i, j: (0, 0)),
                pl.BlockSpec((1, 6, D), lambda i, j: ((i + seg_b) // per_seg, 0, 0)),
                pl.BlockSpec((D, tn), lambda i, j: (0, j))]
    out_shape = jax.ShapeDtypeStruct((n_rows, n_out), out_dtype)
    out_specs = pl.BlockSpec((tm, tn), lambda i, j: (i, j))
    args = [x, g.reshape(1, D), mod, w]
    if has_aux:
        n_aux = w_aux.shape[1]
        in_specs.append(pl.BlockSpec((D, n_aux), lambda i, j: (0, 0)))
        out_shape = (out_shape, jax.ShapeDtypeStruct((n_rows, n_aux), F32))
        out_specs = (out_specs, pl.BlockSpec((tm, n_aux), lambda i, j: (i, 0)))
        args.append(w_aux)
    return pl.pallas_call(
        functools.partial(_nmm_kernel, shift_idx=shift_idx, scale_idx=scale_idx, has_aux=has_aux,
                          precise=precise),
        out_shape=out_shape,
        grid=(n_rows // tm, n_out // tn),
        in_specs=in_specs,
        out_specs=out_specs,
        scratch_shapes=[pltpu.VMEM((tm, D), BF16)] * (2 if precise else 1),
        compiler_params=_cparams(("parallel", "arbitrary"), 56),
        name="norm_mod_matmul",
    )(*args)


def _attn_out_kernel(oc_ref, oa_ref, ob_ref, w_ref, xc_ref, xl_ref, mod_ref, o_ref, *, gate_idx, n_ctx_tiles):
    i = pl.program_id(0)
    gate = mod_ref[0, gate_idx:gate_idx + 1, :]

    @pl.when(i < n_ctx_tiles)
    def _():
        oc_hi, oc_lo = _hi_lo(oc_ref[...])
        acc = (jnp.dot(oc_hi, w_ref[0], preferred_element_type=F32)
               + jnp.dot(oc_lo, w_ref[0], preferred_element_type=F32)
               + jnp.dot(oc_hi, w_ref[1], preferred_element_type=F32))
        o_ref[...] = xc_ref[...] + gate * acc

    @pl.when(i >= n_ctx_tiles)
    def _():
        a = jnp.concatenate([oa_ref[...], ob_ref[...]], axis=-1)
        acc = jnp.dot(a, w_ref[0], preferred_element_type=F32)
        o_ref[...] = xl_ref[...] + gate * acc


def attn_out_residual(o_ctx, oa, ob, w, x_ctx, x_lat, mod, gate_idx, tm=256):
    n_ctx, K = o_ctx.shape
    n_lat = oa.shape[0]
    D = w.shape[2]
    nct = n_ctx // tm
    per_seg = SEG_ROWS // tm

    def ctx_map(i):
        return (jnp.minimum(i, nct - 1), 0)

    def lat_map(i):
        return (jnp.maximum(i - nct, 0), 0)

    return pl.pallas_call(
        functools.partial(_attn_out_kernel, gate_idx=gate_idx, n_ctx_tiles=nct),
        out_shape=jax.ShapeDtypeStruct((n_ctx + n_lat, D), F32),
        grid=((n_ctx + n_lat) // tm,),
        in_specs=[pl.BlockSpec((tm, K), ctx_map),
                  pl.BlockSpec((tm, oa.shape[1]), lat_map),
                  pl.BlockSpec((tm, ob.shape[1]), lat_map),
                  pl.BlockSpec((2, K, D), lambda i: (0, 0, 0), pipeline_mode=pl.Buffered(1)),
                  pl.BlockSpec((tm, D), ctx_map),
                  pl.BlockSpec((tm, D), lat_map),
                  pl.BlockSpec((1, 6, D), lambda i: (i // per_seg, 0, 0))],
        out_specs=pl.BlockSpec((tm, D), lambda i: (i, 0)),
        compiler_params=_cparams(("arbitrary",), 48),
        name="attn_out_residual",
    )(o_ctx, oa, ob, w, x_ctx, x_lat, mod)


def _mlstm_out_kernel(hf_ref, hb_ref, o_ref, ng_ref, w_ref, x_ref, mod_ref, out_ref, *, gate_idx):
    hs = hf_ref[...] + hb_ref[...]
    parts = []
    for h in range(C_HEADS):
        sl = slice(h * C_V_DIM, (h + 1) * C_V_DIM)
        xs = hs[:, sl]
        ms = jnp.mean(xs * xs, axis=-1, keepdims=True)
        hn = xs * lax.rsqrt(ms + EPS) * ng_ref[:, sl]
        parts.append((jax.nn.sigmoid(o_ref[:, sl].astype(F32)) * hn).astype(BF16))
    a = jnp.concatenate(parts, axis=-1)
    acc = jnp.dot(a, w_ref[...], preferred_element_type=F32)
    out_ref[...] = x_ref[...] + mod_ref[0, gate_idx:gate_idx + 1, :] * acc


def mlstm_out_residual(h_dir, proj, o_col_block, norm_g, w, x, mod, gate_idx, tm=256):
    n, D = x.shape
    V = C_V_WIDTH
    per_seg = SEG_ROWS // tm
    return pl.pallas_call(
        functools.partial(_mlstm_out_kernel, gate_idx=gate_idx),
        out_shape=jax.ShapeDtypeStruct((n, D), F32),
        grid=(n // tm,),
        in_specs=[pl.BlockSpec((None, tm, V), lambda i: (0, i, 0)),
                  pl.BlockSpec((None, tm, V), lambda i: (1, i, 0)),
                  pl.BlockSpec((tm, V), lambda i: (i, o_col_block)),
                  pl.BlockSpec((1, V), lambda i: (0, 0)),
                  pl.BlockSpec((V, D), lambda i: (0, 0)),
                  pl.BlockSpec((tm, D), lambda i: (i, 0)),
                  pl.BlockSpec((1, 6, D), lambda i: (i // per_seg, 0, 0))],
        out_specs=pl.BlockSpec((tm, D), lambda i: (i, 0)),
        compiler_params=_cparams(("parallel",), 48),
        name="mlstm_out_residual",
    )(h_dir, h_dir, proj, norm_g.reshape(1, V), w, x, mod)


def _ctx_attn_kernel(sink_ref, q_ref, k_ref, v_ref, o_ref):
    h = pl.program_id(1)
    s = _dot3(q_ref[...], k_ref[...], _NT) * (HEAD_DIM ** -0.5)
    sk = sink_ref[h]
    m = jnp.maximum(jnp.max(s, axis=-1, keepdims=True), sk)
    p = jnp.exp(s - m)
    l = jnp.sum(p, axis=-1, keepdims=True) + jnp.exp(sk - m)
    o_ref[...] = (_dot3(p, v_ref[...]) / l).astype(o_ref.dtype)


def ctx_attention(proj, sink_b, n_batch, seq):
    n_heads = A_HEADS + B_HEADS
    group = B_HEADS // B_KV_HEADS
    qb0 = 3 * A_HEADS
    kb0 = qb0 + B_HEADS
    vb0 = kb0 + B_KV_HEADS
    sinks = jnp.concatenate([jnp.full((A_HEADS,), NEG_BIG, F32), sink_b.astype(F32)])

    def q_map(b, h, s):
        return (b, jnp.where(h < A_HEADS, h, qb0 + h - A_HEADS))

    def k_map(b, h, s):
        return (b, jnp.where(h < A_HEADS, A_HEADS + h, kb0 + (h - A_HEADS) // group))

    def v_map(b, h, s):
        return (b, jnp.where(h < A_HEADS, 2 * A_HEADS + h, vb0 + (h - A_HEADS) // group))

    blk = (seq, HEAD_DIM)
    return pl.pallas_call(
        _ctx_attn_kernel,
        out_shape=jax.ShapeDtypeStruct((n_batch * seq, n_heads * HEAD_DIM), F32),
        grid_spec=pltpu.PrefetchScalarGridSpec(
            num_scalar_prefetch=1,
            grid=(n_batch, n_heads),
            in_specs=[pl.BlockSpec(blk, q_map), pl.BlockSpec(blk, k_map), pl.BlockSpec(blk, v_map)],
            out_specs=pl.BlockSpec(blk, lambda b, h, s: (b, h))),
        compiler_params=_cparams(("parallel", "parallel"), 32),
        name="ctx_attention",
    )(sinks, proj, proj, proj)


NAT_QROWS = 4


def nat_bias_mask(rel_bias, rows):
    W = GRID_W
    nb = rows // NAT_QROWS
    kh = min(NA_ROWS, rows)
    n_dr, n_dc = 2 * NA_ROWS - 1, 2 * NA_COLS - 1
    H = rel_bias.shape[0]
    cidx = np.clip(np.arange(W)[None, :] - np.arange(W)[:, None] + NA_COLS - 1, 0, n_dc - 1)
    onehot = jnp.asarray((cidx.reshape(1, W * W) == np.arange(n_dc)[:, None]).astype(np.float32))
    col_bias = jnp.dot(rel_bias.astype(F32).reshape(H * n_dr, n_dc), onehot,
                       precision=lax.Precision.HIGHEST).reshape(H, n_dr, W, W)
    q_rows = []
    for qi in range(NAT_QROWS):
        tiles = [col_bias[:, int(np.clip(NAT_QROWS * (kj - 1) + kjr - qi + NA_ROWS - 1, 0, n_dr - 1))]
                 for kj in range(3) for kjr in range(NAT_QROWS)]
        q_rows.append(jnp.concatenate(tiles, axis=-1))
    bias = jnp.concatenate(q_rows, axis=1)
    variants = []
    for g in (0, 1, nb - 1):
        i = np.arange(NAT_QROWS)[:, None, None, None, None]
        qc = np.arange(W)[None, :, None, None, None]
        j = np.arange(3)[None, None, :, None, None]
        jr = np.arange(NAT_QROWS)[None, None, None, :, None]
        kc = np.arange(W)[None, None, None, None, :]
        r = NAT_QROWS * g + i
        kblk = g - 1 + j
        kr = NAT_QROWS * kblk + jr
        rs = np.clip(r - kh // 2, 0, rows - kh)
        row_ok = (kblk >= 0) & (kblk < nb) & (kr >= rs) & (kr < rs + kh)
        cstart = np.clip(qc - NA_COLS // 2, 0, W - NA_COLS)
        col_ok = (kc >= cstart) & (kc < cstart + NA_COLS)
        ok = np.broadcast_to(row_ok & col_ok, (NAT_QROWS, W, 3, NAT_QROWS, W))
        n_q, n_k = NAT_QROWS * W, 3 * NAT_QROWS * W
        variants.append(jnp.where(jnp.asarray(ok.reshape(1, n_q, n_k)), bias, NEG_BIG))
    return jnp.stack(variants, axis=0)


NAT_HEADS_PER_STEP = 4


def _nat_kernel(q_ref, k0_ref, k1_ref, k2_ref, v0_ref, v1_ref, v2_ref, kc_ref, vc_ref, bm_ref, o_ref):
    scale = HEAD_DIM ** -0.5
    tq = q_ref.shape[0]
    for h in range(q_ref.shape[1] // HEAD_DIM):
        cs = slice(h * HEAD_DIM, (h + 1) * HEAD_DIM)
        q = q_ref[:, cs]
        s_lat = [lax.dot_general(q, k_ref[:, cs], _NT, preferred_element_type=F32) * scale
                 + bm_ref[0, h, :, j * tq:(j + 1) * tq]
                 for j, k_ref in enumerate((k0_ref, k1_ref, k2_ref))]
        s_ctx = lax.dot_general(q, kc_ref[0, :, cs], _NT, preferred_element_type=F32) * scale
        m = jnp.max(s_ctx, axis=-1, keepdims=True)
        for s in s_lat:
            m = jnp.maximum(m, jnp.max(s, axis=-1, keepdims=True))
        p_ctx = jnp.exp(s_ctx - m)
        l = jnp.sum(p_ctx, axis=-1, keepdims=True)
        acc = jnp.dot(p_ctx.astype(BF16), vc_ref[0, :, cs], preferred_element_type=F32)
        for s, v_ref in zip(s_lat, (v0_ref, v1_ref, v2_ref)):
            p = jnp.exp(s - m)
            l = l + jnp.sum(p, axis=-1, keepdims=True)
            acc = acc + jnp.dot(p.astype(BF16), v_ref[:, cs], preferred_element_type=F32)
        o_ref[:, cs] = (acc / l).astype(o_ref.dtype)


def nat_attention(proj, cache_k, cache_v, bias_mask, n_batch, T, n_heads):
    tq = NAT_QROWS * GRID_W
    nb = T // tq
    P = cache_k.shape[1]
    hs = min(NAT_HEADS_PER_STEP, n_heads)
    ng = n_heads // hs

    def kv_map(col0, j):
        return lambda b, h, g: (b * nb + jnp.clip(g - 1 + j, 0, nb - 1), col0 + h)

    blk = (tq, hs * HEAD_DIM)
    in_specs = [pl.BlockSpec(blk, lambda b, h, g: (b * nb + g, h))]
    in_specs += [pl.BlockSpec(blk, kv_map(ng, j)) for j in range(3)]
    in_specs += [pl.BlockSpec(blk, kv_map(2 * ng, j)) for j in range(3)]
    in_specs += [pl.BlockSpec((1, P, hs * HEAD_DIM), lambda b, h, g: (b, 0, h))] * 2
    in_specs += [pl.BlockSpec((1, hs, tq, 3 * tq),
                              lambda b, h, g: (jnp.where(g == 0, 0, jnp.where(g == nb - 1, 2, 1)), h, 0, 0))]
    return pl.pallas_call(
        _nat_kernel,
        out_shape=jax.ShapeDtypeStruct((n_batch * T, n_heads * HEAD_DIM), BF16),
        grid=(n_batch, ng, nb),
        in_specs=in_specs,
        out_specs=pl.BlockSpec(blk, lambda b, h, g: (b * nb + g, h)),
        compiler_params=_cparams(("parallel", "parallel", "arbitrary"), 40),
        name="nat_attention",
    )(proj, proj, proj, proj, proj, proj, proj, cache_k, cache_v, bias_mask)


def rope_tables(T):
    t = jnp.arange(T)
    row = (t // GRID_W).astype(F32)
    col = (t % GRID_W).astype(F32)
    nf = HEAD_DIM // 4
    freqs = ROPE_THETA ** (-jnp.arange(nf, dtype=F32) / nf)
    ar = row[:, None] * freqs
    ac = col[:, None] * freqs
    cos = jnp.concatenate([jnp.cos(ar), jnp.cos(ar), jnp.cos(ac), jnp.cos(ac)], axis=-1)
    sin = jnp.concatenate([-jnp.sin(ar), jnp.sin(ar), -jnp.sin(ac), jnp.sin(ac)], axis=-1)
    return cos, sin


def _rope_kernel(q_ref, k_ref, cos_ref, sin_ref, o_ref):
    nf = HEAD_DIM // 4
    cos = cos_ref[...]
    sin = sin_ref[...]
    lane = lax.broadcasted_iota(I32, cos.shape, 1)
    first_half = (lane & nf) == 0
    col = 0
    for src in (q_ref, k_ref):
        for h in range(src.shape[1] // HEAD_DIM):
            x = src[:, h * HEAD_DIM:(h + 1) * HEAD_DIM].astype(F32)
            upper = pltpu.roll(x, HEAD_DIM - nf, 1)
            lower = pltpu.roll(x, nf, 1)
            partner = jnp.where(first_half, upper, lower)
            o_ref[:, col:col + HEAD_DIM] = (x * cos + partner * sin).astype(o_ref.dtype)
            col += HEAD_DIM


def rope_heads(proj, q_col0, k_col0, cos, sin, T, tm=512):
    n = proj.shape[0]
    per_seq = T // tm
    return pl.pallas_call(
        _rope_kernel,
        out_shape=jax.ShapeDtypeStruct((n, B_Q_WIDTH + B_KV_WIDTH), BF16),
        grid=(n // tm,),
        in_specs=[pl.BlockSpec((tm, B_Q_WIDTH), lambda i: (i, q_col0 // B_Q_WIDTH)),
                  pl.BlockSpec((tm, B_KV_WIDTH), lambda i: (i, k_col0 // B_KV_WIDTH)),
                  pl.BlockSpec((tm, HEAD_DIM), lambda i: (i % per_seq, 0)),
                  pl.BlockSpec((tm, HEAD_DIM), lambda i: (i % per_seq, 0))],
        out_specs=pl.BlockSpec((tm, B_Q_WIDTH + B_KV_WIDTH), lambda i: (i, 0)),
        compiler_params=_cparams(("parallel",), 32),
        name="rope_heads",
    )(proj, proj, cos, sin)


SWA_TQ = 2 * B_WINDOW


def _swa_kernel(sink_ref, q_ref, k0_ref, k1_ref, k2_ref, k3_ref, v0_ref, v1_ref, v2_ref, v3_ref,
                kc_ref, vc_ref, o_ref, *, T):
    scale = HEAD_DIM ** -0.5
    group = B_HEADS // B_KV_HEADS
    kvh = pl.program_id(1)
    n = pl.program_id(2)
    k = jnp.concatenate([k0_ref[...], k1_ref[...], k2_ref[...], k3_ref[...]], axis=0)
    v = jnp.concatenate([v0_ref[...], v1_ref[...], v2_ref[...], v3_ref[...]], axis=0)
    kc = kc_ref[0].astype(BF16)
    vc = vc_ref[0].astype(BF16)
    nk = k.shape[0]
    qpos = n * SWA_TQ + lax.broadcasted_iota(I32, (SWA_TQ, nk), 0)
    kpos = n * SWA_TQ - B_WINDOW + lax.broadcasted_iota(I32, (SWA_TQ, nk), 1)
    dist = jnp.abs(qpos - kpos)
    ok = jnp.where(kpos >= 0, jnp.where(kpos < T, dist, B_WINDOW + 1), B_WINDOW + 1) <= B_WINDOW
    outs = []
    for gi in range(group):
        q = q_ref[:, gi * HEAD_DIM:(gi + 1) * HEAD_DIM]
        s_lat = jnp.where(ok, lax.dot_general(q, k, _NT, preferred_element_type=F32) * scale, NEG_BIG)
        s_ctx = lax.dot_general(q, kc, _NT, preferred_element_type=F32) * scale
        sk = sink_ref[kvh * group + gi]
        m = jnp.maximum(jnp.maximum(jnp.max(s_lat, axis=-1, keepdims=True),
                                    jnp.max(s_ctx, axis=-1, keepdims=True)), sk)
        p_lat = jnp.exp(s_lat - m)
        p_ctx = jnp.exp(s_ctx - m)
        l = (jnp.sum(p_lat, axis=-1, keepdims=True) + jnp.sum(p_ctx, axis=-1, keepdims=True)
             + jnp.exp(sk - m))
        acc = (jnp.dot(p_lat.astype(BF16), v, preferred_element_type=F32)
               + jnp.dot(p_ctx.astype(BF16), vc, preferred_element_type=F32))
        outs.append((acc / l).astype(o_ref.dtype))
    o_ref[...] = jnp.concatenate(outs, axis=-1)


def swa_attention(qk_rot, proj, v_col0, cache_k, cache_v, sink, n_batch, T):
    group = B_HEADS // B_KV_HEADS
    nq = T // SWA_TQ
    nkb = T // B_WINDOW
    P = cache_k.shape[1]

    def kv_map(col0, j):
        return lambda b, kvh, n, s: (b * nkb + jnp.clip(2 * n - 1 + j, 0, nkb - 1), col0 + kvh)

    kblk = (B_WINDOW, HEAD_DIM)
    in_specs = [pl.BlockSpec((SWA_TQ, group * HEAD_DIM), lambda b, kvh, n, s: (b * nq + n, kvh))]
    in_specs += [pl.BlockSpec(kblk, kv_map(B_HEADS, j)) for j in range(4)]
    in_specs += [pl.BlockSpec(kblk, kv_map(v_col0, j)) for j in range(4)]
    in_specs += [pl.BlockSpec((1, P, HEAD_DIM), lambda b, kvh, n, s: (b, 0, kvh))] * 2
    return pl.pallas_call(
        functools.partial(_swa_kernel, T=T),
        out_shape=jax.ShapeDtypeStruct((n_batch * T, B_Q_WIDTH), BF16),
        grid_spec=pltpu.PrefetchScalarGridSpec(
            num_scalar_prefetch=1,
            grid=(n_batch, B_KV_HEADS, nq),
            in_specs=in_specs,
            out_specs=pl.BlockSpec((SWA_TQ, group * HEAD_DIM), lambda b, kvh, n, s: (b * nq + n, kvh))),
        compiler_params=_cparams(("parallel", "parallel", "arbitrary"), 32),
        name="swa_attention",
    )(sink.astype(F32), qk_rot, qk_rot, qk_rot, qk_rot, qk_rot, proj, proj, proj, proj, cache_k, cache_v)


def _mlstm_kernel(rowblk_ref, seq_ref, first_ref, last_ref, *refs):
    s = pl.program_id(1)
    for d in range(2):
        @pl.when(pl.program_id(0) == d)
        def _(d=d):
            _mlstm_chunk(d, first_ref[s] == 1, last_ref[s] == 1, *refs)


def _mlstm_chunk(d, is_first, is_last, q_ref, k_ref, v_ref, g_ref, bg_ref, S0_ref, m0_ref,
                 h_ref, Sf_ref, mf_ref, S_scr, m_scr):
    H, DK, DV, L, R = C_HEADS, C_QK_DIM, C_V_DIM, C_CHUNK, LANES
    scale = DK ** -0.5

    @pl.when(is_first)
    def _():
        S_scr[...] = S0_ref[0, 0]
        m_scr[...] = m0_ref[0, 0]

    gi = g_ref[0, 0] + bg_ref[0, 0]
    lf = jax.nn.log_sigmoid(g_ref[0, 1] + bg_ref[0, 1])
    row = lax.broadcasted_iota(I32, (L, L), 0)
    col = lax.broadcasted_iota(I32, (L, L), 1)
    causal = col <= row if d == 0 else col >= row
    b_all = jnp.dot(causal.astype(F32), lf, precision=lax.Precision.HIGHEST,
                    preferred_element_type=F32)
    a_row = (gi - b_all).T[0:H, :]
    b_row = b_all.T[0:H, :]
    lane = lax.broadcasted_iota(I32, (H, L), 1)
    cm = a_row
    k = 1
    while k < L:
        if d == 0:
            shifted = jnp.where(lane >= k, pltpu.roll(cm, k, 1), -jnp.inf)
        else:
            shifted = jnp.where(lane < L - k, pltpu.roll(cm, L - k, 1), -jnp.inf)
        cm = jnp.maximum(cm, shifted)
        k *= 2
    end = L - 1 if d == 0 else 0
    m_all = m_scr[...]
    M_row = jnp.maximum(m_all, cm)
    wi_row = jnp.exp(m_all - M_row)
    emt_row = jnp.exp(-(b_row + M_row))
    M_last = M_row[:, end:end + 1]
    b_last = b_row[:, end:end + 1]
    ws_row = jnp.exp(a_row - M_last)
    wc_all = jnp.exp(m_all - M_last)
    m_scr[...] = jnp.broadcast_to(b_last + M_last, m_all.shape)
    cols = jnp.concatenate([M_row, wi_row, emt_row, jnp.zeros((L - 3 * H, L), F32)], axis=0).T
    ones = jnp.ones((L, R), BF16)
    for h in range(H):
        M_col = cols[:, h:h + 1]
        wi_col = cols[:, H + h:H + h + 1]
        emt_col = cols[:, 2 * H + h:2 * H + h + 1]
        qh = q_ref[:, h * DK:(h + 1) * DK]
        kh = k_ref[:, h * DK:(h + 1) * DK]
        vh = v_ref[:, h * DV:(h + 1) * DV]
        v_ext = jnp.concatenate([vh, ones], axis=-1)
        Sh = S_scr[h]

        w = jnp.where(causal, jnp.exp(a_row[h:h + 1, :] - M_col), 0.0)
        sqk = lax.dot_general(qh, kh, _NT, preferred_element_type=F32) * scale * w
        S_hi = Sh.astype(BF16)
        n_lo = (Sh[:, DV:] - S_hi[:, DV:].astype(F32)).astype(BF16)
        inter = jnp.dot(qh, jnp.concatenate([S_hi, n_lo], axis=-1),
                        preferred_element_type=F32) * scale
        qn = inter[:, DV:DV + 1] + inter[:, DV + R:DV + R + 1]
        num = jnp.dot(sqk.astype(BF16), vh, preferred_element_type=F32) + wi_col * inter[:, :DV]
        den = jnp.sum(sqk, axis=-1, keepdims=True) + wi_col * qn
        h_ref[:, h * DV:(h + 1) * DV] = num / jnp.maximum(jnp.abs(den), emt_col)

        kwT = kh.astype(F32).T * ws_row[h:h + 1, :]
        kwT_hi = kwT.astype(BF16)
        kwT_lo = (kwT - kwT_hi.astype(F32)).astype(BF16)
        upd = jnp.dot(kwT_hi, v_ext, preferred_element_type=F32)
        upd_n = upd[:, DV:] + jnp.dot(kwT_lo, ones, preferred_element_type=F32)
        wc = jnp.concatenate([wc_all[h:h + 1, :]] * (DV // R + 1), axis=-1)
        S_scr[h] = wc * Sh + jnp.concatenate([upd[:, :DV], upd_n], axis=-1)

    @pl.when(is_last)
    def _():
        Sf_ref[0, 0] = S_scr[...]
        mf_ref[0, 0] = m_scr[...]


def mlstm_bidir(proj, gates_dir, b_gates_dir, S0, m0, seq_chunks):
    H, DK, DV, L = C_HEADS, C_QK_DIM, C_V_DIM, C_CHUNK
    DS = DV + LANES
    N = proj.shape[0]
    S = len(seq_chunks)
    rowblk, seq_id, first, last = [[], []], [], [], []
    base = 0
    for sq, nc in enumerate(seq_chunks):
        rowblk[0] += [base + c for c in range(nc)]
        rowblk[1] += [base + nc - 1 - c for c in range(nc)]
        seq_id += [sq] * nc
        first += [1] + [0] * (nc - 1)
        last += [0] * (nc - 1) + [1]
        base += nc
    n_steps = base
    rowblk = jnp.asarray(np.array(rowblk, np.int32).reshape(-1))
    tables = (rowblk, jnp.asarray(seq_id, I32), jnp.asarray(first, I32), jnp.asarray(last, I32))

    def row_map(colblk):
        return lambda d, s, rb, sq, fi, la: (rb[d * n_steps + s], colblk)

    def state_map(nd):
        return lambda d, s, rb, sq, fi, la: (sq[s], d) + (0,) * nd

    in_specs = [pl.BlockSpec((L, H * DK), row_map(0)),
                pl.BlockSpec((L, H * DK), row_map(1)),
                pl.BlockSpec((L, H * DV), row_map(1)),
                pl.BlockSpec((1, 2, L, LANES), lambda d, s, rb, sq, fi, la: (d, 0, rb[d * n_steps + s], 0)),
                pl.BlockSpec((1, 2, 1, LANES), lambda d, s, rb, sq, fi, la: (d, 0, 0, 0)),
                pl.BlockSpec((1, 1, H, DK, DS), state_map(3)),
                pl.BlockSpec((1, 1, H, LANES), state_map(2))]
    out_specs = (pl.BlockSpec((None, L, H * DV), lambda d, s, rb, sq, fi, la: (d, rb[d * n_steps + s], 0)),
                 pl.BlockSpec((1, 1, H, DK, DS), state_map(3)),
                 pl.BlockSpec((1, 1, H, LANES), state_map(2)))
    out_shape = (jax.ShapeDtypeStruct((2, N, H * DV), F32),
                 jax.ShapeDtypeStruct((S, 2, H, DK, DS), F32),
                 jax.ShapeDtypeStruct((S, 2, H, LANES), F32))
    return pl.pallas_call(
        _mlstm_kernel,
        out_shape=out_shape,
        grid_spec=pltpu.PrefetchScalarGridSpec(
            num_scalar_prefetch=4,
            grid=(2, n_steps),
            in_specs=in_specs,
            out_specs=out_specs,
            scratch_shapes=[pltpu.VMEM((H, DK, DS), F32), pltpu.VMEM((H, LANES), F32)]),
        compiler_params=_cparams(("arbitrary", "arbitrary"), 40),
        name="mlstm_bidir",
    )(*tables, proj, proj, proj, gates_dir, b_gates_dir, S0, m0)


def _top2_of4(vals):
    m1, i1 = vals[0], jnp.zeros(vals[0].shape, I32)
    for j in range(1, 4):
        better = vals[j] > m1
        m1 = jnp.where(better, vals[j], m1)
        i1 = jnp.where(better, j, i1)
    m2, i2 = jnp.full(vals[0].shape, -jnp.inf, F32), jnp.zeros(vals[0].shape, I32)
    for j in range(4):
        cand = jnp.where(i1 == j, -jnp.inf, vals[j])
        better = cand > m2
        m2 = jnp.where(better, cand, m2)
        i2 = jnp.where(better, j, i2)
    return m1, i1, m2, i2


def _router_kernel(x_ref, g_ref, mod_ref, wr_ref, br_ref, h_ref, ids_ref, wts_ref, *, shift_idx, scale_idx):
    h = _modulated_norm(x_ref[...], g_ref[...], mod_ref[0], shift_idx, scale_idx)
    h_hi = h.astype(BF16)
    h_ref[...] = h_hi
    h_lo = (h - h_hi.astype(F32)).astype(BF16)
    w_hi = wr_ref[0]
    w_lo = wr_ref[1]
    logits = (lax.dot_general(w_hi, h_hi, _NT, preferred_element_type=F32)
              + lax.dot_general(w_lo, h_hi, _NT, preferred_element_type=F32)
              + lax.dot_general(w_hi, h_lo, _NT, preferred_element_type=F32))
    aff = jax.nn.sigmoid(logits)
    sel = aff + br_ref[...]
    aff_rows = [aff[e:e + 1, :] for e in range(N_EXPERTS)]
    sel_rows = [sel[e:e + 1, :] for e in range(N_EXPERTS)]
    tops = [_top2_of4(sel_rows[4 * gidx:4 * gidx + 4]) for gidx in range(N_GROUPS)]
    best = tops[0][0] + tops[0][2]
    grp = jnp.zeros(best.shape, I32)
    i1, i2 = tops[0][1], tops[0][3]
    for gidx in range(1, N_GROUPS):
        score = tops[gidx][0] + tops[gidx][2]
        better = score > best
        best = jnp.where(better, score, best)
        grp = jnp.where(better, gidx, grp)
        i1 = jnp.where(better, tops[gidx][1], i1)
        i2 = jnp.where(better, tops[gidx][3], i2)
    e1 = grp * EXPERTS_PER_GROUP + i1
    e2 = grp * EXPERTS_PER_GROUP + i2
    w1 = jnp.zeros(best.shape, F32)
    w2 = jnp.zeros(best.shape, F32)
    for e in range(N_EXPERTS):
        w1 = jnp.where(e1 == e, aff_rows[e], w1)
        w2 = jnp.where(e2 == e, aff_rows[e], w2)
    tot = w1 + w2
    ids_ref[...] = jnp.zeros(ids_ref.shape, I32)
    wts_ref[...] = jnp.zeros(wts_ref.shape, F32)
    ids_ref[0:1, :] = e1
    ids_ref[1:2, :] = e2
    wts_ref[0:1, :] = w1 / tot
    wts_ref[1:2, :] = w2 / tot


def router_weight_pair(w_router):
    w_t = w_router.T.astype(F32)
    w_hi = w_t.astype(BF16)
    return jnp.stack([w_hi, (w_t - w_hi.astype(F32)).astype(BF16)], axis=0)


def moe_router(x, row_off, n_rows, g, mod, w_router_t, b_router, shift_idx, scale_idx, tm=512):
    D = x.shape[1]
    off_b = row_off // tm
    per_seg = SEG_ROWS // tm
    return pl.pallas_call(
        functools.partial(_router_kernel, shift_idx=shift_idx, scale_idx=scale_idx),
        out_shape=(jax.ShapeDtypeStruct((n_rows, D), BF16),
                   jax.ShapeDtypeStruct((8, n_rows), I32),
                   jax.ShapeDtypeStruct((8, n_rows), F32)),
        grid=(n_rows // tm,),
        in_specs=[pl.BlockSpec((tm, D), lambda i: (i + off_b, 0)),
                  pl.BlockSpec((1, D), lambda i: (0, 0)),
                  pl.BlockSpec((1, 6, D), lambda i: ((i + off_b) // per_seg, 0, 0)),
                  pl.BlockSpec((2, N_EXPERTS, D), lambda i: (0, 0, 0)),
                  pl.BlockSpec((N_EXPERTS, 1), lambda i: (0, 0))],
        out_specs=(pl.BlockSpec((tm, D), lambda i: (i, 0)),
                   pl.BlockSpec((8, tm), lambda i: (0, i)),
                   pl.BlockSpec((8, tm), lambda i: (0, i))),
        compiler_params=_cparams(("parallel",), 40),
        name="moe_router",
    )(x, g.reshape(1, D), mod, w_router_t, b_router.reshape(N_EXPERTS, 1).astype(F32))


MOE_CAST_ROWS = 128


def _moe_expert_kernel(be_ref, first_ref, next_ref, nu_ref, x_ref, wg_hbm, wu_hbm, wd_hbm, y_ref,
                       wg_st, wu_st, wd_st, wg_bf, wu_bf, wd_bf, sems, *, e_off):
    i = pl.program_id(0)

    def weight_copies(e):
        return (pltpu.make_async_copy(wg_hbm.at[e_off + e], wg_st, sems.at[0]),
                pltpu.make_async_copy(wu_hbm.at[e_off + e], wu_st, sems.at[1]),
                pltpu.make_async_copy(wd_hbm.at[e_off + e], wd_st, sems.at[2]))

    @pl.when(i == 0)
    def _():
        for cp in weight_copies(be_ref[0]):
            cp.start()

    @pl.when(first_ref[i] == 1)
    def _():
        for cp in weight_copies(be_ref[i]):
            cp.wait()
        for st, bf in ((wg_st, wg_bf), (wu_st, wu_bf), (wd_st, wd_bf)):
            def cast_rows(c, carry, st=st, bf=bf):
                r = pl.multiple_of(c * MOE_CAST_ROWS, MOE_CAST_ROWS)
                bf[pl.ds(r, MOE_CAST_ROWS), :] = st[pl.ds(r, MOE_CAST_ROWS), :].astype(BF16)
                return carry
            lax.fori_loop(0, st.shape[0] // MOE_CAST_ROWS, cast_rows, 0)

        @pl.when(next_ref[i] >= 0)
        def _():
            for cp in weight_copies(next_ref[i]):
                cp.start()

    @pl.when(i < nu_ref[0])
    def _():
        x = x_ref[...]
        gate = jnp.dot(x, wg_bf[...], preferred_element_type=F32)
        up = jnp.dot(x, wu_bf[...], preferred_element_type=F32)
        act = (gate * jax.nn.sigmoid(gate) * up).astype(BF16)
        y_ref[...] = jnp.dot(act, wd_bf[...], preferred_element_type=F32).astype(y_ref.dtype)

    @pl.when(i >= nu_ref[0])
    def _():
        y_ref[...] = jnp.zeros(y_ref.shape, y_ref.dtype)


def moe_experts(xg, plan, wg, wu, wd, e_off):
    R, D = xg.shape
    F = wg.shape[2]
    nb = R // MOE_ROWS
    hbm = pl.BlockSpec(memory_space=pl.ANY)
    return pl.pallas_call(
        functools.partial(_moe_expert_kernel, e_off=e_off),
        out_shape=jax.ShapeDtypeStruct((R, D), BF16),
        grid_spec=pltpu.PrefetchScalarGridSpec(
            num_scalar_prefetch=4,
            grid=(nb,),
            in_specs=[pl.BlockSpec((MOE_ROWS, D), lambda i, *_: (i, 0)), hbm, hbm, hbm],
            out_specs=pl.BlockSpec((MOE_ROWS, D), lambda i, *_: (i, 0)),
            scratch_shapes=[pltpu.VMEM((D, F), F32), pltpu.VMEM((D, F), F32), pltpu.VMEM((F, D), F32),
                            pltpu.VMEM((D, F), BF16), pltpu.VMEM((D, F), BF16), pltpu.VMEM((F, D), BF16),
                            pltpu.SemaphoreType.DMA((3,))]),
        compiler_params=_cparams(("arbitrary",), 56),
        name="moe_experts",
    )(plan["block_e"], plan["first"], plan["next_e"], plan["n_used"], xg, wg, wu, wd)


def _combine_kernel(x_ref, mod_ref, *rest, gate_idx, final, tile_off, chunk_tiles):
    rest = list(rest)
    fg_ref = rest.pop(0) if final else None
    o_ref = rest.pop()
    t = pl.program_id(0) + tile_off
    for c, (lo, n) in enumerate(chunk_tiles):
        y0_ref, y1_ref, w_ref = rest[3 * c:3 * c + 3]

        @pl.when((t >= lo) & (t < lo + n))
        def _(y0_ref=y0_ref, y1_ref=y1_ref, w_ref=w_ref):
            y = w_ref[:, 0:1] * y0_ref[...].astype(F32) + w_ref[:, 1:2] * y1_ref[...].astype(F32)
            x = x_ref[...] + mod_ref[0, gate_idx:gate_idx + 1, :] * y
            if final:
                ms = jnp.mean(x * x, axis=-1, keepdims=True)
                x = x * lax.rsqrt(ms + EPS) * fg_ref[...]
            o_ref[...] = x


def moe_combine(x, row_off, n_rows, parts, mod, gate_idx, final_g=None, tm=512):
    D = x.shape[1]
    xb = row_off // tm
    per_seg = SEG_ROWS // tm
    final = final_g is not None
    in_specs = [pl.BlockSpec((tm, D), lambda i: (i + xb, 0)),
                pl.BlockSpec((1, 6, D), lambda i: ((i + xb) // per_seg, 0, 0))]
    args = [x, mod]
    if final:
        in_specs.append(pl.BlockSpec((1, D), lambda i: (0, 0)))
        args.append(final_g.reshape(1, D))
    chunk_tiles = []
    for c_off, c_n, y0, y1, wts in parts:
        lo, n = c_off // tm, c_n // tm
        chunk_tiles.append((lo, n))

        def chunk_map(i, lo=lo, n=n):
            return (jnp.clip(i + xb - lo, 0, n - 1), 0)

        in_specs += [pl.BlockSpec((tm, D), chunk_map), pl.BlockSpec((tm, D), chunk_map),
                     pl.BlockSpec((tm, 2), chunk_map)]
        args += [y0, y1, wts]
    return pl.pallas_call(
        functools.partial(_combine_kernel, gate_idx=gate_idx, final=final, tile_off=xb,
                          chunk_tiles=tuple(chunk_tiles)),
        out_shape=jax.ShapeDtypeStruct((n_rows, D), F32),
        grid=(n_rows // tm,),
        in_specs=in_specs,
        out_specs=pl.BlockSpec((tm, D), lambda i: (i, 0)),
        compiler_params=_cparams(("arbitrary",), 48),
        name="moe_combine",
    )(*args)


def moe_dispatch_plan(ids, n_tokens):
    n_assign = 2 * n_tokens
    n_blocks = n_assign // MOE_ROWS + N_EXPERTS
    experts = jnp.arange(N_EXPERTS, dtype=I32)
    flat_e = ids.T.reshape(n_assign)
    onehot = (flat_e[:, None] == experts[None, :]).astype(I32)
    csum = jnp.cumsum(onehot, axis=0)
    rank = jnp.sum((csum - onehot) * onehot, axis=1)
    counts = csum[-1]
    padded = (counts + MOE_ROWS - 1) // MOE_ROWS * MOE_ROWS
    pend = jnp.cumsum(padded)
    dest = (pend - padded)[flat_e] + rank
    n_used = (pend[-1] // MOE_ROWS).astype(I32)
    block_idx = jnp.arange(n_blocks, dtype=I32)
    block_e = jnp.minimum(jnp.sum((pend[None, :] <= block_idx[:, None] * MOE_ROWS).astype(I32), axis=1),
                          N_EXPERTS - 1)
    prev_e = jnp.concatenate([jnp.full((1,), -1, I32), block_e[:-1]])
    first = ((block_idx < n_used) & (block_e != prev_e)).astype(I32)
    later = (experts[None, :] > experts[:, None]) & (counts[None, :] > 0)
    next_of = jnp.min(jnp.where(later, experts[None, :], N_EXPERTS), axis=1)
    next_of = jnp.where(next_of == N_EXPERTS, -1, next_of)
    next_e = jnp.sum(jnp.where(block_e[:, None] == experts[None, :], next_of[None, :], 0), axis=1).astype(I32)
    src_tok = jnp.zeros((n_blocks * MOE_ROWS,), I32).at[dest].set(
        jnp.arange(n_assign, dtype=I32) // 2, unique_indices=True, mode="promise_in_bounds")
    plan = {"block_e": block_e, "first": first, "next_e": next_e, "n_used": n_used.reshape(1)}
    return dest.reshape(n_tokens, 2), src_tok, plan


def channel_mixer(x, g, mod, w_router_t, b_router, wg, wu, wd, e_off, chunks, outputs, final_g=None):
    h, ids8, wts8 = moe_router(x, 0, x.shape[0], g, mod, w_router_t, b_router, shift_idx=3, scale_idx=4)
    parts = []
    for c_off, c_n in chunks:
        dest, src_tok, plan = moe_dispatch_plan(ids8[:2, c_off:c_off + c_n], c_n)
        xg = h.at[src_tok + c_off].get(mode="promise_in_bounds")
        ybuf = moe_experts(xg, plan, wg, wu, wd, e_off)
        y0 = ybuf.at[dest[:, 0]].get(mode="promise_in_bounds")
        y1 = ybuf.at[dest[:, 1]].get(mode="promise_in_bounds")
        parts.append((c_off, c_n, y0, y1, wts8[:2, c_off:c_off + c_n].T))
    return [moe_combine(x, o_off, o_n, parts, mod, 5, final_g) for o_off, o_n in outputs]


def kernel(x_prompt, x_sample, c, cache_a_k, cache_a_v, cache_b_k, cache_b_v, state_C, state_n, state_m,
           c_ctx, norm1_g, norm2_g, w_ada, b_ada, w_in_ab, w_out_ab, rel_bias_a, sink_b, w_in_c, b_gates_c,
           norm_c_g, w_out_c, w_router, b_router, w_gate_e, w_up_e, w_down_e, final_norm_g):
    D = D_MODEL
    Np, Nl = BATCH * SEQ, DEC_BATCH * DEC_SEQ
    N = Np + Nl
    H = C_HEADS
    x = None
    x_ctx, x_lat = x_prompt.reshape(Np, D), x_sample.reshape(Nl, D)
    moe_chunks = [(0, N // 2), (N // 2, N // 2)]
    n_e = w_gate_e.shape[1]
    wg_all = w_gate_e.reshape(DEPTH * n_e, D, D_FF_EXPERT)
    wu_all = w_up_e.reshape(DEPTH * n_e, D, D_FF_EXPERT)
    wd_all = w_down_e.reshape(DEPTH * n_e, D_FF_EXPERT, D)

    cvec = jnp.concatenate([c_ctx[None, :], c, jnp.zeros((8 - 1 - DEC_BATCH, D), F32)], axis=0)
    mod_all = ada_mod_all(cvec, w_ada, b_ada).reshape(DEPTH, 8, 6, D)
    w_router_t = router_weight_pair(w_router)

    outs = {}
    for l in range(DEPTH):
        mod = mod_all[l]
        j = l // 2
        if l % 2 == 0:
            w_in = w_in_ab[j]
            if x is not None:
                x_ctx, x_lat = x[:Np], x[Np:]
            proj_ctx = norm_mod_matmul(x_ctx, 0, Np, norm1_g[l], mod, w_in, F32, 0, 1, seg_row_off=0,
                                       precise=True, tm=512)
            proj_lat = norm_mod_matmul(x_lat, 0, Nl, norm1_g[l], mod, w_in, BF16, 0, 1, seg_row_off=Np, tn=768)
            o_ctx = ctx_attention(proj_ctx, sink_b[j], BATCH, SEQ)
            bias_mask = nat_bias_mask(rel_bias_a[j], DEC_SEQ // GRID_W)
            oa = nat_attention(proj_lat, cache_a_k[:, j].reshape(DEC_BATCH, PAST_LEN, A_WIDTH).astype(BF16),
                               cache_a_v[:, j].reshape(DEC_BATCH, PAST_LEN, A_WIDTH).astype(BF16), bias_mask,
                               DEC_BATCH, DEC_SEQ, A_HEADS)
            cos, sin = rope_tables(DEC_SEQ)
            qk_rot = rope_heads(proj_lat, 3 * A_WIDTH, 3 * A_WIDTH + B_Q_WIDTH, cos, sin, DEC_SEQ)
            ob = swa_attention(qk_rot, proj_lat, 3 * A_HEADS + B_HEADS + B_KV_HEADS,
                               cache_b_k[:, j].reshape(DEC_BATCH, PAST_LEN, B_KV_WIDTH),
                               cache_b_v[:, j].reshape(DEC_BATCH, PAST_LEN, B_KV_WIDTH), sink_b[j],
                               DEC_BATCH, DEC_SEQ)
            x = attn_out_residual(o_ctx, oa, ob, jnp.stack(_hi_lo(w_out_ab[j]), axis=0), x_ctx, x_lat, mod,
                                  gate_idx=2)
            kv = proj_ctx[:, A_WIDTH:]
            outs.setdefault("ak", []).append(kv[:, :A_WIDTH].reshape(BATCH, SEQ, A_HEADS, HEAD_DIM))
            outs.setdefault("av", []).append(kv[:, A_WIDTH:2 * A_WIDTH].reshape(BATCH, SEQ, A_HEADS, HEAD_DIM))
            kvb = proj_ctx[:, 3 * A_WIDTH + B_Q_WIDTH:]
            outs.setdefault("bk", []).append(kvb[:, :B_KV_WIDTH].reshape(BATCH, SEQ, B_KV_HEADS, HEAD_DIM))
            outs.setdefault("bv", []).append(kvb[:, B_KV_WIDTH:].reshape(BATCH, SEQ, B_KV_HEADS, HEAD_DIM))
        else:
            n_main = 2 * C_QK_WIDTH + 2 * C_V_WIDTH
            w_main = w_in_c[j]
            w_gates = jnp.pad(w_in_c[j][:, n_main:], ((0, 0), (0, LANES - 4 * H)))
            proj, gates = norm_mod_matmul(x, 0, N, norm1_g[l], mod, w_main, BF16, 0, 1, w_aux=w_gates,
                                          n_out=n_main, tn=1024)
            gates_dir = jnp.pad(gates[:, :4 * H].reshape(N, 2, 2, H).transpose(1, 2, 0, 3),
                                ((0, 0), (0, 0), (0, 0), (0, LANES - H)))
            bg = jnp.pad(b_gates_c[j].astype(F32).reshape(2, 2, 1, H), ((0, 0), (0, 0), (0, 0), (0, LANES - H)))
            S = BATCH + DEC_BATCH
            C0 = jnp.concatenate([jnp.zeros((BATCH, 2, H, C_QK_DIM, C_V_DIM), F32), state_C[:, j]], axis=0)
            n0 = jnp.concatenate([jnp.zeros((BATCH, 2, H, C_QK_DIM), F32), state_n[:, j]], axis=0)
            m0 = jnp.concatenate([jnp.zeros((BATCH, 2, H), F32), state_m[:, j]], axis=0)
            S0 = jnp.concatenate([C0, jnp.broadcast_to(n0[..., None], n0.shape + (LANES,))], axis=-1)
            m0 = jnp.broadcast_to(m0[..., None], (S, 2, H, LANES))
            seq_chunks = [SEQ // C_CHUNK] * BATCH + [DEC_SEQ // C_CHUNK] * DEC_BATCH
            h_dir, Sf, mf = mlstm_bidir(proj, gates_dir, bg, S0, m0, seq_chunks)
            x = mlstm_out_residual(h_dir, proj, 2, norm_c_g[j], w_out_c[j].astype(BF16), x, mod, gate_idx=2)
            outs.setdefault("C", []).append(Sf[:BATCH, ..., :C_V_DIM])
            outs.setdefault("n", []).append(Sf[:BATCH, ..., C_V_DIM])
            outs.setdefault("m", []).append(mf[:BATCH, :, :, 0])
        last = l == DEPTH - 1
        pieces = channel_mixer(x, norm2_g[l], mod, w_router_t, b_router, wg_all, wu_all, wd_all, l * n_e,
                               moe_chunks, [(0, Np), (Np, Nl)] if last else [(0, N)],
                               final_norm_g if last else None)
        if last:
            y_prompt = pieces[0].reshape(BATCH, SEQ, D)
            y_sample = pieces[1].reshape(DEC_BATCH, DEC_SEQ, D)
        else:
            x = pieces[0]

    return (y_prompt, y_sample,
            jnp.stack(outs["ak"], axis=1), jnp.stack(outs["av"], axis=1),
            jnp.stack(outs["bk"], axis=1), jnp.stack(outs["bv"], axis=1),
            jnp.stack(outs["C"], axis=1), jnp.stack(outs["n"], axis=1), jnp.stack(outs["m"], axis=1))
```

```python
import functools

import numpy as np
import jax
import jax.numpy as jnp
from jax import lax
from jax.experimental import pallas as pl
from jax.experimental.pallas import tpu as pltpu

F32 = jnp.float32
BF16 = jnp.bfloat16
I32 = jnp.int32

D_MODEL = 2048
BATCH = 16
SEQ = 256
DEPTH = 2
DEC_BATCH = 4
DEC_SEQ = 4096
PAST_LEN = 512
GRID_W = 64
HEAD_DIM = 128
A_HEADS = 8
NA_ROWS = 8
NA_COLS = 16
B_HEADS = 8
B_KV_HEADS = 2
B_WINDOW = 128
ROPE_THETA = 10000.0
C_HEADS = 8
C_QK_DIM = 128
C_V_DIM = 256
C_CHUNK = 128
N_EXPERTS = 16
N_GROUPS = 4
EXPERTS_PER_GROUP = N_EXPERTS // N_GROUPS
D_FF_EXPERT = 1024
EPS = 1e-6

A_WIDTH = A_HEADS * HEAD_DIM
B_Q_WIDTH = B_HEADS * HEAD_DIM
B_KV_WIDTH = B_KV_HEADS * HEAD_DIM
C_QK_WIDTH = C_HEADS * C_QK_DIM
C_V_WIDTH = C_HEADS * C_V_DIM

SEG_ROWS = 4096
NEG_BIG = -1e30
MOE_ROWS = 256
NORM_ROWS = 256
MIB = 1024 * 1024
LANES = 128

_NT = (((1,), (1,)), ((), ()))


def _cparams(sem, vmem_mib):
    return pltpu.CompilerParams(dimension_semantics=sem, vmem_limit_bytes=vmem_mib * MIB)


def _modulated_norm(x, g, mod, shift_idx, scale_idx):
    ms = jnp.mean(x * x, axis=-1, keepdims=True)
    y = x * lax.rsqrt(ms + EPS) * g
    return y * (1.0 + mod[scale_idx:scale_idx + 1, :]) + mod[shift_idx:shift_idx + 1, :]


def _hi_lo(x):
    hi = x.astype(BF16)
    return hi, (x - hi.astype(F32)).astype(BF16)


def _dot3(a, b, dims=None):
    a_hi, a_lo = _hi_lo(a)
    b_hi, b_lo = _hi_lo(b)
    dims = (((a.ndim - 1,), (0,)), ((), ())) if dims is None else dims
    dot = functools.partial(lax.dot_general, dimension_numbers=dims, preferred_element_type=F32)
    return dot(a_hi, b_hi) + dot(a_lo, b_hi) + dot(a_hi, b_lo)


def _ada_kernel(c_ref, w_ref, b_ref, o_ref):
    c = c_ref[...]
    o_ref[0] = _dot3(c * jax.nn.sigmoid(c), w_ref[0]) + b_ref[0]


def ada_mod_all(cvec8, w_ada, b_ada, tn=1024):
    L, D, D6 = w_ada.shape
    return pl.pallas_call(
        _ada_kernel,
        out_shape=jax.ShapeDtypeStruct((L, 8, D6), F32),
        grid=(L, D6 // tn),
        in_specs=[pl.BlockSpec((8, D), lambda l, j: (0, 0)),
                  pl.BlockSpec((1, D, tn), lambda l, j: (l, 0, j)),
                  pl.BlockSpec((1, 1, tn), lambda l, j: (l, 0, j))],
        out_specs=pl.BlockSpec((1, 8, tn), lambda l, j: (l, 0, j)),
        compiler_params=_cparams(("parallel", "parallel"), 40),
        name="ada_mod",
    )(cvec8, w_ada, b_ada.reshape(L, 1, D6))


def _nmm_kernel(x_ref, g_ref, mod_ref, w_ref, *rest, shift_idx, scale_idx, has_aux, precise):
    rest = list(rest)
    waux_ref = rest.pop(0) if has_aux else None
    o_ref = rest.pop(0)
    oaux_ref = rest.pop(0) if has_aux else None
    h_scr = rest.pop(0)
    hlo_scr = rest.pop(0) if precise else None

    @pl.when(pl.program_id(1) == 0)
    def _():
        for r in range(x_ref.shape[0] // NORM_ROWS):
            rows = pl.ds(r * NORM_ROWS, NORM_ROWS)
            hf = _modulated_norm(x_ref[rows, :], g_ref[...], mod_ref[0], shift_idx, scale_idx)
            h = hf.astype(BF16)
            h_scr[rows, :] = h
            if precise:
                hlo_scr[rows, :] = (hf - h.astype(F32)).astype(BF16)
            if has_aux:
                oaux_ref[rows, :] = jnp.dot(h, waux_ref[...].astype(BF16), preferred_element_type=F32)

    if precise:
        w_hi, w_lo = _hi_lo(w_ref[...])
        acc = (jnp.dot(h_scr[...], w_hi, preferred_element_type=F32)
               + jnp.dot(hlo_scr[...], w_hi, preferred_element_type=F32)
               + jnp.dot(h_scr[...], w_lo, preferred_element_type=F32))
    else:
        acc = jnp.dot(h_scr[...], w_ref[...].astype(BF16), preferred_element_type=F32)
    o_ref[...] = acc.astype(o_ref.dtype)


def norm_mod_matmul(x, row_off, n_rows, g, mod, w, out_dtype, shift_idx, scale_idx, w_aux=None, n_out=None,
                    seg_row_off=None, precise=False, tm=1024, tn=512):
    D = x.shape[1]
    n_out = w.shape[1] if n_out is None else n_out
    off_b = row_off // tm
    seg_b = off_b if seg_row_off is None else seg_row_off // tm
    per_seg = SEG_ROWS // tm
    has_aux = w_aux is not None
    in_specs = [pl.BlockSpec((tm, D), lambda i, j: (i + off_b, 0)),
                pl.BlockSpec((1, D), lambda i, j: (0, 0)),
                pl.BlockSpec((1, 6, D), lambda i, j: ((i + seg_b) // per_seg, 0, 0)),
                pl.BlockSpec((D, tn), lambda i, j: (0, j))]
    out_shape = jax.ShapeDtypeStruct((n_rows, n_out), out_dtype)
    out_specs = pl.BlockSpec((tm, tn), lambda i, j: (i, j))
    args = [x, g.reshape(1, D), mod, w]
    if has_aux:
        n_aux = w_aux.shape[1]
        in_specs.append(pl.BlockSpec((D, n_aux), lambda i, j: (0, 0)))
        out_shape = (out_shape, jax.ShapeDtypeStruct((n_rows, n_aux), F32))
        out_specs = (out_specs, pl.BlockSpec((tm, n_aux), lambda i, j: (i, 0)))
        args.append(w_aux)
    return pl.pallas_call(
        functools.partial(_nmm_kernel, shift_idx=shift_idx, scale_idx=scale_idx, has_aux=has_aux,
                          precise=precise),
        out_shape=out_shape,
        grid=(n_rows // tm, n_out // tn),
        in_specs=in_specs,
        out_specs=out_specs,
        scratch_shapes=[pltpu.VMEM((tm, D), BF16)] * (2 if precise else 1),
        compiler_params=_cparams(("parallel", "arbitrary"), 56),
        name="norm_mod_matmul",
    )(*args)


def _attn_out_kernel(oc_ref, oa_ref, ob_ref, w_ref, xc_ref, xl_ref, mod_ref, o_ref, *, gate_idx, n_ctx_tiles):
    i = pl.program_id(0)
    gate = mod_ref[0, gate_idx:gate_idx + 1, :]

    @pl.when(i < n_ctx_tiles)
    def _():
        oc_hi, oc_lo = _hi_lo(oc_ref[...])
        acc = (jnp.dot(oc_hi, w_ref[0], preferred_element_type=F32)
               + jnp.dot(oc_lo, w_ref[0], preferred_element_type=F32)
               + jnp.dot(oc_hi, w_ref[1], preferred_element_type=F32))
        o_ref[...] = xc_ref[...] + gate * acc

    @pl.when(i >= n_ctx_tiles)
    def _():
        a = jnp.concatenate([oa_ref[...], ob_ref[...]], axis=-1)
        acc = jnp.dot(a, w_ref[0], preferred_element_type=F32)
        o_ref[...] = xl_ref[...] + gate * acc


def attn_out_residual(o_ctx, oa, ob, w, x_ctx, x_lat, mod, gate_idx, tm=256):
    n_ctx, K = o_ctx.shape
    n_lat = oa.shape[0]
    D = w.shape[2]
    nct = n_ctx // tm
    per_seg = SEG_ROWS // tm

    def ctx_map(i):
        return (jnp.minimum(i, nct - 1), 0)

    def lat_map(i):
        return (jnp.maximum(i - nct, 0), 0)

    return pl.pallas_call(
        functools.partial(_attn_out_kernel, gate_idx=gate_idx, n_ctx_tiles=nct),
        out_shape=jax.ShapeDtypeStruct((n_ctx + n_lat, D), F32),
        grid=((n_ctx + n_lat) // tm,),
        in_specs=[pl.BlockSpec((tm, K), ctx_map),
                  pl.BlockSpec((tm, oa.shape[1]), lat_map),
                  pl.BlockSpec((tm, ob.shape[1]), lat_map),
                  pl.BlockSpec((2, K, D), lambda i: (0, 0, 0), pipeline_mode=pl.Buffered(1)),
                  pl.BlockSpec((tm, D), ctx_map),
                  pl.BlockSpec((tm, D), lat_map),
                  pl.BlockSpec((1, 6, D), lambda i: (i // per_seg, 0, 0))],
        out_specs=pl.BlockSpec((tm, D), lambda i: (i, 0)),
        compiler_params=_cparams(("arbitrary",), 48),
        name="attn_out_residual",
    )(o_ctx, oa, ob, w, x_ctx, x_lat, mod)


def _mlstm_out_kernel(hf_ref, hb_ref, o_ref, ng_ref, w_ref, x_ref, mod_ref, out_ref, *, gate_idx):
    hs = hf_ref[...] + hb_ref[...]
    parts = []
    for h in range(C_HEADS):
        sl = slice(h * C_V_DIM, (h + 1) * C_V_DIM)
        xs = hs[:, sl]
        ms = jnp.mean(xs * xs, axis=-1, keepdims=True)
        hn = xs * lax.rsqrt(ms + EPS) * ng_ref[:, sl]
        parts.append((jax.nn.sigmoid(o_ref[:, sl].astype(F32)) * hn).astype(BF16))
    a = jnp.concatenate(parts, axis=-1)
    acc = jnp.dot(a, w_ref[...], preferred_element_type=F32)
    out_ref[...] = x_ref[...] + mod_ref[0, gate_idx:gate_idx + 1, :] * acc


def mlstm_out_residual(h_dir, proj, o_col_block, norm_g, w, x, mod, gate_idx, tm=256):
    n, D = x.shape
    V = C_V_WIDTH
    per_seg = SEG_ROWS // tm
    return pl.pallas_call(
        functools.partial(_mlstm_out_kernel, gate_idx=gate_idx),
        out_shape=jax.ShapeDtypeStruct((n, D), F32),
        grid=(n // tm,),
        in_specs=[pl.BlockSpec((None, tm, V), lambda i: (0, i, 0)),
                  pl.BlockSpec((None, tm, V), lambda i: (1, i, 0)),
                  pl.BlockSpec((tm, V), lambda i: (i, o_col_block)),
                  pl.BlockSpec((1, V), lambda i: (0, 0)),
                  pl.BlockSpec((V, D), lambda i: (0, 0)),
                  pl.BlockSpec((tm, D), lambda i: (i, 0)),
                  pl.BlockSpec((1, 6, D), lambda i: (i // per_seg, 0, 0))],
        out_specs=pl.BlockSpec((tm, D), lambda i: (i, 0)),
        compiler_params=_cparams(("parallel",), 48),
        name="mlstm_out_residual",
    )(h_dir, h_dir, proj, norm_g.reshape(1, V), w, x, mod)


def _ctx_attn_kernel(sink_ref, q_ref, k_ref, v_ref, o_ref):
    h = pl.program_id(1)
    s = _dot3(q_ref[...], k_ref[...], _NT) * (HEAD_DIM ** -0.5)
    sk = sink_ref[h]
    m = jnp.maximum(jnp.max(s, axis=-1, keepdims=True), sk)
    p = jnp.exp(s - m)
    l = jnp.sum(p, axis=-1, keepdims=True) + jnp.exp(sk - m)
    o_ref[...] = (_dot3(p, v_ref[...]) / l).astype(o_ref.dtype)


def ctx_attention(proj, sink_b, n_batch, seq):
    n_heads = A_HEADS + B_HEADS
    group = B_HEADS // B_KV_HEADS
    qb0 = 3 * A_HEADS
    kb0 = qb0 + B_HEADS
    vb0 = kb0 + B_KV_HEADS
    sinks = jnp.concatenate([jnp.full((A_HEADS,), NEG_BIG, F32), sink_b.astype(F32)])

    def q_map(b, h, s):
        return (b, jnp.where(h < A_HEADS, h, qb0 + h - A_HEADS))

    def k_map(b, h, s):
        return (b, jnp.where(h < A_HEADS, A_HEADS + h, kb0 + (h - A_HEADS) // group))

    def v_map(b, h, s):
        return (b, jnp.where(h < A_HEADS, 2 * A_HEADS + h, vb0 + (h - A_HEADS) // group))

    blk = (seq, HEAD_DIM)
    return pl.pallas_call(
        _ctx_attn_kernel,
        out_shape=jax.ShapeDtypeStruct((n_batch * seq, n_heads * HEAD_DIM), F32),
        grid_spec=pltpu.PrefetchScalarGridSpec(
            num_scalar_prefetch=1,
            grid=(n_batch, n_heads),
            in_specs=[pl.BlockSpec(blk, q_map), pl.BlockSpec(blk, k_map), pl.BlockSpec(blk, v_map)],
            out_specs=pl.BlockSpec(blk, lambda b, h, s: (b, h))),
        compiler_params=_cparams(("parallel", "parallel"), 32),
        name="ctx_attention",
    )(sinks, proj, proj, proj)


NAT_QROWS = 4


def nat_bias_mask(rel_bias, rows):
    W = GRID_W
    nb = rows // NAT_QROWS
    kh = min(NA_ROWS, rows)
    n_dr, n_dc = 2 * NA_ROWS - 1, 2 * NA_COLS - 1
    H = rel_bias.shape[0]
    cidx = np.clip(np.arange(W)[None, :] - np.arange(W)[:, None] + NA_COLS - 1, 0, n_dc - 1)
    onehot = jnp.asarray((cidx.reshape(1, W * W) == np.arange(n_dc)[:, None]).astype(np.float32))
    col_bias = jnp.dot(rel_bias.astype(F32).reshape(H * n_dr, n_dc), onehot,
                       precision=lax.Precision.HIGHEST).reshape(H, n_dr, W, W)
    q_rows = []
    for qi in range(NAT_QROWS):
        tiles = [col_bias[:, int(np.clip(NAT_QROWS * (kj - 1) + kjr - qi + NA_ROWS - 1, 0, n_dr - 1))]
                 for kj in range(3) for kjr in range(NAT_QROWS)]
        q_rows.append(jnp.concatenate(tiles, axis=-1))
    bias = jnp.concatenate(q_rows, axis=1)
    variants = []
    for g in (0, 1, nb - 1):
        i = np.arange(NAT_QROWS)[:, None, None, None, None]
        qc = np.arange(W)[None, :, None, None, None]
        j = np.arange(3)[None, None, :, None, None]
        jr = np.arange(NAT_QROWS)[None, None, None, :, None]
        kc = np.arange(W)[None, None, None, None, :]
        r = NAT_QROWS * g + i
        kblk = g - 1 + j
        kr = NAT_QROWS * kblk + jr
        rs = np.clip(r - kh // 2, 0, rows - kh)
        row_ok = (kblk >= 0) & (kblk < nb) & (kr >= rs) & (kr < rs + kh)
        cstart = np.clip(qc - NA_COLS // 2, 0, W - NA_COLS)
        col_ok = (kc >= cstart) & (kc < cstart + NA_COLS)
        ok = np.broadcast_to(row_ok & col_ok, (NAT_QROWS, W, 3, NAT_QROWS, W))
        n_q, n_k = NAT_QROWS * W, 3 * NAT_QROWS * W
        variants.append(jnp.where(jnp.asarray(ok.reshape(1, n_q, n_k)), bias, NEG_BIG))
    return jnp.stack(variants, axis=0)


NAT_HEADS_PER_STEP = 8


def _nat_kernel(q_ref, k0_ref, k1_ref, k2_ref, v0_ref, v1_ref, v2_ref, kc_ref, vc_ref, bm_ref, o_ref):
    scale = HEAD_DIM ** -0.5
    tq = q_ref.shape[0]
    for h in range(q_ref.shape[1] // HEAD_DIM):
        cs = slice(h * HEAD_DIM, (h + 1) * HEAD_DIM)
        q = q_ref[:, cs]
        s_lat = [lax.dot_general(q, k_ref[:, cs], _NT, preferred_element_type=F32) * scale
                 + bm_ref[0, h, :, j * tq:(j + 1) * tq]
                 for j, k_ref in enumerate((k0_ref, k1_ref, k2_ref))]
        s_ctx = lax.dot_general(q, kc_ref[0, :, cs], _NT, preferred_element_type=F32) * scale
        m = jnp.max(s_ctx, axis=-1, keepdims=True)
        for s in s_lat:
            m = jnp.maximum(m, jnp.max(s, axis=-1, keepdims=True))
        p_ctx = jnp.exp(s_ctx - m)
        l = jnp.sum(p_ctx, axis=-1, keepdims=True)
        acc = jnp.dot(p_ctx.astype(BF16), vc_ref[0, :, cs], preferred_element_type=F32)
        for s, v_ref in zip(s_lat, (v0_ref, v1_ref, v2_ref)):
            p = jnp.exp(s - m)
            l = l + jnp.sum(p, axis=-1, keepdims=True)
            acc = acc + jnp.dot(p.astype(BF16), v_ref[:, cs], preferred_element_type=F32)
        o_ref[:, cs] = (acc / l).astype(o_ref.dtype)


def nat_attention(proj, cache_k, cache_v, bias_mask, n_batch, T, n_heads):
    tq = NAT_QROWS * GRID_W
    nb = T // tq
    P = cache_k.shape[1]
    hs = min(NAT_HEADS_PER_STEP, n_heads)
    ng = n_heads // hs

    def kv_map(col0, j):
        return lambda b, h, g: (b * nb + jnp.clip(g - 1 + j, 0, nb - 1), col0 + h)

    blk = (tq, hs * HEAD_DIM)
    in_specs = [pl.BlockSpec(blk, lambda b, h, g: (b * nb + g, h))]
    in_specs += [pl.BlockSpec(blk, kv_map(ng, j)) for j in range(3)]
    in_specs += [pl.BlockSpec(blk, kv_map(2 * ng, j)) for j in range(3)]
    in_specs += [pl.BlockSpec((1, P, hs * HEAD_DIM), lambda b, h, g: (b, 0, h))] * 2
    in_specs += [pl.BlockSpec((1, hs, tq, 3 * tq),
                              lambda b, h, g: (jnp.where(g == 0, 0, jnp.where(g == nb - 1, 2, 1)), h, 0, 0))]
    return pl.pallas_call(
        _nat_kernel,
        out_shape=jax.ShapeDtypeStruct((n_batch * T, n_heads * HEAD_DIM), BF16),
        grid=(n_batch, ng, nb),
        in_specs=in_specs,
        out_specs=pl.BlockSpec(blk, lambda b, h, g: (b * nb + g, h)),
        compiler_params=_cparams(("parallel", "parallel", "arbitrary"), 40),
        name="nat_attention",
    )(proj, proj, proj, proj, proj, proj, proj, cache_k, cache_v, bias_mask)


def rope_tables(T):
    t = jnp.arange(T)
    row = (t // GRID_W).astype(F32)
    col = (t % GRID_W).astype(F32)
    nf = HEAD_DIM // 4
    freqs = ROPE_THETA ** (-jnp.arange(nf, dtype=F32) / nf)
    ar = row[:, None] * freqs
    ac = col[:, None] * freqs
    cos = jnp.concatenate([jnp.cos(ar), jnp.cos(ar), jnp.cos(ac), jnp.cos(ac)], axis=-1)
    sin = jnp.concatenate([-jnp.sin(ar), jnp.sin(ar), -jnp.sin(ac), jnp.sin(ac)], axis=-1)
    return cos, sin


def _rope_kernel(q_ref, k_ref, cos_ref, sin_ref, o_ref):
    nf = HEAD_DIM // 4
    cos = cos_ref[...]
    sin = sin_ref[...]
    lane = lax.broadcasted_iota(I32, cos.shape, 1)
    first_half = (lane & nf) == 0
    col = 0
    for src, gain in ((q_ref, HEAD_DIM ** -0.5), (k_ref, None)):
        c, s = (cos, sin) if gain is None else (cos * gain, sin * gain)
        for h in range(src.shape[1] // HEAD_DIM):
            x = src[:, h * HEAD_DIM:(h + 1) * HEAD_DIM].astype(F32)
            upper = pltpu.roll(x, HEAD_DIM - nf, 1)
            lower = pltpu.roll(x, nf, 1)
            partner = jnp.where(first_half, upper, lower)
            o_ref[:, col:col + HEAD_DIM] = (x * c + partner * s).astype(o_ref.dtype)
            col += HEAD_DIM


def rope_heads(proj, q_col0, k_col0, cos, sin, T, tm=512):
    n = proj.shape[0]
    per_seq = T // tm
    return pl.pallas_call(
        _rope_kernel,
        out_shape=jax.ShapeDtypeStruct((n, B_Q_WIDTH + B_KV_WIDTH), BF16),
        grid=(n // tm,),
        in_specs=[pl.BlockSpec((tm, B_Q_WIDTH), lambda i: (i, q_col0 // B_Q_WIDTH)),
                  pl.BlockSpec((tm, B_KV_WIDTH), lambda i: (i, k_col0 // B_KV_WIDTH)),
                  pl.BlockSpec((tm, HEAD_DIM), lambda i: (i % per_seq, 0)),
                  pl.BlockSpec((tm, HEAD_DIM), lambda i: (i % per_seq, 0))],
        out_specs=pl.BlockSpec((tm, B_Q_WIDTH + B_KV_WIDTH), lambda i: (i, 0)),
        compiler_params=_cparams(("parallel",), 32),
        name="rope_heads",
    )(proj, proj, cos, sin)


SWA_TQ = 2 * B_WINDOW


def _swa_kernel(sink_ref, q_ref, k0_ref, k1_ref, k2_ref, k3_ref, v0_ref, v1_ref, v2_ref, v3_ref,
                kc_ref, vc_ref, o_ref, *, T):
    group = B_HEADS // B_KV_HEADS
    kvh = pl.program_id(1)
    n = pl.program_id(2)
    k = jnp.concatenate([k0_ref[...], k1_ref[...], k2_ref[...], k3_ref[...]], axis=0)
    v = jnp.concatenate([v0_ref[...], v1_ref[...], v2_ref[...], v3_ref[...]], axis=0)
    kc = kc_ref[0]
    vc = vc_ref[0]
    nk = k.shape[0]
    qpos = n * SWA_TQ + lax.broadcasted_iota(I32, (SWA_TQ, nk), 0)
    kpos = n * SWA_TQ - B_WINDOW + lax.broadcasted_iota(I32, (SWA_TQ, nk), 1)
    dist = jnp.abs(qpos - kpos)
    ok = jnp.where(kpos >= 0, jnp.where(kpos < T, dist, B_WINDOW + 1), B_WINDOW + 1) <= B_WINDOW
    outs = []
    for gi in range(group):
        q = q_ref[:, gi * HEAD_DIM:(gi + 1) * HEAD_DIM]
        s_lat = jnp.where(ok, lax.dot_general(q, k, _NT, preferred_element_type=F32), NEG_BIG)
        s_ctx = lax.dot_general(q, kc, _NT, preferred_element_type=F32)
        sk = sink_ref[kvh * group + gi]
        m = jnp.maximum(jnp.maximum(jnp.max(s_lat, axis=-1, keepdims=True),
                                    jnp.max(s_ctx, axis=-1, keepdims=True)), sk)
        p_lat = jnp.exp(s_lat - m)
        p_ctx = jnp.exp(s_ctx - m)
        l = (jnp.sum(p_lat, axis=-1, keepdims=True) + jnp.sum(p_ctx, axis=-1, keepdims=True)
             + jnp.exp(sk - m))
        acc = (jnp.dot(p_lat.astype(BF16), v, preferred_element_type=F32)
               + jnp.dot(p_ctx.astype(BF16), vc, preferred_element_type=F32))
        outs.append((acc / l).astype(o_ref.dtype))
    o_ref[...] = jnp.concatenate(outs, axis=-1)


def swa_attention(qk_rot, proj, v_col0, cache_k, cache_v, sink, n_batch, T):
    group = B_HEADS // B_KV_HEADS
    nq = T // SWA_TQ
    nkb = T // B_WINDOW
    P = cache_k.shape[1]

    def kv_map(col0, j):
        return lambda b, kvh, n, s: (b * nkb + jnp.clip(2 * n - 1 + j, 0, nkb - 1), col0 + kvh)

    kblk = (B_WINDOW, HEAD_DIM)
    in_specs = [pl.BlockSpec((SWA_TQ, group * HEAD_DIM), lambda b, kvh, n, s: (b * nq + n, kvh))]
    in_specs += [pl.BlockSpec(kblk, kv_map(B_HEADS, j)) for j in range(4)]
    in_specs += [pl.BlockSpec(kblk, kv_map(v_col0, j)) for j in range(4)]
    in_specs += [pl.BlockSpec((1, P, HEAD_DIM), lambda b, kvh, n, s: (b, 0, kvh))] * 2
    return pl.pallas_call(
        functools.partial(_swa_kernel, T=T),
        out_shape=jax.ShapeDtypeStruct((n_batch * T, B_Q_WIDTH), BF16),
        grid_spec=pltpu.PrefetchScalarGridSpec(
            num_scalar_prefetch=1,
            grid=(n_batch, B_KV_HEADS, nq),
            in_specs=in_specs,
            out_specs=pl.BlockSpec((SWA_TQ, group * HEAD_DIM), lambda b, kvh, n, s: (b * nq + n, kvh))),
        compiler_params=_cparams(("parallel", "parallel", "arbitrary"), 32),
        name="swa_attention",
    )(sink.astype(F32), qk_rot, qk_rot, qk_rot, qk_rot, qk_rot, proj, proj, proj, proj, cache_k, cache_v)


def _mlstm_kernel(rowblk_ref, seq_ref, first_ref, last_ref, *refs):
    s = pl.program_id(1)
    for d in range(2):
        @pl.when(pl.program_id(0) == d)
        def _(d=d):
            _mlstm_chunk(d, first_ref[s] == 1, last_ref[s] == 1, *refs)


def _mlstm_chunk(d, is_first, is_last, q_ref, k_ref, v_ref, g_ref, bg_ref, S0_ref, m0_ref,
                 h_ref, Sf_ref, mf_ref, S_scr, m_scr):
    H, DK, DV, L, R = C_HEADS, C_QK_DIM, C_V_DIM, C_CHUNK, LANES
    scale = DK ** -0.5

    @pl.when(is_first)
    def _():
        S_scr[...] = S0_ref[0, 0]
        m_scr[...] = m0_ref[0, 0]

    gi = g_ref[0, 0] + bg_ref[0, 0]
    lf = jax.nn.log_sigmoid(g_ref[0, 1] + bg_ref[0, 1])
    row = lax.broadcasted_iota(I32, (L, L), 0)
    col = lax.broadcasted_iota(I32, (L, L), 1)
    causal = col <= row if d == 0 else col >= row
    b_all = jnp.dot(causal.astype(F32), lf, precision=lax.Precision.HIGHEST,
                    preferred_element_type=F32)
    a_row = (gi - b_all).T[0:H, :]
    b_row = b_all.T[0:H, :]
    lane = lax.broadcasted_iota(I32, (H, L), 1)
    cm = a_row
    k = 1
    while k < L:
        if d == 0:
            shifted = jnp.where(lane >= k, pltpu.roll(cm, k, 1), -jnp.inf)
        else:
            shifted = jnp.where(lane < L - k, pltpu.roll(cm, L - k, 1), -jnp.inf)
        cm = jnp.maximum(cm, shifted)
        k *= 2
    end = L - 1 if d == 0 else 0
    m_all = m_scr[...]
    M_row = jnp.maximum(m_all, cm)
    wi_row = jnp.exp(m_all - M_row)
    emt_row = jnp.exp(-(b_row + M_row))
    M_last = M_row[:, end:end + 1]
    b_last = b_row[:, end:end + 1]
    ws_row = jnp.exp(a_row - M_last)
    wc_all = jnp.exp(m_all - M_last)
    m_scr[...] = jnp.broadcast_to(b_last + M_last, m_all.shape)
    cols = jnp.concatenate([M_row, wi_row, emt_row, jnp.zeros((L - 3 * H, L), F32)], axis=0).T
    ones = jnp.ones((L, R), BF16)
    for h in range(H):
        M_col = cols[:, h:h + 1]
        wi_col = cols[:, H + h:H + h + 1]
        emt_col = cols[:, 2 * H + h:2 * H + h + 1]
        qh = q_ref[:, h * DK:(h + 1) * DK]
        kh = k_ref[:, h * DK:(h + 1) * DK]
        vh = v_ref[:, h * DV:(h + 1) * DV]
        v_ext = jnp.concatenate([vh, ones], axis=-1)
        Sh = S_scr[h]

        w = jnp.where(causal, jnp.exp(a_row[h:h + 1, :] - M_col), 0.0)
        sqk = lax.dot_general(qh, kh, _NT, preferred_element_type=F32) * scale * w
        S_hi = Sh.astype(BF16)
        n_lo = (Sh[:, DV:] - S_hi[:, DV:].astype(F32)).astype(BF16)
        inter = jnp.dot(qh, jnp.concatenate([S_hi, n_lo], axis=-1),
                        preferred_element_type=F32) * scale
        qn = inter[:, DV:DV + 1] + inter[:, DV + R:DV + R + 1]
        num = jnp.dot(sqk.astype(BF16), vh, preferred_element_type=F32) + wi_col * inter[:, :DV]
        den = jnp.sum(sqk, axis=-1, keepdims=True) + wi_col * qn
        h_ref[:, h * DV:(h + 1) * DV] = num / jnp.maximum(jnp.abs(den), emt_col)

        kwT = kh.astype(F32).T * ws_row[h:h + 1, :]
        kwT_hi = kwT.astype(BF16)
        kwT_lo = (kwT - kwT_hi.astype(F32)).astype(BF16)
        upd = jnp.dot(kwT_hi, v_ext, preferred_element_type=F32)
        upd_n = upd[:, DV:] + jnp.dot(kwT_lo, ones, preferred_element_type=F32)
        wc = jnp.concatenate([wc_all[h:h + 1, :]] * (DV // R + 1), axis=-1)
        S_scr[h] = wc * Sh + jnp.concatenate([upd[:, :DV], upd_n], axis=-1)

    @pl.when(is_last)
    def _():
        Sf_ref[0, 0] = S_scr[...]
        mf_ref[0, 0] = m_scr[...]


def mlstm_bidir(proj, gates_dir, b_gates_dir, S0, m0, seq_chunks):
    H, DK, DV, L = C_HEADS, C_QK_DIM, C_V_DIM, C_CHUNK
    DS = DV + LANES
    N = proj.shape[0]
    S = len(seq_chunks)
    rowblk, seq_id, first, last = [[], []], [], [], []
    base = 0
    for sq, nc in enumerate(seq_chunks):
        rowblk[0] += [base + c for c in range(nc)]
        rowblk[1] += [base + nc - 1 - c for c in range(nc)]
        seq_id += [sq] * nc
        first += [1] + [0] * (nc - 1)
        last += [0] * (nc - 1) + [1]
        base += nc
    n_steps = base
    rowblk = jnp.asarray(np.array(rowblk, np.int32).reshape(-1))
    tables = (rowblk, jnp.asarray(seq_id, I32), jnp.asarray(first, I32), jnp.asarray(last, I32))

    def row_map(colblk):
        return lambda d, s, rb, sq, fi, la: (rb[d * n_steps + s], colblk)

    def state_map(nd):
        return lambda d, s, rb, sq, fi, la: (sq[s], d) + (0,) * nd

    in_specs = [pl.BlockSpec((L, H * DK), row_map(0)),
                pl.BlockSpec((L, H * DK), row_map(1)),
                pl.BlockSpec((L, H * DV), row_map(1)),
                pl.BlockSpec((1, 2, L, LANES), lambda d, s, rb, sq, fi, la: (d, 0, rb[d * n_steps + s], 0)),
                pl.BlockSpec((1, 2, 1, LANES), lambda d, s, rb, sq, fi, la: (d, 0, 0, 0)),
                pl.BlockSpec((1, 1, H, DK, DS), state_map(3)),
                pl.BlockSpec((1, 1, H, LANES), state_map(2))]
    out_specs = (pl.BlockSpec((None, L, H * DV), lambda d, s, rb, sq, fi, la: (d, rb[d * n_steps + s], 0)),
                 pl.BlockSpec((1, 1, H, DK, DS), state_map(3)),
                 pl.BlockSpec((1, 1, H, LANES), state_map(2)))
    out_shape = (jax.ShapeDtypeStruct((2, N, H * DV), F32),
                 jax.ShapeDtypeStruct((S, 2, H, DK, DS), F32),
                 jax.ShapeDtypeStruct((S, 2, H, LANES), F32))
    return pl.pallas_call(
        _mlstm_kernel,
        out_shape=out_shape,
        grid_spec=pltpu.PrefetchScalarGridSpec(
            num_scalar_prefetch=4,
            grid=(2, n_steps),
            in_specs=in_specs,
            out_specs=out_specs,
            scratch_shapes=[pltpu.VMEM((H, DK, DS), F32), pltpu.VMEM((H, LANES), F32)]),
        compiler_params=_cparams(("arbitrary", "arbitrary"), 40),
        name="mlstm_bidir",
    )(*tables, proj, proj, proj, gates_dir, b_gates_dir, S0, m0)


def _top2_of4(vals):
    m1, i1 = vals[0], jnp.zeros(vals[0].shape, I32)
    for j in range(1, 4):
        better = vals[j] > m1
        m1 = jnp.where(better, vals[j], m1)
        i1 = jnp.where(better, j, i1)
    m2, i2 = jnp.full(vals[0].shape, -jnp.inf, F32), jnp.zeros(vals[0].shape, I32)
    for j in range(4):
        cand = jnp.where(i1 == j, -jnp.inf, vals[j])
        better = cand > m2
        m2 = jnp.where(better, cand, m2)
        i2 = jnp.where(better, j, i2)
    return m1, i1, m2, i2


def _router_kernel(x_ref, g_ref, mod_ref, wr_ref, br_ref, h_ref, ids_ref, wts_ref, *, shift_idx, scale_idx):
    h = _modulated_norm(x_ref[...], g_ref[...], mod_ref[0], shift_idx, scale_idx)
    h_hi = h.astype(BF16)
    h_ref[...] = h_hi
    h_lo = (h - h_hi.astype(F32)).astype(BF16)
    w_hi = wr_ref[0]
    w_lo = wr_ref[1]
    logits = (lax.dot_general(w_hi, h_hi, _NT, preferred_element_type=F32)
              + lax.dot_general(w_lo, h_hi, _NT, preferred_element_type=F32)
              + lax.dot_general(w_hi, h_lo, _NT, preferred_element_type=F32))
    aff = jax.nn.sigmoid(logits)
    sel = aff + br_ref[...]
    aff_rows = [aff[e:e + 1, :] for e in range(N_EXPERTS)]
    sel_rows = [sel[e:e + 1, :] for e in range(N_EXPERTS)]
    tops = [_top2_of4(sel_rows[4 * gidx:4 * gidx + 4]) for gidx in range(N_GROUPS)]
    best = tops[0][0] + tops[0][2]
    grp = jnp.zeros(best.shape, I32)
    i1, i2 = tops[0][1], tops[0][3]
    for gidx in range(1, N_GROUPS):
        score = tops[gidx][0] + tops[gidx][2]
        better = score > best
        best = jnp.where(better, score, best)
        grp = jnp.where(better, gidx, grp)
        i1 = jnp.where(better, tops[gidx][1], i1)
        i2 = jnp.where(better, tops[gidx][3], i2)
    e1 = grp * EXPERTS_PER_GROUP + i1
    e2 = grp * EXPERTS_PER_GROUP + i2
    w1 = jnp.zeros(best.shape, F32)
    w2 = jnp.zeros(best.shape, F32)
    for e in range(N_EXPERTS):
        w1 = jnp.where(e1 == e, aff_rows[e], w1)
        w2 = jnp.where(e2 == e, aff_rows[e], w2)
    tot = w1 + w2
    ids_ref[...] = jnp.zeros(ids_ref.shape, I32)
    wts_ref[...] = jnp.zeros(wts_ref.shape, F32)
    ids_ref[0:1, :] = e1
    ids_ref[1:2, :] = e2
    wts_ref[0:1, :] = w1 / tot
    wts_ref[1:2, :] = w2 / tot


def router_weight_pair(w_router):
    w_t = w_router.T.astype(F32)
    w_hi = w_t.astype(BF16)
    return jnp.stack([w_hi, (w_t - w_hi.astype(F32)).astype(BF16)], axis=0)


def moe_router(x, row_off, n_rows, g, mod, w_router_t, b_router, shift_idx, scale_idx, tm=512):
    D = x.shape[1]
    off_b = row_off // tm
    per_seg = SEG_ROWS // tm
    return pl.pallas_call(
        functools.partial(_router_kernel, shift_idx=shift_idx, scale_idx=scale_idx),
        out_shape=(jax.ShapeDtypeStruct((n_rows, D), BF16),
                   jax.ShapeDtypeStruct((8, n_rows), I32),
                   jax.ShapeDtypeStruct((8, n_rows), F32)),
        grid=(n_rows // tm,),
        in_specs=[pl.BlockSpec((tm, D), lambda i: (i + off_b, 0)),
                  pl.BlockSpec((1, D), lambda i: (0, 0)),
                  pl.BlockSpec((1, 6, D), lambda i: ((i + off_b) // per_seg, 0, 0)),
                  pl.BlockSpec((2, N_EXPERTS, D), lambda i: (0, 0, 0)),
                  pl.BlockSpec((N_EXPERTS, 1), lambda i: (0, 0))],
        out_specs=(pl.BlockSpec((tm, D), lambda i: (i, 0)),
                   pl.BlockSpec((8, tm), lambda i: (0, i)),
                   pl.BlockSpec((8, tm), lambda i: (0, i))),
        compiler_params=_cparams(("parallel",), 40),
        name="moe_router",
    )(x, g.reshape(1, D), mod, w_router_t, b_router.reshape(N_EXPERTS, 1).astype(F32))


MOE_CAST_ROWS = 128


def _moe_expert_kernel(be_ref, first_ref, next_ref, nu_ref, x_ref, wg_hbm, wu_hbm, wd_hbm, y_ref,
                       wg_st, wu_st, wd_st, wg_bf, wu_bf, wd_bf, sems, *, e_off):
    i = pl.program_id(0)

    def weight_copies(e):
        return (pltpu.make_async_copy(wg_hbm.at[e_off + e], wg_st, sems.at[0]),
                pltpu.make_async_copy(wu_hbm.at[e_off + e], wu_st, sems.at[1]),
                pltpu.make_async_copy(wd_hbm.at[e_off + e], wd_st, sems.at[2]))

    @pl.when(i == 0)
    def _():
        for cp in weight_copies(be_ref[0]):
            cp.start()

    @pl.when(first_ref[i] == 1)
    def _():
        for cp in weight_copies(be_ref[i]):
            cp.wait()
        for st, bf in ((wg_st, wg_bf), (wu_st, wu_bf), (wd_st, wd_bf)):
            def cast_rows(c, carry, st=st, bf=bf):
                r = pl.multiple_of(c * MOE_CAST_ROWS, MOE_CAST_ROWS)
                bf[pl.ds(r, MOE_CAST_ROWS), :] = st[pl.ds(r, MOE_CAST_ROWS), :].astype(BF16)
                return carry
            lax.fori_loop(0, st.shape[0] // MOE_CAST_ROWS, cast_rows, 0)

        @pl.when(next_ref[i] >= 0)
        def _():
            for cp in weight_copies(next_ref[i]):
                cp.start()

    @pl.when(i < nu_ref[0])
    def _():
        x = x_ref[...]
        gate = jnp.dot(x, wg_bf[...], preferred_element_type=F32)
        up = jnp.dot(x, wu_bf[...], preferred_element_type=F32)
        act = (gate * jax.nn.sigmoid(gate) * up).astype(BF16)
        y_ref[...] = jnp.dot(act, wd_bf[...], preferred_element_type=F32).astype(y_ref.dtype)

    @pl.when(i >= nu_ref[0])
    def _():
        y_ref[...] = jnp.zeros(y_ref.shape, y_ref.dtype)


def moe_experts(xg, plan, wg, wu, wd, e_off):
    R, D = xg.shape
    F = wg.shape[2]
    nb = R // MOE_ROWS
    hbm = pl.BlockSpec(memory_space=pl.ANY)
    return pl.pallas_call(
        functools.partial(_moe_expert_kernel, e_off=e_off),
        out_shape=jax.ShapeDtypeStruct((R, D), BF16),
        grid_spec=pltpu.PrefetchScalarGridSpec(
            num_scalar_prefetch=4,
            grid=(nb,),
            in_specs=[pl.BlockSpec((MOE_ROWS, D), lambda i, *_: (i, 0)), hbm, hbm, hbm],
            out_specs=pl.BlockSpec((MOE_ROWS, D), lambda i, *_: (i, 0)),
            scratch_shapes=[pltpu.VMEM((D, F), F32), pltpu.VMEM((D, F), F32), pltpu.VMEM((F, D), F32),
                            pltpu.VMEM((D, F), BF16), pltpu.VMEM((D, F), BF16), pltpu.VMEM((F, D), BF16),
                            pltpu.SemaphoreType.DMA((3,))]),
        compiler_params=_cparams(("arbitrary",), 56),
        name="moe_experts",
    )(plan["block_e"], plan["first"], plan["next_e"], plan["n_used"], xg, wg, wu, wd)


def _combine_kernel(x_ref, mod_ref, *rest, gate_idx, final, tile_off, chunk_tiles):
    rest = list(rest)
    fg_ref = rest.pop(0) if final else None
    o_ref = rest.pop()
    t = pl.program_id(0) + tile_off
    for c, (lo, n) in enumerate(chunk_tiles):
        y0_ref, y1_ref, w_ref = rest[3 * c:3 * c + 3]

        @pl.when((t >= lo) & (t < lo + n))
        def _(y0_ref=y0_ref, y1_ref=y1_ref, w_ref=w_ref):
            y = w_ref[:, 0:1] * y0_ref[...].astype(F32) + w_ref[:, 1:2] * y1_ref[...].astype(F32)
            x = x_ref[...] + mod_ref[0, gate_idx:gate_idx + 1, :] * y
            if final:
                ms = jnp.mean(x * x, axis=-1, keepdims=True)
                x = x * lax.rsqrt(ms + EPS) * fg_ref[...]
            o_ref[...] = x


def moe_combine(x, row_off, n_rows, parts, mod, gate_idx, final_g=None, tm=512):
    D = x.shape[1]
    xb = row_off // tm
    per_seg = SEG_ROWS // tm
    final = final_g is not None
    in_specs = [pl.BlockSpec((tm, D), lambda i: (i + xb, 0)),
                pl.BlockSpec((1, 6, D), lambda i: ((i + xb) // per_seg, 0, 0))]
    args = [x, mod]
    if final:
        in_specs.append(pl.BlockSpec((1, D), lambda i: (0, 0)))
        args.append(final_g.reshape(1, D))
    chunk_tiles = []
    for c_off, c_n, y0, y1, wts in parts:
        lo, n = c_off // tm, c_n // tm
        chunk_tiles.append((lo, n))

        def chunk_map(i, lo=lo, n=n):
            return (jnp.clip(i + xb - lo, 0, n - 1), 0)

        in_specs += [pl.BlockSpec((tm, D), chunk_map), pl.BlockSpec((tm, D), chunk_map),
                     pl.BlockSpec((tm, 2), chunk_map)]
        args += [y0, y1, wts]
    return pl.pallas_call(
        functools.partial(_combine_kernel, gate_idx=gate_idx, final=final, tile_off=xb,
                          chunk_tiles=tuple(chunk_tiles)),
        out_shape=jax.ShapeDtypeStruct((n_rows, D), F32),
        grid=(n_rows // tm,),
        in_specs=in_specs,
        out_specs=pl.BlockSpec((tm, D), lambda i: (i, 0)),
        compiler_params=_cparams(("arbitrary",), 48),
        name="moe_combine",
    )(*args)


def moe_dispatch_plan(ids, n_tokens):
    n_assign = 2 * n_tokens
    n_blocks = n_assign // MOE_ROWS + N_EXPERTS
    experts = jnp.arange(N_EXPERTS, dtype=I32)
    flat_e = ids.T.reshape(n_assign)
    onehot = (flat_e[:, None] == experts[None, :]).astype(I32)
    csum = jnp.cumsum(onehot, axis=0)
    rank = jnp.sum((csum - onehot) * onehot, axis=1)
    counts = csum[-1]
    padded = (counts + MOE_ROWS - 1) // MOE_ROWS * MOE_ROWS
    pend = jnp.cumsum(padded)
    dest = (pend - padded)[flat_e] + rank
    n_used = (pend[-1] // MOE_ROWS).astype(I32)
    block_idx = jnp.arange(n_blocks, dtype=I32)
    block_e = jnp.minimum(jnp.sum((pend[None, :] <= block_idx[:, None] * MOE_ROWS).astype(I32), axis=1),
                          N_EXPERTS - 1)
    prev_e = jnp.concatenate([jnp.full((1,), -1, I32), block_e[:-1]])
    first = ((block_idx < n_used) & (block_e != prev_e)).astype(I32)
    later = (experts[None, :] > experts[:, None]) & (counts[None, :] > 0)
    next_of = jnp.min(jnp.where(later, experts[None, :], N_EXPERTS), axis=1)
    next_of = jnp.where(next_of == N_EXPERTS, -1, next_of)
    next_e = jnp.sum(jnp.where(block_e[:, None] == experts[None, :], next_of[None, :], 0), axis=1).astype(I32)
    src_tok = jnp.zeros((n_blocks * MOE_ROWS,), I32).at[dest].set(
        jnp.arange(n_assign, dtype=I32) // 2, unique_indices=True, mode="promise_in_bounds")
    plan = {"block_e": block_e, "first": first, "next_e": next_e, "n_used": n_used.reshape(1)}
    return dest.reshape(n_tokens, 2), src_tok, plan


def channel_mixer(x, g, mod, w_router_t, b_router, wg, wu, wd, e_off, chunks, outputs, final_g=None):
    h, ids8, wts8 = moe_router(x, 0, x.shape[0], g, mod, w_router_t, b_router, shift_idx=3, scale_idx=4)
    parts = []
    for c_off, c_n in chunks:
        dest, src_tok, plan = moe_dispatch_plan(ids8[:2, c_off:c_off + c_n], c_n)
        xg = h.at[src_tok + c_off].get(mode="promise_in_bounds")
        ybuf = moe_experts(xg, plan, wg, wu, wd, e_off)
        y0 = ybuf.at[dest[:, 0]].get(mode="promise_in_bounds")
        y1 = ybuf.at[dest[:, 1]].get(mode="promise_in_bounds")
        parts.append((c_off, c_n, y0, y1, wts8[:2, c_off:c_off + c_n].T))
    return [moe_combine(x, o_off, o_n, parts, mod, 5, final_g) for o_off, o_n in outputs]


def kernel(x_prompt, x_sample, c, cache_a_k, cache_a_v, cache_b_k, cache_b_v, state_C, state_n, state_m,
           c_ctx, norm1_g, norm2_g, w_ada, b_ada, w_in_ab, w_out_ab, rel_bias_a, sink_b, w_in_c, b_gates_c,
           norm_c_g, w_out_c, w_router, b_router, w_gate_e, w_up_e, w_down_e, final_norm_g):
    D = D_MODEL
    Np, Nl = BATCH * SEQ, DEC_BATCH * DEC_SEQ
    N = Np + Nl
    H = C_HEADS
    x = None
    x_ctx, x_lat = x_prompt.reshape(Np, D), x_sample.reshape(Nl, D)
    moe_chunks = [(0, N)]
    n_e = w_gate_e.shape[1]
    wg_all = w_gate_e.reshape(DEPTH * n_e, D, D_FF_EXPERT)
    wu_all = w_up_e.reshape(DEPTH * n_e, D, D_FF_EXPERT)
    wd_all = w_down_e.reshape(DEPTH * n_e, D_FF_EXPERT, D)

    cvec = jnp.concatenate([c_ctx[None, :], c, jnp.zeros((8 - 1 - DEC_BATCH, D), F32)], axis=0)
    mod_all = ada_mod_all(cvec, w_ada, b_ada).reshape(DEPTH, 8, 6, D)
    w_router_t = router_weight_pair(w_router)

    outs = {}
    for l in range(DEPTH):
        mod = mod_all[l]
        j = l // 2
        if l % 2 == 0:
            w_in = w_in_ab[j]
            if x is not None:
                x_ctx, x_lat = x[:Np], x[Np:]
            proj_ctx = norm_mod_matmul(x_ctx, 0, Np, norm1_g[l], mod, w_in, F32, 0, 1, seg_row_off=0,
                                       precise=True, tm=512)
            proj_lat = norm_mod_matmul(x_lat, 0, Nl, norm1_g[l], mod, w_in, BF16, 0, 1, seg_row_off=Np, tn=768)
            o_ctx = ctx_attention(proj_ctx, sink_b[j], BATCH, SEQ)
            bias_mask = nat_bias_mask(rel_bias_a[j], DEC_SEQ // GRID_W)
            oa = nat_attention(proj_lat, cache_a_k[:, j].reshape(DEC_BATCH, PAST_LEN, A_WIDTH).astype(BF16),
                               cache_a_v[:, j].reshape(DEC_BATCH, PAST_LEN, A_WIDTH).astype(BF16), bias_mask,
                               DEC_BATCH, DEC_SEQ, A_HEADS)
            cos, sin = rope_tables(DEC_SEQ)
            qk_rot = rope_heads(proj_lat, 3 * A_WIDTH, 3 * A_WIDTH + B_Q_WIDTH, cos, sin, DEC_SEQ)
            ob = swa_attention(qk_rot, proj_lat, 3 * A_HEADS + B_HEADS + B_KV_HEADS,
                               cache_b_k[:, j].reshape(DEC_BATCH, PAST_LEN, B_KV_WIDTH).astype(BF16),
                               cache_b_v[:, j].reshape(DEC_BATCH, PAST_LEN, B_KV_WIDTH).astype(BF16), sink_b[j],
                               DEC_BATCH, DEC_SEQ)
            x = attn_out_residual(o_ctx, oa, ob, jnp.stack(_hi_lo(w_out_ab[j]), axis=0), x_ctx, x_lat, mod,
                                  gate_idx=2)
            kv = proj_ctx[:, A_WIDTH:]
            outs.setdefault("ak", []).append(kv[:, :A_WIDTH].reshape(BATCH, SEQ, A_HEADS, HEAD_DIM))
            outs.setdefault("av", []).append(kv[:, A_WIDTH:2 * A_WIDTH].reshape(BATCH, SEQ, A_HEADS, HEAD_DIM))
            kvb = proj_ctx[:, 3 * A_WIDTH + B_Q_WIDTH:]
            outs.setdefault("bk", []).append(kvb[:, :B_KV_WIDTH].reshape(BATCH, SEQ, B_KV_HEADS, HEAD_DIM))
            outs.setdefault("bv", []).append(kvb[:, B_KV_WIDTH:].reshape(BATCH, SEQ, B_KV_HEADS, HEAD_DIM))
        else:
            n_main = 2 * C_QK_WIDTH + 2 * C_V_WIDTH
            w_main = w_in_c[j]
            w_gates = jnp.pad(w_in_c[j][:, n_main:], ((0, 0), (0, LANES - 4 * H)))
            proj, gates = norm_mod_matmul(x, 0, N, norm1_g[l], mod, w_main, BF16, 0, 1, w_aux=w_gates,
                                          n_out=n_main, tn=1024)
            gates_dir = jnp.pad(gates[:, :4 * H].reshape(N, 2, 2, H).transpose(1, 2, 0, 3),
                                ((0, 0), (0, 0), (0, 0), (0, LANES - H)))
            bg = jnp.pad(b_gates_c[j].astype(F32).reshape(2, 2, 1, H), ((0, 0), (0, 0), (0, 0), (0, LANES - H)))
            S = BATCH + DEC_BATCH
            C0 = jnp.concatenate([jnp.zeros((BATCH, 2, H, C_QK_DIM, C_V_DIM), F32), state_C[:, j]], axis=0)
            n0 = jnp.concatenate([jnp.zeros((BATCH, 2, H, C_QK_DIM), F32), state_n[:, j]], axis=0)
            m0 = jnp.concatenate([jnp.zeros((BATCH, 2, H), F32), state_m[:, j]], axis=0)
            S0 = jnp.concatenate([C0, jnp.broadcast_to(n0[..., None], n0.shape + (LANES,))], axis=-1)
            m0 = jnp.broadcast_to(m0[..., None], (S, 2, H, LANES))
            seq_chunks = [SEQ // C_CHUNK] * BATCH + [DEC_SEQ // C_CHUNK] * DEC_BATCH
            h_dir, Sf, mf = mlstm_bidir(proj, gates_dir, bg, S0, m0, seq_chunks)
            x = mlstm_out_residual(h_dir, proj, 2, norm_c_g[j], w_out_c[j].astype(BF16), x, mod, gate_idx=2)
            outs.setdefault("C", []).append(Sf[:BATCH, ..., :C_V_DIM])
            outs.setdefault("n", []).append(Sf[:BATCH, ..., C_V_DIM])
            outs.setdefault("m", []).append(mf[:BATCH, :, :, 0])
        last = l == DEPTH - 1
        pieces = channel_mixer(x, norm2_g[l], mod, w_router_t, b_router, wg_all, wu_all, wd_all, l * n_e,
                               moe_chunks, [(0, Np), (Np, Nl)] if last else [(0, N)],
                               final_norm_g if last else None)
        if last:
            y_prompt = pieces[0].reshape(BATCH, SEQ, D)
            y_sample = pieces[1].reshape(DEC_BATCH, DEC_SEQ, D)
        else:
            x = pieces[0]

    return (y_prompt, y_sample,
            jnp.stack(outs["ak"], axis=1), jnp.stack(outs["av"], axis=1),
            jnp.stack(outs["bk"], axis=1), jnp.stack(outs["bv"], axis=1),
            jnp.stack(outs["C"], axis=1), jnp.stack(outs["n"], axis=1), jnp.stack(outs["m"], axis=1))
```

```python
import functools

import numpy as np
import jax
import jax.numpy as jnp
from jax import lax
from jax.experimental import pallas as pl
from jax.experimental.pallas import tpu as pltpu

F32 = jnp.float32
BF16 = jnp.bfloat16
I32 = jnp.int32

D_MODEL = 2048
BATCH = 16
SEQ = 256
DEPTH = 2
DEC_BATCH = 4
DEC_SEQ = 4096
PAST_LEN = 512
GRID_W = 64
HEAD_DIM = 128
A_HEADS = 8
NA_ROWS = 8
NA_COLS = 16
B_HEADS = 8
B_KV_HEADS = 2
B_WINDOW = 128
ROPE_THETA = 10000.0
C_HEADS = 8
C_QK_DIM = 128
C_V_DIM = 256
C_CHUNK = 128
N_EXPERTS = 16
N_GROUPS = 4
EXPERTS_PER_GROUP = N_EXPERTS // N_GROUPS
D_FF_EXPERT = 1024
EPS = 1e-6

A_WIDTH = A_HEADS * HEAD_DIM
B_Q_WIDTH = B_HEADS * HEAD_DIM
B_KV_WIDTH = B_KV_HEADS * HEAD_DIM
C_QK_WIDTH = C_HEADS * C_QK_DIM
C_V_WIDTH = C_HEADS * C_V_DIM

SEG_ROWS = 4096
NEG_BIG = -1e30
MOE_ROWS = 256
NORM_ROWS = 256
MIB = 1024 * 1024
LANES = 128

_NT = (((1,), (1,)), ((), ()))


def _cparams(sem, vmem_mib):
    return pltpu.CompilerParams(dimension_semantics=sem, vmem_limit_bytes=vmem_mib * MIB)


def _modulated_norm(x, g, mod, shift_idx, scale_idx):
    ms = jnp.mean(x * x, axis=-1, keepdims=True)
    y = x * lax.rsqrt(ms + EPS) * g
    return y * (1.0 + mod[scale_idx:scale_idx + 1, :]) + mod[shift_idx:shift_idx + 1, :]


def _hi_lo(x):
    hi = x.astype(BF16)
    return hi, (x - hi.astype(F32)).astype(BF16)


def _dot3(a, b, dims=None):
    a_hi, a_lo = _hi_lo(a)
    b_hi, b_lo = _hi_lo(b)
    dims = (((a.ndim - 1,), (0,)), ((), ())) if dims is None else dims
    dot = functools.partial(lax.dot_general, dimension_numbers=dims, preferred_element_type=F32)
    return dot(a_hi, b_hi) + dot(a_lo, b_hi) + dot(a_hi, b_lo)


def _ada_kernel(c_ref, w_ref, b_ref, o_ref):
    c = c_ref[...]
    o_ref[0] = _dot3(c * jax.nn.sigmoid(c), w_ref[0]) + b_ref[0]


def ada_mod_all(cvec8, w_ada, b_ada, tn=1024):
    L, D, D6 = w_ada.shape
    return pl.pallas_call(
        _ada_kernel,
        out_shape=jax.ShapeDtypeStruct((L, 8, D6), F32),
        grid=(L, D6 // tn),
        in_specs=[pl.BlockSpec((8, D), lambda l, j: (0, 0)),
                  pl.BlockSpec((1, D, tn), lambda l, j: (l, 0, j)),
                  pl.BlockSpec((1, 1, tn), lambda l, j: (l, 0, j))],
        out_specs=pl.BlockSpec((1, 8, tn), lambda l, j: (l, 0, j)),
        compiler_params=_cparams(("parallel", "parallel"), 40),
        name="ada_mod",
    )(cvec8, w_ada, b_ada.reshape(L, 1, D6))


def _nmm_kernel(x_ref, g_ref, mod_ref, w_ref, *rest, shift_idx, scale_idx, has_aux, precise):
    rest = list(rest)
    waux_ref = rest.pop(0) if has_aux else None
    o_ref = rest.pop(0)
    oaux_ref = rest.pop(0) if has_aux else None
    h_scr = rest.pop(0)
    hlo_scr = rest.pop(0) if precise else None

    @pl.when(pl.program_id(1) == 0)
    def _():
        for r in range(x_ref.shape[0] // NORM_ROWS):
            rows = pl.ds(r * NORM_ROWS, NORM_ROWS)
            hf = _modulated_norm(x_ref[rows, :], g_ref[...], mod_ref[0], shift_idx, scale_idx)
            h = hf.astype(BF16)
            h_scr[rows, :] = h
            if precise:
                hlo_scr[rows, :] = (hf - h.astype(F32)).astype(BF16)
            if has_aux:
                oaux_ref[rows, :] = jnp.dot(h, waux_ref[...].astype(BF16), preferred_element_type=F32)

    if precise:
        w_hi, w_lo = _hi_lo(w_ref[...])
        acc = (jnp.dot(h_scr[...], w_hi, preferred_element_type=F32)
               + jnp.dot(hlo_scr[...], w_hi, preferred_element_type=F32)
               + jnp.dot(h_scr[...], w_lo, preferred_element_type=F32))
    else:
        acc = jnp.dot(h_scr[...], w_ref[...].astype(BF16), preferred_element_type=F32)
    o_ref[...] = acc.astype(o_ref.dtype)


def norm_mod_matmul(x, row_off, n_rows, g, mod, w, out_dtype, shift_idx, scale_idx, w_aux=None, n_out=None,
                    seg_row_off=None, precise=False, tm=1024, tn=512):
    D = x.shape[1]
    n_out = w.shape[1] if n_out is None else n_out
    off_b = row_off // tm
    seg_b = off_b if seg_row_off is None else seg_row_off // tm
    per_seg = SEG_ROWS // tm
    has_aux = w_aux is not None
    in_specs = [pl.BlockSpec((tm, D), lambda i, j: (i + off_b, 0)),
                pl.BlockSpec((1, D), lambda i, j: (0, 0)),
                pl.BlockSpec((1, 6, D), lambda i, j: ((i + seg_b) // per_seg, 0, 0)),
                pl.BlockSpec((D, tn), lambda i, j: (0, j))]
    out_shape = jax.ShapeDtypeStruct((n_rows, n_out), out_dtype)
    out_specs = pl.BlockSpec((tm, tn), lambda i, j: (i, j))
    args = [x, g.reshape(1, D), mod, w]
    if has_aux:
        n_aux = w_aux.shape[1]
        in_specs.append(pl.BlockSpec((D, n_aux), lambda i, j: (0, 0)))
        out_shape = (out_shape, jax.ShapeDtypeStruct((n_rows, n_aux), F32))
        out_specs = (out_specs, pl.BlockSpec((tm, n_aux), lambda i, j: (i, 0)))
        args.append(w_aux)
    return pl.pallas_call(
        functools.partial(_nmm_kernel, shift_idx=shift_idx, scale_idx=scale_idx, has_aux=has_aux,
                          precise=precise),
        out_shape=out_shape,
        grid=(n_rows // tm, n_out // tn),
        in_specs=in_specs,
        out_specs=out_specs,
        scratch_shapes=[pltpu.VMEM((tm, D), BF16)] * (2 if precise else 1),
        compiler_params=_cparams(("parallel", "arbitrary"), 56),
        name="norm_mod_matmul",
    )(*args)


def _attn_out_kernel(oc_ref, oa_ref, ob_ref, w_ref, xc_ref, xl_ref, mod_ref, o_ref, *, gate_idx, n_ctx_tiles):
    i = pl.program_id(0)
    gate = mod_ref[0, gate_idx:gate_idx + 1, :]

    @pl.when(i < n_ctx_tiles)
    def _():
        oc_hi, oc_lo = _hi_lo(oc_ref[...])
        acc = (jnp.dot(oc_hi, w_ref[0], preferred_element_type=F32)
               + jnp.dot(oc_lo, w_ref[0], preferred_element_type=F32)
               + jnp.dot(oc_hi, w_ref[1], preferred_element_type=F32))
        o_ref[...] = xc_ref[...] + gate * acc

    @pl.when(i >= n_ctx_tiles)
    def _():
        a = jnp.concatenate([oa_ref[...], ob_ref[...]], axis=-1)
        acc = jnp.dot(a, w_ref[0], preferred_element_type=F32)
        o_ref[...] = xl_ref[...] + gate * acc


def attn_out_residual(o_ctx, oa, ob, w, x_ctx, x_lat, mod, gate_idx, tm=256):
    n_ctx, K = o_ctx.shape
    n_lat = oa.shape[0]
    D = w.shape[2]
    nct = n_ctx // tm
    per_seg = SEG_ROWS // tm

    def ctx_map(i):
        return (jnp.minimum(i, nct - 1), 0)

    def lat_map(i):
        return (jnp.maximum(i - nct, 0), 0)

    return pl.pallas_call(
        functools.partial(_attn_out_kernel, gate_idx=gate_idx, n_ctx_tiles=nct),
        out_shape=jax.ShapeDtypeStruct((n_ctx + n_lat, D), F32),
        grid=((n_ctx + n_lat) // tm,),
        in_specs=[pl.BlockSpec((tm, K), ctx_map),
                  pl.BlockSpec((tm, oa.shape[1]), lat_map),
                  pl.BlockSpec((tm, ob.shape[1]), lat_map),
                  pl.BlockSpec((2, K, D), lambda i: (0, 0, 0), pipeline_mode=pl.Buffered(1)),
                  pl.BlockSpec((tm, D), ctx_map),
                  pl.BlockSpec((tm, D), lat_map),
                  pl.BlockSpec((1, 6, D), lambda i: (i // per_seg, 0, 0))],
        out_specs=pl.BlockSpec((tm, D), lambda i: (i, 0)),
        compiler_params=_cparams(("arbitrary",), 48),
        name="attn_out_residual",
    )(o_ctx, oa, ob, w, x_ctx, x_lat, mod)


def _mlstm_out_kernel(hf_ref, hb_ref, o_ref, ng_ref, w_ref, x_ref, mod_ref, out_ref, *, gate_idx):
    hs = hf_ref[...] + hb_ref[...]
    parts = []
    for h in range(C_HEADS):
        sl = slice(h * C_V_DIM, (h + 1) * C_V_DIM)
        xs = hs[:, sl]
        ms = jnp.mean(xs * xs, axis=-1, keepdims=True)
        hn = xs * lax.rsqrt(ms + EPS) * ng_ref[:, sl]
        parts.append((jax.nn.sigmoid(o_ref[:, sl].astype(F32)) * hn).astype(BF16))
    a = jnp.concatenate(parts, axis=-1)
    acc = jnp.dot(a, w_ref[...], preferred_element_type=F32)
    out_ref[...] = x_ref[...] + mod_ref[0, gate_idx:gate_idx + 1, :] * acc


def mlstm_out_residual(h_dir, proj, o_col_block, norm_g, w, x, mod, gate_idx, tm=256):
    n, D = x.shape
    V = C_V_WIDTH
    per_seg = SEG_ROWS // tm
    return pl.pallas_call(
        functools.partial(_mlstm_out_kernel, gate_idx=gate_idx),
        out_shape=jax.ShapeDtypeStruct((n, D), F32),
        grid=(n // tm,),
        in_specs=[pl.BlockSpec((None, tm, V), lambda i: (0, i, 0)),
                  pl.BlockSpec((None, tm, V), lambda i: (1, i, 0)),
                  pl.BlockSpec((tm, V), lambda i: (i, o_col_block)),
                  pl.BlockSpec((1, V), lambda i: (0, 0)),
                  pl.BlockSpec((V, D), lambda i: (0, 0)),
                  pl.BlockSpec((tm, D), lambda i: (i, 0)),
                  pl.BlockSpec((1, 6, D), lambda i: (i // per_seg, 0, 0))],
        out_specs=pl.BlockSpec((tm, D), lambda i: (i, 0)),
        compiler_params=_cparams(("parallel",), 48),
        name="mlstm_out_residual",
    )(h_dir, h_dir, proj, norm_g.reshape(1, V), w, x, mod)


def _ctx_attn_kernel(sink_ref, q_ref, k_ref, v_ref, o_ref):
    h = pl.program_id(1)
    s = _dot3(q_ref[...], k_ref[...], _NT) * (HEAD_DIM ** -0.5)
    sk = sink_ref[h]
    m = jnp.maximum(jnp.max(s, axis=-1, keepdims=True), sk)
    p = jnp.exp(s - m)
    l = jnp.sum(p, axis=-1, keepdims=True) + jnp.exp(sk - m)
    o_ref[...] = (_dot3(p, v_ref[...]) / l).astype(o_ref.dtype)


def ctx_attention(proj, sink_b, n_batch, seq):
    n_heads = A_HEADS + B_HEADS
    group = B_HEADS // B_KV_HEADS
    qb0 = 3 * A_HEADS
    kb0 = qb0 + B_HEADS
    vb0 = kb0 + B_KV_HEADS
    sinks = jnp.concatenate([jnp.full((A_HEADS,), NEG_BIG, F32), sink_b.astype(F32)])

    def q_map(b, h, s):
        return (b, jnp.where(h < A_HEADS, h, qb0 + h - A_HEADS))

    def k_map(b, h, s):
        return (b, jnp.where(h < A_HEADS, A_HEADS + h, kb0 + (h - A_HEADS) // group))

    def v_map(b, h, s):
        return (b, jnp.where(h < A_HEADS, 2 * A_HEADS + h, vb0 + (h - A_HEADS) // group))

    blk = (seq, HEAD_DIM)
    return pl.pallas_call(
        _ctx_attn_kernel,
        out_shape=jax.ShapeDtypeStruct((n_batch * seq, n_heads * HEAD_DIM), F32),
        grid_spec=pltpu.PrefetchScalarGridSpec(
            num_scalar_prefetch=1,
            grid=(n_batch, n_heads),
            in_specs=[pl.BlockSpec(blk, q_map), pl.BlockSpec(blk, k_map), pl.BlockSpec(blk, v_map)],
            out_specs=pl.BlockSpec(blk, lambda b, h, s: (b, h))),
        compiler_params=_cparams(("parallel", "parallel"), 32),
        name="ctx_attention",
    )(sinks, proj, proj, proj)


NAT_QROWS = 4


def nat_bias_mask(rel_bias, rows):
    W = GRID_W
    nb = rows // NAT_QROWS
    kh = min(NA_ROWS, rows)
    n_dr, n_dc = 2 * NA_ROWS - 1, 2 * NA_COLS - 1
    H = rel_bias.shape[0]
    cidx = np.clip(np.arange(W)[None, :] - np.arange(W)[:, None] + NA_COLS - 1, 0, n_dc - 1)
    onehot = jnp.asarray((cidx.reshape(1, W * W) == np.arange(n_dc)[:, None]).astype(np.float32))
    col_bias = jnp.dot(rel_bias.astype(F32).reshape(H * n_dr, n_dc), onehot,
                       precision=lax.Precision.HIGHEST).reshape(H, n_dr, W, W)
    q_rows = []
    for qi in range(NAT_QROWS):
        tiles = [col_bias[:, int(np.clip(NAT_QROWS * (kj - 1) + kjr - qi + NA_ROWS - 1, 0, n_dr - 1))]
                 for kj in range(3) for kjr in range(NAT_QROWS)]
        q_rows.append(jnp.concatenate(tiles, axis=-1))
    bias = jnp.concatenate(q_rows, axis=1)
    variants = []
    for g in (0, 1, nb - 1):
        i = np.arange(NAT_QROWS)[:, None, None, None, None]
        qc = np.arange(W)[None, :, None, None, None]
        j = np.arange(3)[None, None, :, None, None]
        jr = np.arange(NAT_QROWS)[None, None, None, :, None]
        kc = np.arange(W)[None, None, None, None, :]
        r = NAT_QROWS * g + i
        kblk = g - 1 + j
        kr = NAT_QROWS * kblk + jr
        rs = np.clip(r - kh // 2, 0, rows - kh)
        row_ok = (kblk >= 0) & (kblk < nb) & (kr >= rs) & (kr < rs + kh)
        cstart = np.clip(qc - NA_COLS // 2, 0, W - NA_COLS)
        col_ok = (kc >= cstart) & (kc < cstart + NA_COLS)
        ok = np.broadcast_to(row_ok & col_ok, (NAT_QROWS, W, 3, NAT_QROWS, W))
        n_q, n_k = NAT_QROWS * W, 3 * NAT_QROWS * W
        variants.append(jnp.where(jnp.asarray(ok.reshape(1, n_q, n_k)), bias, NEG_BIG))
    return jnp.stack(variants, axis=0)


NAT_HEADS_PER_STEP = 8


def _nat_kernel(q_ref, k0_ref, k1_ref, k2_ref, v0_ref, v1_ref, v2_ref, kc_ref, vc_ref, bm_ref, o_ref):
    scale = HEAD_DIM ** -0.5
    tq = q_ref.shape[0]
    for h in range(q_ref.shape[1] // HEAD_DIM):
        cs = slice(h * HEAD_DIM, (h + 1) * HEAD_DIM)
        q = q_ref[:, cs]
        s_lat = [lax.dot_general(q, k_ref[:, cs], _NT, preferred_element_type=F32) * scale
                 + bm_ref[0, h, :, j * tq:(j + 1) * tq]
                 for j, k_ref in enumerate((k0_ref, k1_ref, k2_ref))]
        s_ctx = lax.dot_general(q, kc_ref[0, :, cs], _NT, preferred_element_type=F32) * scale
        m = jnp.max(s_ctx, axis=-1, keepdims=True)
        for s in s_lat:
            m = jnp.maximum(m, jnp.max(s, axis=-1, keepdims=True))
        p_ctx = jnp.exp(s_ctx - m)
        l = jnp.sum(p_ctx, axis=-1, keepdims=True)
        acc = jnp.dot(p_ctx.astype(BF16), vc_ref[0, :, cs], preferred_element_type=F32)
        for s, v_ref in zip(s_lat, (v0_ref, v1_ref, v2_ref)):
            p = jnp.exp(s - m)
            l = l + jnp.sum(p, axis=-1, keepdims=True)
            acc = acc + jnp.dot(p.astype(BF16), v_ref[:, cs], preferred_element_type=F32)
        o_ref[:, cs] = (acc / l).astype(o_ref.dtype)


def nat_attention(proj, cache_k, cache_v, bias_mask, n_batch, T, n_heads):
    tq = NAT_QROWS * GRID_W
    nb = T // tq
    P = cache_k.shape[1]
    hs = min(NAT_HEADS_PER_STEP, n_heads)
    ng = n_heads // hs

    def kv_map(col0, j):
        return lambda b, h, g: (b * nb + jnp.clip(g - 1 + j, 0, nb - 1), col0 + h)

    blk = (tq, hs * HEAD_DIM)
    in_specs = [pl.BlockSpec(blk, lambda b, h, g: (b * nb + g, h))]
    in_specs += [pl.BlockSpec(blk, kv_map(ng, j)) for j in range(3)]
    in_specs += [pl.BlockSpec(blk, kv_map(2 * ng, j)) for j in range(3)]
    in_specs += [pl.BlockSpec((1, P, hs * HEAD_DIM), lambda b, h, g: (b, 0, h))] * 2
    in_specs += [pl.BlockSpec((1, hs, tq, 3 * tq),
                              lambda b, h, g: (jnp.where(g == 0, 0, jnp.where(g == nb - 1, 2, 1)), h, 0, 0))]
    return pl.pallas_call(
        _nat_kernel,
        out_shape=jax.ShapeDtypeStruct((n_batch * T, n_heads * HEAD_DIM), BF16),
        grid=(n_batch, ng, nb),
        in_specs=in_specs,
        out_specs=pl.BlockSpec(blk, lambda b, h, g: (b * nb + g, h)),
        compiler_params=_cparams(("parallel", "parallel", "arbitrary"), 40),
        name="nat_attention",
    )(proj, proj, proj, proj, proj, proj, proj, cache_k, cache_v, bias_mask)


def rope_tables(T):
    t = jnp.arange(T)
    row = (t // GRID_W).astype(F32)
    col = (t % GRID_W).astype(F32)
    nf = HEAD_DIM // 4
    freqs = ROPE_THETA ** (-jnp.arange(nf, dtype=F32) / nf)
    ar = row[:, None] * freqs
    ac = col[:, None] * freqs
    cos = jnp.concatenate([jnp.cos(ar), jnp.cos(ar), jnp.cos(ac), jnp.cos(ac)], axis=-1)
    sin = jnp.concatenate([-jnp.sin(ar), jnp.sin(ar), -jnp.sin(ac), jnp.sin(ac)], axis=-1)
    return cos, sin


def _rope_kernel(q_ref, k_ref, cos_ref, sin_ref, o_ref):
    nf = HEAD_DIM // 4
    cos = cos_ref[...]
    sin = sin_ref[...]
    lane = lax.broadcasted_iota(I32, cos.shape, 1)
    first_half = (lane & nf) == 0
    col = 0
    for src, gain in ((q_ref, HEAD_DIM ** -0.5), (k_ref, None)):
        c, s = (cos, sin) if gain is None else (cos * gain, sin * gain)
        for h in range(src.shape[1] // HEAD_DIM):
            x = src[:, h * HEAD_DIM:(h + 1) * HEAD_DIM].astype(F32)
            upper = pltpu.roll(x, HEAD_DIM - nf, 1)
            lower = pltpu.roll(x, nf, 1)
            partner = jnp.where(first_half, upper, lower)
            o_ref[:, col:col + HEAD_DIM] = (x * c + partner * s).astype(o_ref.dtype)
            col += HEAD_DIM


def rope_heads(proj, q_col0, k_col0, cos, sin, T, tm=512):
    n = proj.shape[0]
    per_seq = T // tm
    return pl.pallas_call(
        _rope_kernel,
        out_shape=jax.ShapeDtypeStruct((n, B_Q_WIDTH + B_KV_WIDTH), BF16),
        grid=(n // tm,),
        in_specs=[pl.BlockSpec((tm, B_Q_WIDTH), lambda i: (i, q_col0 // B_Q_WIDTH)),
                  pl.BlockSpec((tm, B_KV_WIDTH), lambda i: (i, k_col0 // B_KV_WIDTH)),
                  pl.BlockSpec((tm, HEAD_DIM), lambda i: (i % per_seq, 0)),
                  pl.BlockSpec((tm, HEAD_DIM), lambda i: (i % per_seq, 0))],
        out_specs=pl.BlockSpec((tm, B_Q_WIDTH + B_KV_WIDTH), lambda i: (i, 0)),
        compiler_params=_cparams(("parallel",), 32),
        name="rope_heads",
    )(proj, proj, cos, sin)


SWA_TQ = 2 * B_WINDOW


def _swa_kernel(sink_ref, q_ref, k0_ref, k1_ref, k2_ref, k3_ref, v0_ref, v1_ref, v2_ref, v3_ref,
                kc_ref, vc_ref, o_ref, *, T):
    group = B_HEADS // B_KV_HEADS
    kvh = pl.program_id(1)
    n = pl.program_id(2)
    k = jnp.concatenate([k0_ref[...], k1_ref[...], k2_ref[...], k3_ref[...]], axis=0)
    v = jnp.concatenate([v0_ref[...], v1_ref[...], v2_ref[...], v3_ref[...]], axis=0)
    kc = kc_ref[0]
    vc = vc_ref[0]
    nk = k.shape[0]
    qpos = n * SWA_TQ + lax.broadcasted_iota(I32, (SWA_TQ, nk), 0)
    kpos = n * SWA_TQ - B_WINDOW + lax.broadcasted_iota(I32, (SWA_TQ, nk), 1)
    dist = jnp.abs(qpos - kpos)
    ok = jnp.where(kpos >= 0, jnp.where(kpos < T, dist, B_WINDOW + 1), B_WINDOW + 1) <= B_WINDOW
    outs = []
    for gi in range(group):
        q = q_ref[:, gi * HEAD_DIM:(gi + 1) * HEAD_DIM]
        s_lat = jnp.where(ok, lax.dot_general(q, k, _NT, preferred_element_type=F32), NEG_BIG)
        s_ctx = lax.dot_general(q, kc, _NT, preferred_element_type=F32)
        sk = sink_ref[kvh * group + gi]
        m = jnp.maximum(jnp.maximum(jnp.max(s_lat, axis=-1, keepdims=True),
                                    jnp.max(s_ctx, axis=-1, keepdims=True)), sk)
        p_lat = jnp.exp(s_lat - m)
        p_ctx = jnp.exp(s_ctx - m)
        l = (jnp.sum(p_lat, axis=-1, keepdims=True) + jnp.sum(p_ctx, axis=-1, keepdims=True)
             + jnp.exp(sk - m))
        acc = (jnp.dot(p_lat.astype(BF16), v, preferred_element_type=F32)
               + jnp.dot(p_ctx.astype(BF16), vc, preferred_element_type=F32))
        outs.append((acc / l).astype(o_ref.dtype))
    o_ref[...] = jnp.concatenate(outs, axis=-1)


def swa_attention(qk_rot, proj, v_col0, cache_k, cache_v, sink, n_batch, T):
    group = B_HEADS // B_KV_HEADS
    nq = T // SWA_TQ
    nkb = T // B_WINDOW
    P = cache_k.shape[1]

    def kv_map(col0, j):
        return lambda b, kvh, n, s: (b * nkb + jnp.clip(2 * n - 1 + j, 0, nkb - 1), col0 + kvh)

    kblk = (B_WINDOW, HEAD_DIM)
    in_specs = [pl.BlockSpec((SWA_TQ, group * HEAD_DIM), lambda b, kvh, n, s: (b * nq + n, kvh))]
    in_specs += [pl.BlockSpec(kblk, kv_map(B_HEADS, j)) for j in range(4)]
    in_specs += [pl.BlockSpec(kblk, kv_map(v_col0, j)) for j in range(4)]
    in_specs += [pl.BlockSpec((1, P, HEAD_DIM), lambda b, kvh, n, s: (b, 0, kvh))] * 2
    return pl.pallas_call(
        functools.partial(_swa_kernel, T=T),
        out_shape=jax.ShapeDtypeStruct((n_batch * T, B_Q_WIDTH), BF16),
        grid_spec=pltpu.PrefetchScalarGridSpec(
            num_scalar_prefetch=1,
            grid=(n_batch, B_KV_HEADS, nq),
            in_specs=in_specs,
            out_specs=pl.BlockSpec((SWA_TQ, group * HEAD_DIM), lambda b, kvh, n, s: (b * nq + n, kvh))),
        compiler_params=_cparams(("parallel", "parallel", "arbitrary"), 32),
        name="swa_attention",
    )(sink.astype(F32), qk_rot, qk_rot, qk_rot, qk_rot, qk_rot, proj, proj, proj, proj, cache_k, cache_v)


def _mlstm_kernel(rowblk_ref, seq_ref, first_ref, last_ref, *refs):
    s = pl.program_id(1)
    for d in range(2):
        @pl.when(pl.program_id(0) == d)
        def _(d=d):
            _mlstm_chunk(d, first_ref[s] == 1, last_ref[s] == 1, *refs)


def _mlstm_chunk(d, is_first, is_last, q_ref, k_ref, v_ref, g_ref, bg_ref, S0_ref, m0_ref,
                 h_ref, Sf_ref, mf_ref, S_scr, m_scr, qk_scr, inter_scr, num_scr):
    H, DK, DV, L, R = C_HEADS, C_QK_DIM, C_V_DIM, C_CHUNK, LANES
    scale = DK ** -0.5

    @pl.when(is_first)
    def _():
        S_scr[...] = S0_ref[0, 0]
        m_scr[...] = m0_ref[0, 0]

    gi = g_ref[0, 0] + bg_ref[0, 0]
    lf = jax.nn.log_sigmoid(g_ref[0, 1] + bg_ref[0, 1])
    row = lax.broadcasted_iota(I32, (L, L), 0)
    col = lax.broadcasted_iota(I32, (L, L), 1)
    causal = col <= row if d == 0 else col >= row
    b_all = jnp.dot(causal.astype(F32), lf, precision=lax.Precision.HIGHEST,
                    preferred_element_type=F32)
    a_row = (gi - b_all).T[0:H, :]
    b_row = b_all.T[0:H, :]
    lane = lax.broadcasted_iota(I32, (H, L), 1)
    cm = a_row
    k = 1
    while k < L:
        if d == 0:
            shifted = jnp.where(lane >= k, pltpu.roll(cm, k, 1), -jnp.inf)
        else:
            shifted = jnp.where(lane < L - k, pltpu.roll(cm, L - k, 1), -jnp.inf)
        cm = jnp.maximum(cm, shifted)
        k *= 2
    end = L - 1 if d == 0 else 0
    m_all = m_scr[...]
    M_row = jnp.maximum(m_all, cm)
    wi_row = jnp.exp(m_all - M_row)
    emt_row = jnp.exp(-(b_row + M_row))
    M_last = M_row[:, end:end + 1]
    b_last = b_row[:, end:end + 1]
    ws_row = jnp.exp(a_row - M_last)
    wc_all = jnp.exp(m_all - M_last)
    m_scr[...] = jnp.broadcast_to(b_last + M_last, m_all.shape)
    cols = jnp.concatenate([M_row, wi_row, emt_row, jnp.zeros((L - 3 * H, L), F32)], axis=0).T
    ones = jnp.ones((L, R), BF16)
    for h in range(H):
        qh = q_ref[:, h * DK:(h + 1) * DK]
        kh = k_ref[:, h * DK:(h + 1) * DK]
        Sh = S_scr[h]
        qk_scr[h] = lax.dot_general(qh, kh, _NT, preferred_element_type=F32)
        S_hi = Sh.astype(BF16)
        n_lo = (Sh[:, DV:] - S_hi[:, DV:].astype(F32)).astype(BF16)
        inter_scr[h] = jnp.dot(qh, jnp.concatenate([S_hi, n_lo], axis=-1),
                               preferred_element_type=F32)
    for h in range(H):
        kh = k_ref[:, h * DK:(h + 1) * DK]
        v_ext = jnp.concatenate([v_ref[:, h * DV:(h + 1) * DV], ones], axis=-1)
        kwT = kh.astype(F32).T * ws_row[h:h + 1, :]
        kwT_hi = kwT.astype(BF16)
        kwT_lo = (kwT - kwT_hi.astype(F32)).astype(BF16)
        upd = jnp.dot(kwT_hi, v_ext, preferred_element_type=F32)
        upd_n = upd[:, DV:] + jnp.dot(kwT_lo, ones, preferred_element_type=F32)
        wc = jnp.concatenate([wc_all[h:h + 1, :]] * (DV // R + 1), axis=-1)
        S_scr[h] = wc * S_scr[h] + jnp.concatenate([upd[:, :DV], upd_n], axis=-1)
    for h in range(H):
        M_col = cols[:, h:h + 1]
        w = jnp.where(causal, jnp.exp(a_row[h:h + 1, :] - M_col), 0.0)
        sqk = qk_scr[h] * scale * w
        qk_scr[h] = sqk
        num_scr[h] = jnp.dot(sqk.astype(BF16), v_ref[:, h * DV:(h + 1) * DV], preferred_element_type=F32)
    for h in range(H):
        wi_col = cols[:, H + h:H + h + 1]
        emt_col = cols[:, 2 * H + h:2 * H + h + 1]
        inter = inter_scr[h] * scale
        qn = inter[:, DV:DV + 1] + inter[:, DV + R:DV + R + 1]
        num = num_scr[h] + wi_col * inter[:, :DV]
        den = jnp.sum(qk_scr[h], axis=-1, keepdims=True) + wi_col * qn
        h_ref[:, h * DV:(h + 1) * DV] = num / jnp.maximum(jnp.abs(den), emt_col)

    @pl.when(is_last)
    def _():
        Sf_ref[0, 0] = S_scr[...]
        mf_ref[0, 0] = m_scr[...]


def mlstm_bidir(proj, gates_dir, b_gates_dir, S0, m0, seq_chunks):
    H, DK, DV, L = C_HEADS, C_QK_DIM, C_V_DIM, C_CHUNK
    DS = DV + LANES
    N = proj.shape[0]
    S = len(seq_chunks)
    rowblk, seq_id, first, last = [[], []], [], [], []
    base = 0
    for sq, nc in enumerate(seq_chunks):
        rowblk[0] += [base + c for c in range(nc)]
        rowblk[1] += [base + nc - 1 - c for c in range(nc)]
        seq_id += [sq] * nc
        first += [1] + [0] * (nc - 1)
        last += [0] * (nc - 1) + [1]
        base += nc
    n_steps = base
    rowblk = jnp.asarray(np.array(rowblk, np.int32).reshape(-1))
    tables = (rowblk, jnp.asarray(seq_id, I32), jnp.asarray(first, I32), jnp.asarray(last, I32))

    def row_map(colblk):
        return lambda d, s, rb, sq, fi, la: (rb[d * n_steps + s], colblk)

    def state_map(nd):
        return lambda d, s, rb, sq, fi, la: (sq[s], d) + (0,) * nd

    in_specs = [pl.BlockSpec((L, H * DK), row_map(0)),
                pl.BlockSpec((L, H * DK), row_map(1)),
                pl.BlockSpec((L, H * DV), row_map(1)),
                pl.BlockSpec((1, 2, L, LANES), lambda d, s, rb, sq, fi, la: (d, 0, rb[d * n_steps + s], 0)),
                pl.BlockSpec((1, 2, 1, LANES), lambda d, s, rb, sq, fi, la: (d, 0, 0, 0)),
                pl.BlockSpec((1, 1, H, DK, DS), state_map(3)),
                pl.BlockSpec((1, 1, H, LANES), state_map(2))]
    out_specs = (pl.BlockSpec((None, L, H * DV), lambda d, s, rb, sq, fi, la: (d, rb[d * n_steps + s], 0)),
                 pl.BlockSpec((1, 1, H, DK, DS), state_map(3)),
                 pl.BlockSpec((1, 1, H, LANES), state_map(2)))
    out_shape = (jax.ShapeDtypeStruct((2, N, H * DV), F32),
                 jax.ShapeDtypeStruct((S, 2, H, DK, DS), F32),
                 jax.ShapeDtypeStruct((S, 2, H, LANES), F32))
    return pl.pallas_call(
        _mlstm_kernel,
        out_shape=out_shape,
        grid_spec=pltpu.PrefetchScalarGridSpec(
            num_scalar_prefetch=4,
            grid=(2, n_steps),
            in_specs=in_specs,
            out_specs=out_specs,
            scratch_shapes=[pltpu.VMEM((H, DK, DS), F32), pltpu.VMEM((H, LANES), F32),
                            pltpu.VMEM((H, L, L), F32), pltpu.VMEM((H, L, DS + LANES), F32),
                            pltpu.VMEM((H, L, DV), F32)]),
        compiler_params=_cparams(("arbitrary", "arbitrary"), 40),
        name="mlstm_bidir",
    )(*tables, proj, proj, proj, gates_dir, b_gates_dir, S0, m0)


def _top2_of4(vals):
    m1, i1 = vals[0], jnp.zeros(vals[0].shape, I32)
    for j in range(1, 4):
        better = vals[j] > m1
        m1 = jnp.where(better, vals[j], m1)
        i1 = jnp.where(better, j, i1)
    m2, i2 = jnp.full(vals[0].shape, -jnp.inf, F32), jnp.zeros(vals[0].shape, I32)
    for j in range(4):
        cand = jnp.where(i1 == j, -jnp.inf, vals[j])
        better = cand > m2
        m2 = jnp.where(better, cand, m2)
        i2 = jnp.where(better, j, i2)
    return m1, i1, m2, i2


def _router_kernel(x_ref, g_ref, mod_ref, wr_ref, br_ref, h_ref, ids_ref, wts_ref, *, shift_idx, scale_idx):
    h = _modulated_norm(x_ref[...], g_ref[...], mod_ref[0], shift_idx, scale_idx)
    h_hi = h.astype(BF16)
    h_ref[...] = h_hi
    h_lo = (h - h_hi.astype(F32)).astype(BF16)
    w_hi = wr_ref[0]
    w_lo = wr_ref[1]
    logits = (lax.dot_general(w_hi, h_hi, _NT, preferred_element_type=F32)
              + lax.dot_general(w_lo, h_hi, _NT, preferred_element_type=F32)
              + lax.dot_general(w_hi, h_lo, _NT, preferred_element_type=F32))
    aff = jax.nn.sigmoid(logits)
    sel = aff + br_ref[...]
    aff_rows = [aff[e:e + 1, :] for e in range(N_EXPERTS)]
    sel_rows = [sel[e:e + 1, :] for e in range(N_EXPERTS)]
    tops = [_top2_of4(sel_rows[4 * gidx:4 * gidx + 4]) for gidx in range(N_GROUPS)]
    best = tops[0][0] + tops[0][2]
    grp = jnp.zeros(best.shape, I32)
    i1, i2 = tops[0][1], tops[0][3]
    for gidx in range(1, N_GROUPS):
        score = tops[gidx][0] + tops[gidx][2]
        better = score > best
        best = jnp.where(better, score, best)
        grp = jnp.where(better, gidx, grp)
        i1 = jnp.where(better, tops[gidx][1], i1)
        i2 = jnp.where(better, tops[gidx][3], i2)
    e1 = grp * EXPERTS_PER_GROUP + i1
    e2 = grp * EXPERTS_PER_GROUP + i2
    w1 = jnp.zeros(best.shape, F32)
    w2 = jnp.zeros(best.shape, F32)
    for e in range(N_EXPERTS):
        w1 = jnp.where(e1 == e, aff_rows[e], w1)
        w2 = jnp.where(e2 == e, aff_rows[e], w2)
    tot = w1 + w2
    ids_ref[...] = jnp.zeros(ids_ref.shape, I32)
    wts_ref[...] = jnp.zeros(wts_ref.shape, F32)
    ids_ref[0:1, :] = e1
    ids_ref[1:2, :] = e2
    wts_ref[0:1, :] = w1 / tot
    wts_ref[1:2, :] = w2 / tot


def router_weight_pair(w_router):
    w_t = w_router.T.astype(F32)
    w_hi = w_t.astype(BF16)
    return jnp.stack([w_hi, (w_t - w_hi.astype(F32)).astype(BF16)], axis=0)


def moe_router(x, row_off, n_rows, g, mod, w_router_t, b_router, shift_idx, scale_idx, tm=512):
    D = x.shape[1]
    off_b = row_off // tm
    per_seg = SEG_ROWS // tm
    return pl.pallas_call(
        functools.partial(_router_kernel, shift_idx=shift_idx, scale_idx=scale_idx),
        out_shape=(jax.ShapeDtypeStruct((n_rows, D), BF16),
                   jax.ShapeDtypeStruct((8, n_rows), I32),
                   jax.ShapeDtypeStruct((8, n_rows), F32)),
        grid=(n_rows // tm,),
        in_specs=[pl.BlockSpec((tm, D), lambda i: (i + off_b, 0)),
                  pl.BlockSpec((1, D), lambda i: (0, 0)),
                  pl.BlockSpec((1, 6, D), lambda i: ((i + off_b) // per_seg, 0, 0)),
                  pl.BlockSpec((2, N_EXPERTS, D), lambda i: (0, 0, 0)),
                  pl.BlockSpec((N_EXPERTS, 1), lambda i: (0, 0))],
        out_specs=(pl.BlockSpec((tm, D), lambda i: (i, 0)),
                   pl.BlockSpec((8, tm), lambda i: (0, i)),
                   pl.BlockSpec((8, tm), lambda i: (0, i))),
        compiler_params=_cparams(("parallel",), 40),
        name="moe_router",
    )(x, g.reshape(1, D), mod, w_router_t, b_router.reshape(N_EXPERTS, 1).astype(F32))


MOE_CAST_ROWS = 128


def _moe_expert_kernel(be_ref, first_ref, next_ref, nu_ref, x_ref, wg_hbm, wu_hbm, wd_hbm, y_ref,
                       wg_st, wu_st, wd_st, wg_bf, wu_bf, wd_bf, sems, *, e_off):
    i = pl.program_id(0)

    def weight_copies(e):
        return (pltpu.make_async_copy(wg_hbm.at[e_off + e], wg_st, sems.at[0]),
                pltpu.make_async_copy(wu_hbm.at[e_off + e], wu_st, sems.at[1]),
                pltpu.make_async_copy(wd_hbm.at[e_off + e], wd_st, sems.at[2]))

    @pl.when(i == 0)
    def _():
        for cp in weight_copies(be_ref[0]):
            cp.start()

    @pl.when(first_ref[i] == 1)
    def _():
        for cp in weight_copies(be_ref[i]):
            cp.wait()
        for st, bf in ((wg_st, wg_bf), (wu_st, wu_bf), (wd_st, wd_bf)):
            def cast_rows(c, carry, st=st, bf=bf):
                r = pl.multiple_of(c * MOE_CAST_ROWS, MOE_CAST_ROWS)
                bf[pl.ds(r, MOE_CAST_ROWS), :] = st[pl.ds(r, MOE_CAST_ROWS), :].astype(BF16)
                return carry
            lax.fori_loop(0, st.shape[0] // MOE_CAST_ROWS, cast_rows, 0)

        @pl.when(next_ref[i] >= 0)
        def _():
            for cp in weight_copies(next_ref[i]):
                cp.start()

    @pl.when(i < nu_ref[0])
    def _():
        x = x_ref[...]
        gate = jnp.dot(x, wg_bf[...], preferred_element_type=F32)
        up = jnp.dot(x, wu_bf[...], preferred_element_type=F32)
        act = (gate * jax.nn.sigmoid(gate) * up).astype(BF16)
        y_ref[...] = jnp.dot(act, wd_bf[...], preferred_element_type=F32).astype(y_ref.dtype)

    @pl.when(i >= nu_ref[0])
    def _():
        y_ref[...] = jnp.zeros(y_ref.shape, y_ref.dtype)


def moe_experts(xg, plan, wg, wu, wd, e_off):
    R, D = xg.shape
    F = wg.shape[2]
    nb = R // MOE_ROWS
    hbm = pl.BlockSpec(memory_space=pl.ANY)
    return pl.pallas_call(
        functools.partial(_moe_expert_kernel, e_off=e_off),
        out_shape=jax.ShapeDtypeStruct((R, D), BF16),
        grid_spec=pltpu.PrefetchScalarGridSpec(
            num_scalar_prefetch=4,
            grid=(nb,),
            in_specs=[pl.BlockSpec((MOE_ROWS, D), lambda i, *_: (i, 0)), hbm, hbm, hbm],
            out_specs=pl.BlockSpec((MOE_ROWS, D), lambda i, *_: (i, 0)),
            scratch_shapes=[pltpu.VMEM((D, F), F32), pltpu.VMEM((D, F), F32), pltpu.VMEM((F, D), F32),
                            pltpu.VMEM((D, F), BF16), pltpu.VMEM((D, F), BF16), pltpu.VMEM((F, D), BF16),
                            pltpu.SemaphoreType.DMA((3,))]),
        compiler_params=_cparams(("arbitrary",), 56),
        name="moe_experts",
    )(plan["block_e"], plan["first"], plan["next_e"], plan["n_used"], xg, wg, wu, wd)


def _combine_kernel(x_ref, mod_ref, *rest, gate_idx, final, tile_off, chunk_tiles):
    rest = list(rest)
    fg_ref = rest.pop(0) if final else None
    o_ref = rest.pop()
    t = pl.program_id(0) + tile_off
    for c, (lo, n) in enumerate(chunk_tiles):
        y0_ref, y1_ref, w_ref = rest[3 * c:3 * c + 3]

        @pl.when((t >= lo) & (t < lo + n))
        def _(y0_ref=y0_ref, y1_ref=y1_ref, w_ref=w_ref):
            y = w_ref[:, 0:1] * y0_ref[...].astype(F32) + w_ref[:, 1:2] * y1_ref[...].astype(F32)
            x = x_ref[...] + mod_ref[0, gate_idx:gate_idx + 1, :] * y
            if final:
                ms = jnp.mean(x * x, axis=-1, keepdims=True)
                x = x * lax.rsqrt(ms + EPS) * fg_ref[...]
            o_ref[...] = x


def moe_combine(x, row_off, n_rows, parts, mod, gate_idx, final_g=None, tm=512):
    D = x.shape[1]
    xb = row_off // tm
    per_seg = SEG_ROWS // tm
    final = final_g is not None
    in_specs = [pl.BlockSpec((tm, D), lambda i: (i + xb, 0)),
                pl.BlockSpec((1, 6, D), lambda i: ((i + xb) // per_seg, 0, 0))]
    args = [x, mod]
    if final:
        in_specs.append(pl.BlockSpec((1, D), lambda i: (0, 0)))
        args.append(final_g.reshape(1, D))
    chunk_tiles = []
    for c_off, c_n, y0, y1, wts in parts:
        lo, n = c_off // tm, c_n // tm
        chunk_tiles.append((lo, n))

        def chunk_map(i, lo=lo, n=n):
            return (jnp.clip(i + xb - lo, 0, n - 1), 0)

        in_specs += [pl.BlockSpec((tm, D), chunk_map), pl.BlockSpec((tm, D), chunk_map),
                     pl.BlockSpec((tm, 2), chunk_map)]
        args += [y0, y1, wts]
    return pl.pallas_call(
        functools.partial(_combine_kernel, gate_idx=gate_idx, final=final, tile_off=xb,
                          chunk_tiles=tuple(chunk_tiles)),
        out_shape=jax.ShapeDtypeStruct((n_rows, D), F32),
        grid=(n_rows // tm,),
        in_specs=in_specs,
        out_specs=pl.BlockSpec((tm, D), lambda i: (i, 0)),
        compiler_params=_cparams(("arbitrary",), 48),
        name="moe_combine",
    )(*args)


def moe_dispatch_plan(ids, n_tokens):
    n_assign = 2 * n_tokens
    n_blocks = n_assign // MOE_ROWS + N_EXPERTS
    experts = jnp.arange(N_EXPERTS, dtype=I32)
    flat_e = ids.T.reshape(n_assign)
    onehot = (flat_e[:, None] == experts[None, :]).astype(I32)
    csum = jnp.cumsum(onehot, axis=0)
    rank = jnp.sum((csum - onehot) * onehot, axis=1)
    counts = csum[-1]
    padded = (counts + MOE_ROWS - 1) // MOE_ROWS * MOE_ROWS
    pend = jnp.cumsum(padded)
    dest = (pend - padded)[flat_e] + rank
    n_used = (pend[-1] // MOE_ROWS).astype(I32)
    block_idx = jnp.arange(n_blocks, dtype=I32)
    block_e = jnp.minimum(jnp.sum((pend[None, :] <= block_idx[:, None] * MOE_ROWS).astype(I32), axis=1),
                          N_EXPERTS - 1)
    prev_e = jnp.concatenate([jnp.full((1,), -1, I32), block_e[:-1]])
    first = ((block_idx < n_used) & (block_e != prev_e)).astype(I32)
    later = (experts[None, :] > experts[:, None]) & (counts[None, :] > 0)
    next_of = jnp.min(jnp.where(later, experts[None, :], N_EXPERTS), axis=1)
    next_of = jnp.where(next_of == N_EXPERTS, -1, next_of)
    next_e = jnp.sum(jnp.where(block_e[:, None] == experts[None, :], next_of[None, :], 0), axis=1).astype(I32)
    src_tok = jnp.zeros((n_blocks * MOE_ROWS,), I32).at[dest].set(
        jnp.arange(n_assign, dtype=I32) // 2, unique_indices=True, mode="promise_in_bounds")
    plan = {"block_e": block_e, "first": first, "next_e": next_e, "n_used": n_used.reshape(1)}
    return dest.reshape(n_tokens, 2), src_tok, plan


def channel_mixer(x, g, mod, w_router_t, b_router, wg, wu, wd, e_off, chunks, outputs, final_g=None):
    h, ids8, wts8 = moe_router(x, 0, x.shape[0], g, mod, w_router_t, b_router, shift_idx=3, scale_idx=4)
    parts = []
    for c_off, c_n in chunks:
        dest, src_tok, plan = moe_dispatch_plan(ids8[:2, c_off:c_off + c_n], c_n)
        xg = h.at[src_tok + c_off].get(mode="promise_in_bounds")
        ybuf = moe_experts(xg, plan, wg, wu, wd, e_off)
        y0 = ybuf.at[dest[:, 0]].get(mode="promise_in_bounds")
        y1 = ybuf.at[dest[:, 1]].get(mode="promise_in_bounds")
        parts.append((c_off, c_n, y0, y1, wts8[:2, c_off:c_off + c_n].T))
    return [moe_combine(x, o_off, o_n, parts, mod, 5, final_g) for o_off, o_n in outputs]


def kernel(x_prompt, x_sample, c, cache_a_k, cache_a_v, cache_b_k, cache_b_v, state_C, state_n, state_m,
           c_ctx, norm1_g, norm2_g, w_ada, b_ada, w_in_ab, w_out_ab, rel_bias_a, sink_b, w_in_c, b_gates_c,
           norm_c_g, w_out_c, w_router, b_router, w_gate_e, w_up_e, w_down_e, final_norm_g):
    D = D_MODEL
    Np, Nl = BATCH * SEQ, DEC_BATCH * DEC_SEQ
    N = Np + Nl
    H = C_HEADS
    x = None
    x_ctx, x_lat = x_prompt.reshape(Np, D), x_sample.reshape(Nl, D)
    moe_chunks = [(0, N)]
    n_e = w_gate_e.shape[1]
    wg_all = w_gate_e.reshape(DEPTH * n_e, D, D_FF_EXPERT)
    wu_all = w_up_e.reshape(DEPTH * n_e, D, D_FF_EXPERT)
    wd_all = w_down_e.reshape(DEPTH * n_e, D_FF_EXPERT, D)

    cvec = jnp.concatenate([c_ctx[None, :], c, jnp.zeros((8 - 1 - DEC_BATCH, D), F32)], axis=0)
    mod_all = ada_mod_all(cvec, w_ada, b_ada).reshape(DEPTH, 8, 6, D)
    w_router_t = router_weight_pair(w_router)

    outs = {}
    for l in range(DEPTH):
        mod = mod_all[l]
        j = l // 2
        if l % 2 == 0:
            w_in = w_in_ab[j]
            if x is not None:
                x_ctx, x_lat = x[:Np], x[Np:]
            proj_ctx = norm_mod_matmul(x_ctx, 0, Np, norm1_g[l], mod, w_in, F32, 0, 1, seg_row_off=0,
                                       precise=True, tm=512)
            proj_lat = norm_mod_matmul(x_lat, 0, Nl, norm1_g[l], mod, w_in, BF16, 0, 1, seg_row_off=Np, tn=768)
            o_ctx = ctx_attention(proj_ctx, sink_b[j], BATCH, SEQ)
            bias_mask = nat_bias_mask(rel_bias_a[j], DEC_SEQ // GRID_W)
            oa = nat_attention(proj_lat, cache_a_k[:, j].reshape(DEC_BATCH, PAST_LEN, A_WIDTH).astype(BF16),
                               cache_a_v[:, j].reshape(DEC_BATCH, PAST_LEN, A_WIDTH).astype(BF16), bias_mask,
                               DEC_BATCH, DEC_SEQ, A_HEADS)
            cos, sin = rope_tables(DEC_SEQ)
            qk_rot = rope_heads(proj_lat, 3 * A_WIDTH, 3 * A_WIDTH + B_Q_WIDTH, cos, sin, DEC_SEQ)
            ob = swa_attention(qk_rot, proj_lat, 3 * A_HEADS + B_HEADS + B_KV_HEADS,
                               cache_b_k[:, j].reshape(DEC_BATCH, PAST_LEN, B_KV_WIDTH).astype(BF16),
                               cache_b_v[:, j].reshape(DEC_BATCH, PAST_LEN, B_KV_WIDTH).astype(BF16), sink_b[j],
                               DEC_BATCH, DEC_SEQ)
            x = attn_out_residual(o_ctx, oa, ob, jnp.stack(_hi_lo(w_out_ab[j]), axis=0), x_ctx, x_lat, mod,
                                  gate_idx=2)
            kv = proj_ctx[:, A_WIDTH:]
            outs.setdefault("ak", []).append(kv[:, :A_WIDTH].reshape(BATCH, SEQ, A_HEADS, HEAD_DIM))
            outs.setdefault("av", []).append(kv[:, A_WIDTH:2 * A_WIDTH].reshape(BATCH, SEQ, A_HEADS, HEAD_DIM))
            kvb = proj_ctx[:, 3 * A_WIDTH + B_Q_WIDTH:]
            outs.setdefault("bk", []).append(kvb[:, :B_KV_WIDTH].reshape(BATCH, SEQ, B_KV_HEADS, HEAD_DIM))
            outs.setdefault("bv", []).append(kvb[:, B_KV_WIDTH:].reshape(BATCH, SEQ, B_KV_HEADS, HEAD_DIM))
        else:
            n_main = 2 * C_QK_WIDTH + 2 * C_V_WIDTH
            w_main = w_in_c[j]
            w_gates = jnp.pad(w_in_c[j][:, n_main:], ((0, 0), (0, LANES - 4 * H)))
            proj, gates = norm_mod_matmul(x, 0, N, norm1_g[l], mod, w_main, BF16, 0, 1, w_aux=w_gates,
                                          n_out=n_main, tn=1024)
            gates_dir = jnp.pad(gates[:, :4 * H].reshape(N, 2, 2, H).transpose(1, 2, 0, 3),
                                ((0, 0), (0, 0), (0, 0), (0, LANES - H)))
            bg = jnp.pad(b_gates_c[j].astype(F32).reshape(2, 2, 1, H), ((0, 0), (0, 0), (0, 0), (0, LANES - H)))
            S = BATCH + DEC_BATCH
            C0 = jnp.concatenate([jnp.zeros((BATCH, 2, H, C_QK_DIM, C_V_DIM), F32), state_C[:, j]], axis=0)
            n0 = jnp.concatenate([jnp.zeros((BATCH, 2, H, C_QK_DIM), F32), state_n[:, j]], axis=0)
            m0 = jnp.concatenate([jnp.zeros((BATCH, 2, H), F32), state_m[:, j]], axis=0)
            S0 = jnp.concatenate([C0, jnp.broadcast_to(n0[..., None], n0.shape + (LANES,))], axis=-1)
            m0 = jnp.broadcast_to(m0[..., None], (S, 2, H, LANES))
            seq_chunks = [SEQ // C_CHUNK] * BATCH + [DEC_SEQ // C_CHUNK] * DEC_BATCH
            h_dir, Sf, mf = mlstm_bidir(proj, gates_dir, bg, S0, m0, seq_chunks)
            x = mlstm_out_residual(h_dir, proj, 2, norm_c_g[j], w_out_c[j].astype(BF16), x, mod, gate_idx=2)
            outs.setdefault("C", []).append(Sf[:BATCH, ..., :C_V_DIM])
            outs.setdefault("n", []).append(Sf[:BATCH, ..., C_V_DIM])
            outs.setdefault("m", []).append(mf[:BATCH, :, :, 0])
        last = l == DEPTH - 1
        pieces = channel_mixer(x, norm2_g[l], mod, w_router_t, b_router, wg_all, wu_all, wd_all, l * n_e,
                               moe_chunks, [(0, Np), (Np, Nl)] if last else [(0, N)],
                               final_norm_g if last else None)
        if last:
            y_prompt = pieces[0].reshape(BATCH, SEQ, D)
            y_sample = pieces[1].reshape(DEC_BATCH, DEC_SEQ, D)
        else:
            x = pieces[0]

    return (y_prompt, y_sample,
            jnp.stack(outs["ak"], axis=1), jnp.stack(outs["av"], axis=1),
            jnp.stack(outs["bk"], axis=1), jnp.stack(outs["bv"], axis=1),
            jnp.stack(outs["C"], axis=1), jnp.stack(outs["n"], axis=1), jnp.stack(outs["m"], axis=1))
```

```python
import functools

import numpy as np
import jax
import jax.numpy as jnp
from jax import lax
from jax.experimental import pallas as pl
from jax.experimental.pallas import tpu as pltpu

F32 = jnp.float32
BF16 = jnp.bfloat16
I32 = jnp.int32

D_MODEL = 2048
BATCH = 16
SEQ = 256
DEPTH = 2
DEC_BATCH = 4
DEC_SEQ = 4096
PAST_LEN = 512
GRID_W = 64
HEAD_DIM = 128
A_HEADS = 8
NA_ROWS = 8
NA_COLS = 16
B_HEADS = 8
B_KV_HEADS = 2
B_WINDOW = 128
ROPE_THETA = 10000.0
C_HEADS = 8
C_QK_DIM = 128
C_V_DIM = 256
C_CHUNK = 128
N_EXPERTS = 16
N_GROUPS = 4
EXPERTS_PER_GROUP = N_EXPERTS // N_GROUPS
D_FF_EXPERT = 1024
EPS = 1e-6

A_WIDTH = A_HEADS * HEAD_DIM
B_Q_WIDTH = B_HEADS * HEAD_DIM
B_KV_WIDTH = B_KV_HEADS * HEAD_DIM
C_QK_WIDTH = C_HEADS * C_QK_DIM
C_V_WIDTH = C_HEADS * C_V_DIM

SEG_ROWS = 4096
NEG_BIG = -1e30
MOE_ROWS = 256
NORM_ROWS = 256
MIB = 1024 * 1024
LANES = 128

_NT = (((1,), (1,)), ((), ()))


def _cparams(sem, vmem_mib):
    return pltpu.CompilerParams(dimension_semantics=sem, vmem_limit_bytes=vmem_mib * MIB)


def _modulated_norm(x, g, mod, shift_idx, scale_idx):
    ms = jnp.mean(x * x, axis=-1, keepdims=True)
    y = x * lax.rsqrt(ms + EPS) * g
    return y * (1.0 + mod[scale_idx:scale_idx + 1, :]) + mod[shift_idx:shift_idx + 1, :]


def _hi_lo(x):
    hi = x.astype(BF16)
    return hi, (x - hi.astype(F32)).astype(BF16)


def _dot3(a, b, dims=None):
    a_hi, a_lo = _hi_lo(a)
    b_hi, b_lo = _hi_lo(b)
    dims = (((a.ndim - 1,), (0,)), ((), ())) if dims is None else dims
    dot = functools.partial(lax.dot_general, dimension_numbers=dims, preferred_element_type=F32)
    return dot(a_hi, b_hi) + dot(a_lo, b_hi) + dot(a_hi, b_lo)


def _ada_kernel(c_ref, w_ref, b_ref, o_ref):
    c = c_ref[...]
    o_ref[0] = _dot3(c * jax.nn.sigmoid(c), w_ref[0]) + b_ref[0]


def ada_mod_all(cvec8, w_ada, b_ada, tn=1024):
    L, D, D6 = w_ada.shape
    return pl.pallas_call(
        _ada_kernel,
        out_shape=jax.ShapeDtypeStruct((L, 8, D6), F32),
        grid=(L, D6 // tn),
        in_specs=[pl.BlockSpec((8, D), lambda l, j: (0, 0)),
                  pl.BlockSpec((1, D, tn), lambda l, j: (l, 0, j)),
                  pl.BlockSpec((1, 1, tn), lambda l, j: (l, 0, j))],
        out_specs=pl.BlockSpec((1, 8, tn), lambda l, j: (l, 0, j)),
        compiler_params=_cparams(("parallel", "parallel"), 40),
        name="ada_mod",
    )(cvec8, w_ada, b_ada.reshape(L, 1, D6))


def _nmm_kernel(x_ref, g_ref, mod_ref, w_ref, *rest, shift_idx, scale_idx, has_aux, precise):
    rest = list(rest)
    waux_ref = rest.pop(0) if has_aux else None
    o_ref = rest.pop(0)
    oaux_ref = rest.pop(0) if has_aux else None
    h_scr = rest.pop(0)
    hlo_scr = rest.pop(0) if precise else None

    @pl.when(pl.program_id(1) == 0)
    def _():
        for r in range(x_ref.shape[0] // NORM_ROWS):
            rows = pl.ds(r * NORM_ROWS, NORM_ROWS)
            hf = _modulated_norm(x_ref[rows, :], g_ref[...], mod_ref[0], shift_idx, scale_idx)
            h = hf.astype(BF16)
            h_scr[rows, :] = h
            if precise:
                hlo_scr[rows, :] = (hf - h.astype(F32)).astype(BF16)
            if has_aux:
                oaux_ref[rows, :] = jnp.dot(h, waux_ref[...].astype(BF16), preferred_element_type=F32)

    if precise:
        w_hi, w_lo = _hi_lo(w_ref[...])
        acc = (jnp.dot(h_scr[...], w_hi, preferred_element_type=F32)
               + jnp.dot(hlo_scr[...], w_hi, preferred_element_type=F32)
               + jnp.dot(h_scr[...], w_lo, preferred_element_type=F32))
    else:
        acc = jnp.dot(h_scr[...], w_ref[...].astype(BF16), preferred_element_type=F32)
    o_ref[...] = acc.astype(o_ref.dtype)


def norm_mod_matmul(x, row_off, n_rows, g, mod, w, out_dtype, shift_idx, scale_idx, w_aux=None, n_out=None,
                    seg_row_off=None, precise=False, tm=1024, tn=512):
    D = x.shape[1]
    n_out = w.shape[1] if n_out is None else n_out
    off_b = row_off // tm
    seg_b = off_b if seg_row_off is None else seg_row_off // tm
    per_seg = SEG_ROWS // tm
    has_aux = w_aux is not None
    in_specs = [pl.BlockSpec((tm, D), lambda i, j: (i + off_b, 0)),
                pl.BlockSpec((1, D), lambda i, j: (0, 0)),
                pl.BlockSpec((1, 6, D), lambda i, j: ((i + seg_b) // per_seg, 0, 0)),
                pl.BlockSpec((D, tn), lambda i, j: (0, j))]
    out_shape = jax.ShapeDtypeStruct((n_rows, n_out), out_dtype)
    out_specs = pl.BlockSpec((tm, tn), lambda i, j: (i, j))
    args = [x, g.reshape(1, D), mod, w]
    if has_aux:
        n_aux = w_aux.shape[1]
        in_specs.append(pl.BlockSpec((D, n_aux), lambda i, j: (0, 0)))
        out_shape = (out_shape, jax.ShapeDtypeStruct((n_rows, n_aux), F32))
        out_specs = (out_specs, pl.BlockSpec((tm, n_aux), lambda i, j: (i, 0)))
        args.append(w_aux)
    return pl.pallas_call(
        functools.partial(_nmm_kernel, shift_idx=shift_idx, scale_idx=scale_idx, has_aux=has_aux,
                          precise=precise),
        out_shape=out_shape,
        grid=(n_rows // tm, n_out // tn),
        in_specs=in_specs,
        out_specs=out_specs,
        scratch_shapes=[pltpu.VMEM((tm, D), BF16)] * (2 if precise else 1),
        compiler_params=_cparams(("parallel", "arbitrary"), 56),
        name="norm_mod_matmul",
    )(*args)


def _attn_out_kernel(oc_ref, oa_ref, ob_ref, w_ref, xc_ref, xl_ref, mod_ref, o_ref, *, gate_idx, n_ctx_tiles):
    i = pl.program_id(0)
    gate = mod_ref[0, gate_idx:gate_idx + 1, :]

    @pl.when(i < n_ctx_tiles)
    def _():
        oc_hi, oc_lo = _hi_lo(oc_ref[...])
        acc = (jnp.dot(oc_hi, w_ref[0], preferred_element_type=F32)
               + jnp.dot(oc_lo, w_ref[0], preferred_element_type=F32)
               + jnp.dot(oc_hi, w_ref[1], preferred_element_type=F32))
        o_ref[...] = xc_ref[...] + gate * acc

    @pl.when(i >= n_ctx_tiles)
    def _():
        a = jnp.concatenate([oa_ref[...], ob_ref[...]], axis=-1)
        acc = jnp.dot(a, w_ref[0], preferred_element_type=F32)
        o_ref[...] = xl_ref[...] + gate * acc


def attn_out_residual(o_ctx, oa, ob, w, x_ctx, x_lat, mod, gate_idx, tm=256):
    n_ctx, K = o_ctx.shape
    n_lat = oa.shape[0]
    D = w.shape[2]
    nct = n_ctx // tm
    per_seg = SEG_ROWS // tm

    def ctx_map(i):
        return (jnp.minimum(i, nct - 1), 0)

    def lat_map(i):
        return (jnp.maximum(i - nct, 0), 0)

    return pl.pallas_call(
        functools.partial(_attn_out_kernel, gate_idx=gate_idx, n_ctx_tiles=nct),
        out_shape=jax.ShapeDtypeStruct((n_ctx + n_lat, D), F32),
        grid=((n_ctx + n_lat) // tm,),
        in_specs=[pl.BlockSpec((tm, K), ctx_map),
                  pl.BlockSpec((tm, oa.shape[1]), lat_map),
                  pl.BlockSpec((tm, ob.shape[1]), lat_map),
                  pl.BlockSpec((2, K, D), lambda i: (0, 0, 0), pipeline_mode=pl.Buffered(1)),
                  pl.BlockSpec((tm, D), ctx_map),
                  pl.BlockSpec((tm, D), lat_map),
                  pl.BlockSpec((1, 6, D), lambda i: (i // per_seg, 0, 0))],
        out_specs=pl.BlockSpec((tm, D), lambda i: (i, 0)),
        compiler_params=_cparams(("arbitrary",), 48),
        name="attn_out_residual",
    )(o_ctx, oa, ob, w, x_ctx, x_lat, mod)


def _mlstm_out_kernel(hf_ref, hb_ref, o_ref, ng_ref, w_ref, x_ref, mod_ref, out_ref, *, gate_idx):
    hs = hf_ref[...] + hb_ref[...]
    parts = []
    for h in range(C_HEADS):
        sl = slice(h * C_V_DIM, (h + 1) * C_V_DIM)
        xs = hs[:, sl]
        ms = jnp.mean(xs * xs, axis=-1, keepdims=True)
        hn = xs * lax.rsqrt(ms + EPS) * ng_ref[:, sl]
        parts.append((jax.nn.sigmoid(o_ref[:, sl].astype(F32)) * hn).astype(BF16))
    a = jnp.concatenate(parts, axis=-1)
    acc = jnp.dot(a, w_ref[...], preferred_element_type=F32)
    out_ref[...] = x_ref[...] + mod_ref[0, gate_idx:gate_idx + 1, :] * acc


def mlstm_out_residual(h_dir, proj, o_col_block, norm_g, w, x, mod, gate_idx, tm=256):
    n, D = x.shape
    V = C_V_WIDTH
    per_seg = SEG_ROWS // tm
    return pl.pallas_call(
        functools.partial(_mlstm_out_kernel, gate_idx=gate_idx),
        out_shape=jax.ShapeDtypeStruct((n, D), F32),
        grid=(n // tm,),
        in_specs=[pl.BlockSpec((None, tm, V), lambda i: (0, i, 0)),
                  pl.BlockSpec((None, tm, V), lambda i: (1, i, 0)),
                  pl.BlockSpec((tm, V), lambda i: (i, o_col_block)),
                  pl.BlockSpec((1, V), lambda i: (0, 0)),
                  pl.BlockSpec((V, D), lambda i: (0, 0)),
                  pl.BlockSpec((tm, D), lambda i: (i, 0)),
                  pl.BlockSpec((1, 6, D), lambda i: (i // per_seg, 0, 0))],
        out_specs=pl.BlockSpec((tm, D), lambda i: (i, 0)),
        compiler_params=_cparams(("parallel",), 48),
        name="mlstm_out_residual",
    )(h_dir, h_dir, proj, norm_g.reshape(1, V), w, x, mod)


def _ctx_attn_kernel(sink_ref, q_ref, k_ref, v_ref, o_ref):
    h = pl.program_id(1)
    s = _dot3(q_ref[...], k_ref[...], _NT) * (HEAD_DIM ** -0.5)
    sk = sink_ref[h]
    m = jnp.maximum(jnp.max(s, axis=-1, keepdims=True), sk)
    p = jnp.exp(s - m)
    l = jnp.sum(p, axis=-1, keepdims=True) + jnp.exp(sk - m)
    o_ref[...] = (_dot3(p, v_ref[...]) / l).astype(o_ref.dtype)


def ctx_attention(proj, sink_b, n_batch, seq):
    n_heads = A_HEADS + B_HEADS
    group = B_HEADS // B_KV_HEADS
    qb0 = 3 * A_HEADS
    kb0 = qb0 + B_HEADS
    vb0 = kb0 + B_KV_HEADS
    sinks = jnp.concatenate([jnp.full((A_HEADS,), NEG_BIG, F32), sink_b.astype(F32)])

    def q_map(b, h, s):
        return (b, jnp.where(h < A_HEADS, h, qb0 + h - A_HEADS))

    def k_map(b, h, s):
        return (b, jnp.where(h < A_HEADS, A_HEADS + h, kb0 + (h - A_HEADS) // group))

    def v_map(b, h, s):
        return (b, jnp.where(h < A_HEADS, 2 * A_HEADS + h, vb0 + (h - A_HEADS) // group))

    blk = (seq, HEAD_DIM)
    return pl.pallas_call(
        _ctx_attn_kernel,
        out_shape=jax.ShapeDtypeStruct((n_batch * seq, n_heads * HEAD_DIM), F32),
        grid_spec=pltpu.PrefetchScalarGridSpec(
            num_scalar_prefetch=1,
            grid=(n_batch, n_heads),
            in_specs=[pl.BlockSpec(blk, q_map), pl.BlockSpec(blk, k_map), pl.BlockSpec(blk, v_map)],
            out_specs=pl.BlockSpec(blk, lambda b, h, s: (b, h))),
        compiler_params=_cparams(("parallel", "parallel"), 32),
        name="ctx_attention",
    )(sinks, proj, proj, proj)


NAT_QROWS = 4


def nat_bias_mask(rel_bias, rows):
    W = GRID_W
    nb = rows // NAT_QROWS
    kh = min(NA_ROWS, rows)
    n_dr, n_dc = 2 * NA_ROWS - 1, 2 * NA_COLS - 1
    H = rel_bias.shape[0]
    cidx = np.clip(np.arange(W)[None, :] - np.arange(W)[:, None] + NA_COLS - 1, 0, n_dc - 1)
    onehot = jnp.asarray((cidx.reshape(1, W * W) == np.arange(n_dc)[:, None]).astype(np.float32))
    col_bias = jnp.dot(rel_bias.astype(F32).reshape(H * n_dr, n_dc), onehot,
                       precision=lax.Precision.HIGHEST).reshape(H, n_dr, W, W)
    q_rows = []
    for qi in range(NAT_QROWS):
        tiles = [col_bias[:, int(np.clip(NAT_QROWS * (kj - 1) + kjr - qi + NA_ROWS - 1, 0, n_dr - 1))]
                 for kj in range(3) for kjr in range(NAT_QROWS)]
        q_rows.append(jnp.concatenate(tiles, axis=-1))
    bias = jnp.concatenate(q_rows, axis=1)
    variants = []
    for g in (0, 1, nb - 1):
        i = np.arange(NAT_QROWS)[:, None, None, None, None]
        qc = np.arange(W)[None, :, None, None, None]
        j = np.arange(3)[None, None, :, None, None]
        jr = np.arange(NAT_QROWS)[None, None, None, :, None]
        kc = np.arange(W)[None, None, None, None, :]
        r = NAT_QROWS * g + i
        kblk = g - 1 + j
        kr = NAT_QROWS * kblk + jr
        rs = np.clip(r - kh // 2, 0, rows - kh)
        row_ok = (kblk >= 0) & (kblk < nb) & (kr >= rs) & (kr < rs + kh)
        cstart = np.clip(qc - NA_COLS // 2, 0, W - NA_COLS)
        col_ok = (kc >= cstart) & (kc < cstart + NA_COLS)
        ok = np.broadcast_to(row_ok & col_ok, (NAT_QROWS, W, 3, NAT_QROWS, W))
        n_q, n_k = NAT_QROWS * W, 3 * NAT_QROWS * W
        variants.append(jnp.where(jnp.asarray(ok.reshape(1, n_q, n_k)), bias, NEG_BIG))
    return jnp.stack(variants, axis=0)


NAT_HEADS_PER_STEP = 8


def _nat_kernel(q_ref, k0_ref, k1_ref, k2_ref, v0_ref, v1_ref, v2_ref, kc_ref, vc_ref, bm_ref, o_ref):
    scale = HEAD_DIM ** -0.5
    tq = q_ref.shape[0]
    for h in range(q_ref.shape[1] // HEAD_DIM):
        cs = slice(h * HEAD_DIM, (h + 1) * HEAD_DIM)
        q = q_ref[:, cs]
        s_lat = [lax.dot_general(q, k_ref[:, cs], _NT, preferred_element_type=F32) * scale
                 + bm_ref[0, h, :, j * tq:(j + 1) * tq]
                 for j, k_ref in enumerate((k0_ref, k1_ref, k2_ref))]
        s_ctx = lax.dot_general(q, kc_ref[0, :, cs], _NT, preferred_element_type=F32) * scale
        m = jnp.max(s_ctx, axis=-1, keepdims=True)
        for s in s_lat:
            m = jnp.maximum(m, jnp.max(s, axis=-1, keepdims=True))
        p_ctx = jnp.exp(s_ctx - m)
        l = jnp.sum(p_ctx, axis=-1, keepdims=True)
        acc = jnp.dot(p_ctx.astype(BF16), vc_ref[0, :, cs], preferred_element_type=F32)
        for s, v_ref in zip(s_lat, (v0_ref, v1_ref, v2_ref)):
            p = jnp.exp(s - m)
            l = l + jnp.sum(p, axis=-1, keepdims=True)
            acc = acc + jnp.dot(p.astype(BF16), v_ref[:, cs], preferred_element_type=F32)
        o_ref[:, cs] = (acc / l).astype(o_ref.dtype)


def nat_attention(proj, cache_k, cache_v, bias_mask, n_batch, T, n_heads):
    tq = NAT_QROWS * GRID_W
    nb = T // tq
    P = cache_k.shape[1]
    hs = min(NAT_HEADS_PER_STEP, n_heads)
    ng = n_heads // hs

    def kv_map(col0, j):
        return lambda b, h, g: (b * nb + jnp.clip(g - 1 + j, 0, nb - 1), col0 + h)

    blk = (tq, hs * HEAD_DIM)
    in_specs = [pl.BlockSpec(blk, lambda b, h, g: (b * nb + g, h))]
    in_specs += [pl.BlockSpec(blk, kv_map(ng, j)) for j in range(3)]
    in_specs += [pl.BlockSpec(blk, kv_map(2 * ng, j)) for j in range(3)]
    in_specs += [pl.BlockSpec((1, P, hs * HEAD_DIM), lambda b, h, g: (b, 0, h))] * 2
    in_specs += [pl.BlockSpec((1, hs, tq, 3 * tq),
                              lambda b, h, g: (jnp.where(g == 0, 0, jnp.where(g == nb - 1, 2, 1)), h, 0, 0))]
    return pl.pallas_call(
        _nat_kernel,
        out_shape=jax.ShapeDtypeStruct((n_batch * T, n_heads * HEAD_DIM), BF16),
        grid=(n_batch, ng, nb),
        in_specs=in_specs,
        out_specs=pl.BlockSpec(blk, lambda b, h, g: (b * nb + g, h)),
        compiler_params=_cparams(("parallel", "parallel", "arbitrary"), 40),
        name="nat_attention",
    )(proj, proj, proj, proj, proj, proj, proj, cache_k, cache_v, bias_mask)


def rope_tables(T):
    t = jnp.arange(T)
    row = (t // GRID_W).astype(F32)
    col = (t % GRID_W).astype(F32)
    nf = HEAD_DIM // 4
    freqs = ROPE_THETA ** (-jnp.arange(nf, dtype=F32) / nf)
    ar = row[:, None] * freqs
    ac = col[:, None] * freqs
    cos = jnp.concatenate([jnp.cos(ar), jnp.cos(ar), jnp.cos(ac), jnp.cos(ac)], axis=-1)
    sin = jnp.concatenate([-jnp.sin(ar), jnp.sin(ar), -jnp.sin(ac), jnp.sin(ac)], axis=-1)
    return cos, sin


def _rope_kernel(q_ref, k_ref, cos_ref, sin_ref, o_ref):
    nf = HEAD_DIM // 4
    cos = cos_ref[...]
    sin = sin_ref[...]
    lane = lax.broadcasted_iota(I32, cos.shape, 1)
    first_half = (lane & nf) == 0
    col = 0
    for src, gain in ((q_ref, HEAD_DIM ** -0.5), (k_ref, None)):
        c, s = (cos, sin) if gain is None else (cos * gain, sin * gain)
        for h in range(src.shape[1] // HEAD_DIM):
            x = src[:, h * HEAD_DIM:(h + 1) * HEAD_DIM].astype(F32)
            upper = pltpu.roll(x, HEAD_DIM - nf, 1)
            lower = pltpu.roll(x, nf, 1)
            partner = jnp.where(first_half, upper, lower)
            o_ref[:, col:col + HEAD_DIM] = (x * c + partner * s).astype(o_ref.dtype)
            col += HEAD_DIM


def rope_heads(proj, q_col0, k_col0, cos, sin, T, tm=512):
    n = proj.shape[0]
    per_seq = T // tm
    return pl.pallas_call(
        _rope_kernel,
        out_shape=jax.ShapeDtypeStruct((n, B_Q_WIDTH + B_KV_WIDTH), BF16),
        grid=(n // tm,),
        in_specs=[pl.BlockSpec((tm, B_Q_WIDTH), lambda i: (i, q_col0 // B_Q_WIDTH)),
                  pl.BlockSpec((tm, B_KV_WIDTH), lambda i: (i, k_col0 // B_KV_WIDTH)),
                  pl.BlockSpec((tm, HEAD_DIM), lambda i: (i % per_seq, 0)),
                  pl.BlockSpec((tm, HEAD_DIM), lambda i: (i % per_seq, 0))],
        out_specs=pl.BlockSpec((tm, B_Q_WIDTH + B_KV_WIDTH), lambda i: (i, 0)),
        compiler_params=_cparams(("parallel",), 32),
        name="rope_heads",
    )(proj, proj, cos, sin)


SWA_TQ = 2 * B_WINDOW


def _swa_kernel(sink_ref, q_ref, k0_ref, k1_ref, k2_ref, k3_ref, v0_ref, v1_ref, v2_ref, v3_ref,
                kc_ref, vc_ref, o_ref, *, T):
    group = B_HEADS // B_KV_HEADS
    kvh = pl.program_id(1)
    n = pl.program_id(2)
    k = jnp.concatenate([k0_ref[...], k1_ref[...], k2_ref[...], k3_ref[...]], axis=0)
    v = jnp.concatenate([v0_ref[...], v1_ref[...], v2_ref[...], v3_ref[...]], axis=0)
    kc = kc_ref[0]
    vc = vc_ref[0]
    nk = k.shape[0]
    qpos = n * SWA_TQ + lax.broadcasted_iota(I32, (SWA_TQ, nk), 0)
    kpos = n * SWA_TQ - B_WINDOW + lax.broadcasted_iota(I32, (SWA_TQ, nk), 1)
    dist = jnp.abs(qpos - kpos)
    ok = jnp.where(kpos >= 0, jnp.where(kpos < T, dist, B_WINDOW + 1), B_WINDOW + 1) <= B_WINDOW
    outs = []
    for gi in range(group):
        q = q_ref[:, gi * HEAD_DIM:(gi + 1) * HEAD_DIM]
        s_lat = jnp.where(ok, lax.dot_general(q, k, _NT, preferred_element_type=F32), NEG_BIG)
        s_ctx = lax.dot_general(q, kc, _NT, preferred_element_type=F32)
        sk = sink_ref[kvh * group + gi]
        m = jnp.maximum(jnp.maximum(jnp.max(s_lat, axis=-1, keepdims=True),
                                    jnp.max(s_ctx, axis=-1, keepdims=True)), sk)
        p_lat = jnp.exp(s_lat - m)
        p_ctx = jnp.exp(s_ctx - m)
        l = (jnp.sum(p_lat, axis=-1, keepdims=True) + jnp.sum(p_ctx, axis=-1, keepdims=True)
             + jnp.exp(sk - m))
        acc = (jnp.dot(p_lat.astype(BF16), v, preferred_element_type=F32)
               + jnp.dot(p_ctx.astype(BF16), vc, preferred_element_type=F32))
        outs.append((acc / l).astype(o_ref.dtype))
    o_ref[...] = jnp.concatenate(outs, axis=-1)


def swa_attention(qk_rot, proj, v_col0, cache_k, cache_v, sink, n_batch, T):
    group = B_HEADS // B_KV_HEADS
    nq = T // SWA_TQ
    nkb = T // B_WINDOW
    P = cache_k.shape[1]

    def kv_map(col0, j):
        return lambda b, kvh, n, s: (b * nkb + jnp.clip(2 * n - 1 + j, 0, nkb - 1), col0 + kvh)

    kblk = (B_WINDOW, HEAD_DIM)
    in_specs = [pl.BlockSpec((SWA_TQ, group * HEAD_DIM), lambda b, kvh, n, s: (b * nq + n, kvh))]
    in_specs += [pl.BlockSpec(kblk, kv_map(B_HEADS, j)) for j in range(4)]
    in_specs += [pl.BlockSpec(kblk, kv_map(v_col0, j)) for j in range(4)]
    in_specs += [pl.BlockSpec((1, P, HEAD_DIM), lambda b, kvh, n, s: (b, 0, kvh))] * 2
    return pl.pallas_call(
        functools.partial(_swa_kernel, T=T),
        out_shape=jax.ShapeDtypeStruct((n_batch * T, B_Q_WIDTH), BF16),
        grid_spec=pltpu.PrefetchScalarGridSpec(
            num_scalar_prefetch=1,
            grid=(n_batch, B_KV_HEADS, nq),
            in_specs=in_specs,
            out_specs=pl.BlockSpec((SWA_TQ, group * HEAD_DIM), lambda b, kvh, n, s: (b * nq + n, kvh))),
        compiler_params=_cparams(("parallel", "parallel", "arbitrary"), 32),
        name="swa_attention",
    )(sink.astype(F32), qk_rot, qk_rot, qk_rot, qk_rot, qk_rot, proj, proj, proj, proj, cache_k, cache_v)


def _mlstm_kernel(rowblk_ref, seq_ref, first_ref, last_ref, *refs):
    s = pl.program_id(1)
    for d in range(2):
        @pl.when(pl.program_id(0) == d)
        def _(d=d):
            _mlstm_chunk(d, first_ref[s] == 1, last_ref[s] == 1, *refs)


def _mlstm_chunk(d, is_first, is_last, q_ref, k_ref, v_ref, g_ref, bg_ref, S0_ref, m0_ref,
                 h_ref, Sf_ref, mf_ref, S_scr, m_scr, qk_scr, inter_scr, num_scr, kw_scr):
    H, DK, DV, L, R = C_HEADS, C_QK_DIM, C_V_DIM, C_CHUNK, LANES
    scale = DK ** -0.5

    @pl.when(is_first)
    def _():
        S_scr[...] = S0_ref[0, 0]
        m_scr[...] = m0_ref[0, 0]

    gi = g_ref[0, 0] + bg_ref[0, 0]
    lf = jax.nn.log_sigmoid(g_ref[0, 1] + bg_ref[0, 1])
    row = lax.broadcasted_iota(I32, (L, L), 0)
    col = lax.broadcasted_iota(I32, (L, L), 1)
    causal = col <= row if d == 0 else col >= row
    b_all = jnp.dot(causal.astype(F32), lf, precision=lax.Precision.HIGHEST,
                    preferred_element_type=F32)
    a_row = (gi - b_all).T[0:H, :]
    b_row = b_all.T[0:H, :]
    lane = lax.broadcasted_iota(I32, (H, L), 1)
    cm = a_row
    k = 1
    while k < L:
        if d == 0:
            shifted = jnp.where(lane >= k, pltpu.roll(cm, k, 1), -jnp.inf)
        else:
            shifted = jnp.where(lane < L - k, pltpu.roll(cm, L - k, 1), -jnp.inf)
        cm = jnp.maximum(cm, shifted)
        k *= 2
    end = L - 1 if d == 0 else 0
    m_all = m_scr[...]
    M_row = jnp.maximum(m_all, cm)
    wi_row = jnp.exp(m_all - M_row)
    emt_row = jnp.exp(-(b_row + M_row))
    M_last = M_row[:, end:end + 1]
    b_last = b_row[:, end:end + 1]
    ws_row = jnp.exp(a_row - M_last)
    wc_all = jnp.exp(m_all - M_last)
    m_scr[...] = jnp.broadcast_to(b_last + M_last, m_all.shape)
    cols = jnp.concatenate([M_row, wi_row, emt_row, jnp.zeros((L - 3 * H, L), F32)], axis=0).T
    ones = jnp.ones((L, R), BF16)
    for h in range(H):
        qh = q_ref[:, h * DK:(h + 1) * DK]
        kh = k_ref[:, h * DK:(h + 1) * DK]
        Sh = S_scr[h]
        qk_scr[h] = lax.dot_general(qh, kh, _NT, preferred_element_type=F32)
        S_hi = Sh.astype(BF16)
        n_lo = (Sh[:, DV:] - S_hi[:, DV:].astype(F32)).astype(BF16)
        inter_scr[h] = jnp.dot(qh, jnp.concatenate([S_hi, n_lo], axis=-1),
                               preferred_element_type=F32)
    for h in range(H):
        kwT = k_ref[:, h * DK:(h + 1) * DK].astype(F32).T * ws_row[h:h + 1, :]
        kwT_hi = kwT.astype(BF16)
        kw_scr[0, h] = kwT_hi
        kw_scr[1, h] = (kwT - kwT_hi.astype(F32)).astype(BF16)
    for h in range(H):
        v_ext = jnp.concatenate([v_ref[:, h * DV:(h + 1) * DV], ones], axis=-1)
        upd = jnp.dot(kw_scr[0, h], v_ext, preferred_element_type=F32)
        upd_n = upd[:, DV:] + jnp.dot(kw_scr[1, h], ones, preferred_element_type=F32)
        wc = jnp.concatenate([wc_all[h:h + 1, :]] * (DV // R + 1), axis=-1)
        S_scr[h] = wc * S_scr[h] + jnp.concatenate([upd[:, :DV], upd_n], axis=-1)
    for h in range(H):
        M_col = cols[:, h:h + 1]
        w = jnp.where(causal, jnp.exp(a_row[h:h + 1, :] - M_col), 0.0)
        sqk = qk_scr[h] * scale * w
        qk_scr[h] = sqk
        num_scr[h] = jnp.dot(sqk.astype(BF16), v_ref[:, h * DV:(h + 1) * DV], preferred_element_type=F32)
    for h in range(H):
        wi_col = cols[:, H + h:H + h + 1]
        emt_col = cols[:, 2 * H + h:2 * H + h + 1]
        inter = inter_scr[h] * scale
        qn = inter[:, DV:DV + 1] + inter[:, DV + R:DV + R + 1]
        num = num_scr[h] + wi_col * inter[:, :DV]
        den = jnp.sum(qk_scr[h], axis=-1, keepdims=True) + wi_col * qn
        h_ref[:, h * DV:(h + 1) * DV] = num / jnp.maximum(jnp.abs(den), emt_col)

    @pl.when(is_last)
    def _():
        Sf_ref[0, 0] = S_scr[...]
        mf_ref[0, 0] = m_scr[...]


def mlstm_bidir(proj, gates_dir, b_gates_dir, S0, m0, seq_chunks):
    H, DK, DV, L = C_HEADS, C_QK_DIM, C_V_DIM, C_CHUNK
    DS = DV + LANES
    N = proj.shape[0]
    S = len(seq_chunks)
    rowblk, seq_id, first, last = [[], []], [], [], []
    base = 0
    for sq, nc in enumerate(seq_chunks):
        rowblk[0] += [base + c for c in range(nc)]
        rowblk[1] += [base + nc - 1 - c for c in range(nc)]
        seq_id += [sq] * nc
        first += [1] + [0] * (nc - 1)
        last += [0] * (nc - 1) + [1]
        base += nc
    n_steps = base
    rowblk = jnp.asarray(np.array(rowblk, np.int32).reshape(-1))
    tables = (rowblk, jnp.asarray(seq_id, I32), jnp.asarray(first, I32), jnp.asarray(last, I32))

    def row_map(colblk):
        return lambda d, s, rb, sq, fi, la: (rb[d * n_steps + s], colblk)

    def state_map(nd):
        return lambda d, s, rb, sq, fi, la: (sq[s], d) + (0,) * nd

    in_specs = [pl.BlockSpec((L, H * DK), row_map(0)),
                pl.BlockSpec((L, H * DK), row_map(1)),
                pl.BlockSpec((L, H * DV), row_map(1)),
                pl.BlockSpec((1, 2, L, LANES), lambda d, s, rb, sq, fi, la: (d, 0, rb[d * n_steps + s], 0)),
                pl.BlockSpec((1, 2, 1, LANES), lambda d, s, rb, sq, fi, la: (d, 0, 0, 0)),
                pl.BlockSpec((1, 1, H, DK, DS), state_map(3)),
                pl.BlockSpec((1, 1, H, LANES), state_map(2))]
    out_specs = (pl.BlockSpec((None, L, H * DV), lambda d, s, rb, sq, fi, la: (d, rb[d * n_steps + s], 0)),
                 pl.BlockSpec((1, 1, H, DK, DS), state_map(3)),
                 pl.BlockSpec((1, 1, H, LANES), state_map(2)))
    out_shape = (jax.ShapeDtypeStruct((2, N, H * DV), F32),
                 jax.ShapeDtypeStruct((S, 2, H, DK, DS), F32),
                 jax.ShapeDtypeStruct((S, 2, H, LANES), F32))
    return pl.pallas_call(
        _mlstm_kernel,
        out_shape=out_shape,
        grid_spec=pltpu.PrefetchScalarGridSpec(
            num_scalar_prefetch=4,
            grid=(2, n_steps),
            in_specs=in_specs,
            out_specs=out_specs,
            scratch_shapes=[pltpu.VMEM((H, DK, DS), F32), pltpu.VMEM((H, LANES), F32),
                            pltpu.VMEM((H, L, L), F32), pltpu.VMEM((H, L, DS + LANES), F32),
                            pltpu.VMEM((H, L, DV), F32), pltpu.VMEM((2, H, DK, L), BF16)]),
        compiler_params=_cparams(("arbitrary", "arbitrary"), 40),
        name="mlstm_bidir",
    )(*tables, proj, proj, proj, gates_dir, b_gates_dir, S0, m0)


def _top2_of4(vals):
    m1, i1 = vals[0], jnp.zeros(vals[0].shape, I32)
    for j in range(1, 4):
        better = vals[j] > m1
        m1 = jnp.where(better, vals[j], m1)
        i1 = jnp.where(better, j, i1)
    m2, i2 = jnp.full(vals[0].shape, -jnp.inf, F32), jnp.zeros(vals[0].shape, I32)
    for j in range(4):
        cand = jnp.where(i1 == j, -jnp.inf, vals[j])
        better = cand > m2
        m2 = jnp.where(better, cand, m2)
        i2 = jnp.where(better, j, i2)
    return m1, i1, m2, i2


def _router_kernel(x_ref, g_ref, mod_ref, wr_ref, br_ref, h_ref, ids_ref, wts_ref, *, shift_idx, scale_idx):
    h = _modulated_norm(x_ref[...], g_ref[...], mod_ref[0], shift_idx, scale_idx)
    h_hi = h.astype(BF16)
    h_ref[...] = h_hi
    h_lo = (h - h_hi.astype(F32)).astype(BF16)
    w_hi = wr_ref[0]
    w_lo = wr_ref[1]
    logits = (lax.dot_general(w_hi, h_hi, _NT, preferred_element_type=F32)
              + lax.dot_general(w_lo, h_hi, _NT, preferred_element_type=F32)
              + lax.dot_general(w_hi, h_lo, _NT, preferred_element_type=F32))
    aff = jax.nn.sigmoid(logits)
    sel = aff + br_ref[...]
    aff_rows = [aff[e:e + 1, :] for e in range(N_EXPERTS)]
    sel_rows = [sel[e:e + 1, :] for e in range(N_EXPERTS)]
    tops = [_top2_of4(sel_rows[4 * gidx:4 * gidx + 4]) for gidx in range(N_GROUPS)]
    best = tops[0][0] + tops[0][2]
    grp = jnp.zeros(best.shape, I32)
    i1, i2 = tops[0][1], tops[0][3]
    for gidx in range(1, N_GROUPS):
        score = tops[gidx][0] + tops[gidx][2]
        better = score > best
        best = jnp.where(better, score, best)
        grp = jnp.where(better, gidx, grp)
        i1 = jnp.where(better, tops[gidx][1], i1)
        i2 = jnp.where(better, tops[gidx][3], i2)
    e1 = grp * EXPERTS_PER_GROUP + i1
    e2 = grp * EXPERTS_PER_GROUP + i2
    w1 = jnp.zeros(best.shape, F32)
    w2 = jnp.zeros(best.shape, F32)
    for e in range(N_EXPERTS):
        w1 = jnp.where(e1 == e, aff_rows[e], w1)
        w2 = jnp.where(e2 == e, aff_rows[e], w2)
    tot = w1 + w2
    ids_ref[...] = jnp.zeros(ids_ref.shape, I32)
    wts_ref[...] = jnp.zeros(wts_ref.shape, F32)
    ids_ref[0:1, :] = e1
    ids_ref[1:2, :] = e2
    wts_ref[0:1, :] = w1 / tot
    wts_ref[1:2, :] = w2 / tot


def router_weight_pair(w_router):
    w_t = w_router.T.astype(F32)
    w_hi = w_t.astype(BF16)
    return jnp.stack([w_hi, (w_t - w_hi.astype(F32)).astype(BF16)], axis=0)


def moe_router(x, row_off, n_rows, g, mod, w_router_t, b_router, shift_idx, scale_idx, tm=512):
    D = x.shape[1]
    off_b = row_off // tm
    per_seg = SEG_ROWS // tm
    return pl.pallas_call(
        functools.partial(_router_kernel, shift_idx=shift_idx, scale_idx=scale_idx),
        out_shape=(jax.ShapeDtypeStruct((n_rows, D), BF16),
                   jax.ShapeDtypeStruct((8, n_rows), I32),
                   jax.ShapeDtypeStruct((8, n_rows), F32)),
        grid=(n_rows // tm,),
        in_specs=[pl.BlockSpec((tm, D), lambda i: (i + off_b, 0)),
                  pl.BlockSpec((1, D), lambda i: (0, 0)),
                  pl.BlockSpec((1, 6, D), lambda i: ((i + off_b) // per_seg, 0, 0)),
                  pl.BlockSpec((2, N_EXPERTS, D), lambda i: (0, 0, 0)),
                  pl.BlockSpec((N_EXPERTS, 1), lambda i: (0, 0))],
        out_specs=(pl.BlockSpec((tm, D), lambda i: (i, 0)),
                   pl.BlockSpec((8, tm), lambda i: (0, i)),
                   pl.BlockSpec((8, tm), lambda i: (0, i))),
        compiler_params=_cparams(("parallel",), 40),
        name="moe_router",
    )(x, g.reshape(1, D), mod, w_router_t, b_router.reshape(N_EXPERTS, 1).astype(F32))


MOE_CAST_ROWS = 128


def _moe_expert_kernel(be_ref, first_ref, next_ref, nu_ref, x_ref, wg_hbm, wu_hbm, wd_hbm, y_ref,
                       wg_st, wu_st, wd_st, wg_bf, wu_bf, wd_bf, sems, *, e_off):
    i = pl.program_id(0)

    def weight_copies(e):
        return (pltpu.make_async_copy(wg_hbm.at[e_off + e], wg_st, sems.at[0]),
                pltpu.make_async_copy(wu_hbm.at[e_off + e], wu_st, sems.at[1]),
                pltpu.make_async_copy(wd_hbm.at[e_off + e], wd_st, sems.at[2]))

    @pl.when(i == 0)
    def _():
        for cp in weight_copies(be_ref[0]):
            cp.start()

    @pl.when(first_ref[i] == 1)
    def _():
        for cp in weight_copies(be_ref[i]):
            cp.wait()
        for st, bf in ((wg_st, wg_bf), (wu_st, wu_bf), (wd_st, wd_bf)):
            def cast_rows(c, carry, st=st, bf=bf):
                r = pl.multiple_of(c * MOE_CAST_ROWS, MOE_CAST_ROWS)
                bf[pl.ds(r, MOE_CAST_ROWS), :] = st[pl.ds(r, MOE_CAST_ROWS), :].astype(BF16)
                return carry
            lax.fori_loop(0, st.shape[0] // MOE_CAST_ROWS, cast_rows, 0)

        @pl.when(next_ref[i] >= 0)
        def _():
            for cp in weight_copies(next_ref[i]):
                cp.start()

    @pl.when(i < nu_ref[0])
    def _():
        x = x_ref[...]
        gate = jnp.dot(x, wg_bf[...], preferred_element_type=F32)
        up = jnp.dot(x, wu_bf[...], preferred_element_type=F32)
        act = (gate * jax.nn.sigmoid(gate) * up).astype(BF16)
        y_ref[...] = jnp.dot(act, wd_bf[...], preferred_element_type=F32).astype(y_ref.dtype)

    @pl.when(i >= nu_ref[0])
    def _():
        y_ref[...] = jnp.zeros(y_ref.shape, y_ref.dtype)


def moe_experts(xg, plan, wg, wu, wd, e_off):
    R, D = xg.shape
    F = wg.shape[2]
    nb = R // MOE_ROWS
    hbm = pl.BlockSpec(memory_space=pl.ANY)
    return pl.pallas_call(
        functools.partial(_moe_expert_kernel, e_off=e_off),
        out_shape=jax.ShapeDtypeStruct((R, D), BF16),
        grid_spec=pltpu.PrefetchScalarGridSpec(
            num_scalar_prefetch=4,
            grid=(nb,),
            in_specs=[pl.BlockSpec((MOE_ROWS, D), lambda i, *_: (i, 0)), hbm, hbm, hbm],
            out_specs=pl.BlockSpec((MOE_ROWS, D), lambda i, *_: (i, 0)),
            scratch_shapes=[pltpu.VMEM((D, F), F32), pltpu.VMEM((D, F), F32), pltpu.VMEM((F, D), F32),
                            pltpu.VMEM((D, F), BF16), pltpu.VMEM((D, F), BF16), pltpu.VMEM((F, D), BF16),
                            pltpu.SemaphoreType.DMA((3,))]),
        compiler_params=_cparams(("arbitrary",), 56),
        name="moe_experts",
    )(plan["block_e"], plan["first"], plan["next_e"], plan["n_used"], xg, wg, wu, wd)


def _combine_kernel(x_ref, mod_ref, *rest, gate_idx, final, tile_off, chunk_tiles):
    rest = list(rest)
    fg_ref = rest.pop(0) if final else None
    o_ref = rest.pop()
    t = pl.program_id(0) + tile_off
    for c, (lo, n) in enumerate(chunk_tiles):
        y0_ref, y1_ref, w_ref = rest[3 * c:3 * c + 3]

        @pl.when((t >= lo) & (t < lo + n))
        def _(y0_ref=y0_ref, y1_ref=y1_ref, w_ref=w_ref):
            y = w_ref[:, 0:1] * y0_ref[...].astype(F32) + w_ref[:, 1:2] * y1_ref[...].astype(F32)
            x = x_ref[...] + mod_ref[0, gate_idx:gate_idx + 1, :] * y
            if final:
                ms = jnp.mean(x * x, axis=-1, keepdims=True)
                x = x * lax.rsqrt(ms + EPS) * fg_ref[...]
            o_ref[...] = x


def moe_combine(x, row_off, n_rows, parts, mod, gate_idx, final_g=None, tm=512):
    D = x.shape[1]
    xb = row_off // tm
    per_seg = SEG_ROWS // tm
    final = final_g is not None
    in_specs = [pl.BlockSpec((tm, D), lambda i: (i + xb, 0)),
                pl.BlockSpec((1, 6, D), lambda i: ((i + xb) // per_seg, 0, 0))]
    args = [x, mod]
    if final:
        in_specs.append(pl.BlockSpec((1, D), lambda i: (0, 0)))
        args.append(final_g.reshape(1, D))
    chunk_tiles = []
    for c_off, c_n, y01, wts in parts:
        lo, n = c_off // tm, c_n // tm
        chunk_tiles.append((lo, n))

        def chunk_map(k, lo=lo, n=n):
            return lambda i: (jnp.clip(i + xb - lo, 0, n - 1) + k * n, 0)

        in_specs += [pl.BlockSpec((tm, D), chunk_map(0)), pl.BlockSpec((tm, D), chunk_map(1)),
                     pl.BlockSpec((tm, 2), chunk_map(0))]
        args += [y01, y01, wts]
    return pl.pallas_call(
        functools.partial(_combine_kernel, gate_idx=gate_idx, final=final, tile_off=xb,
                          chunk_tiles=tuple(chunk_tiles)),
        out_shape=jax.ShapeDtypeStruct((n_rows, D), F32),
        grid=(n_rows // tm,),
        in_specs=in_specs,
        out_specs=pl.BlockSpec((tm, D), lambda i: (i, 0)),
        compiler_params=_cparams(("arbitrary",), 48),
        name="moe_combine",
    )(*args)


def moe_dispatch_plan(ids, n_tokens):
    n_assign = 2 * n_tokens
    n_blocks = n_assign // MOE_ROWS + N_EXPERTS
    experts = jnp.arange(N_EXPERTS, dtype=I32)
    flat_e = ids.T.reshape(n_assign)
    onehot = (flat_e[:, None] == experts[None, :]).astype(I32)
    csum = jnp.cumsum(onehot, axis=0)
    rank = jnp.sum((csum - onehot) * onehot, axis=1)
    counts = csum[-1]
    padded = (counts + MOE_ROWS - 1) // MOE_ROWS * MOE_ROWS
    pend = jnp.cumsum(padded)
    dest = (pend - padded)[flat_e] + rank
    n_used = (pend[-1] // MOE_ROWS).astype(I32)
    block_idx = jnp.arange(n_blocks, dtype=I32)
    block_e = jnp.minimum(jnp.sum((pend[None, :] <= block_idx[:, None] * MOE_ROWS).astype(I32), axis=1),
                          N_EXPERTS - 1)
    prev_e = jnp.concatenate([jnp.full((1,), -1, I32), block_e[:-1]])
    first = ((block_idx < n_used) & (block_e != prev_e)).astype(I32)
    later = (experts[None, :] > experts[:, None]) & (counts[None, :] > 0)
    next_of = jnp.min(jnp.where(later, experts[None, :], N_EXPERTS), axis=1)
    next_of = jnp.where(next_of == N_EXPERTS, -1, next_of)
    next_e = jnp.sum(jnp.where(block_e[:, None] == experts[None, :], next_of[None, :], 0), axis=1).astype(I32)
    src_tok = jnp.zeros((n_blocks * MOE_ROWS,), I32).at[dest].set(
        jnp.arange(n_assign, dtype=I32) // 2, unique_indices=True, mode="promise_in_bounds")
    plan = {"block_e": block_e, "first": first, "next_e": next_e, "n_used": n_used.reshape(1)}
    return dest.reshape(n_tokens, 2), src_tok, plan


def channel_mixer(x, g, mod, w_router_t, b_router, wg, wu, wd, e_off, chunks, outputs, final_g=None):
    h, ids8, wts8 = moe_router(x, 0, x.shape[0], g, mod, w_router_t, b_router, shift_idx=3, scale_idx=4)
    parts = []
    for c_off, c_n in chunks:
        dest, src_tok, plan = moe_dispatch_plan(ids8[:2, c_off:c_off + c_n], c_n)
        xg = h.at[src_tok + c_off].get(mode="promise_in_bounds")
        ybuf = moe_experts(xg, plan, wg, wu, wd, e_off)
        y01 = ybuf.at[dest.T.reshape(-1)].get(mode="promise_in_bounds")
        parts.append((c_off, c_n, y01, wts8[:2, c_off:c_off + c_n].T))
    return [moe_combine(x, o_off, o_n, parts, mod, 5, final_g) for o_off, o_n in outputs]


def kernel(x_prompt, x_sample, c, cache_a_k, cache_a_v, cache_b_k, cache_b_v, state_C, state_n, state_m,
           c_ctx, norm1_g, norm2_g, w_ada, b_ada, w_in_ab, w_out_ab, rel_bias_a, sink_b, w_in_c, b_gates_c,
           norm_c_g, w_out_c, w_router, b_router, w_gate_e, w_up_e, w_down_e, final_norm_g):
    D = D_MODEL
    Np, Nl = BATCH * SEQ, DEC_BATCH * DEC_SEQ
    N = Np + Nl
    H = C_HEADS
    x = None
    x_ctx, x_lat = x_prompt.reshape(Np, D), x_sample.reshape(Nl, D)
    moe_chunks = [(0, N)]
    n_e = w_gate_e.shape[1]
    wg_all = w_gate_e.reshape(DEPTH * n_e, D, D_FF_EXPERT)
    wu_all = w_up_e.reshape(DEPTH * n_e, D, D_FF_EXPERT)
    wd_all = w_down_e.reshape(DEPTH * n_e, D_FF_EXPERT, D)

    cvec = jnp.concatenate([c_ctx[None, :], c, jnp.zeros((8 - 1 - DEC_BATCH, D), F32)], axis=0)
    mod_all = ada_mod_all(cvec, w_ada, b_ada).reshape(DEPTH, 8, 6, D)
    w_router_t = router_weight_pair(w_router)

    outs = {}
    for l in range(DEPTH):
        mod = mod_all[l]
        j = l // 2
        if l % 2 == 0:
            w_in = w_in_ab[j]
            if x is not None:
                x_ctx, x_lat = x[:Np], x[Np:]
            proj_ctx = norm_mod_matmul(x_ctx, 0, Np, norm1_g[l], mod, w_in, F32, 0, 1, seg_row_off=0,
                                       precise=True, tm=512)
            proj_lat = norm_mod_matmul(x_lat, 0, Nl, norm1_g[l], mod, w_in, BF16, 0, 1, seg_row_off=Np, tn=768)
            o_ctx = ctx_attention(proj_ctx, sink_b[j], BATCH, SEQ)
            bias_mask = nat_bias_mask(rel_bias_a[j], DEC_SEQ // GRID_W)
            oa = nat_attention(proj_lat, cache_a_k[:, j].reshape(DEC_BATCH, PAST_LEN, A_WIDTH).astype(BF16),
                               cache_a_v[:, j].reshape(DEC_BATCH, PAST_LEN, A_WIDTH).astype(BF16), bias_mask,
                               DEC_BATCH, DEC_SEQ, A_HEADS)
            cos, sin = rope_tables(DEC_SEQ)
            qk_rot = rope_heads(proj_lat, 3 * A_WIDTH, 3 * A_WIDTH + B_Q_WIDTH, cos, sin, DEC_SEQ)
            ob = swa_attention(qk_rot, proj_lat, 3 * A_HEADS + B_HEADS + B_KV_HEADS,
                               cache_b_k[:, j].reshape(DEC_BATCH, PAST_LEN, B_KV_WIDTH).astype(BF16),
                               cache_b_v[:, j].reshape(DEC_BATCH, PAST_LEN, B_KV_WIDTH).astype(BF16), sink_b[j],
                               DEC_BATCH, DEC_SEQ)
            x = attn_out_residual(o_ctx, oa, ob, jnp.stack(_hi_lo(w_out_ab[j]), axis=0), x_ctx, x_lat, mod,
                                  gate_idx=2)
            kv = proj_ctx[:, A_WIDTH:]
            outs.setdefault("ak", []).append(kv[:, :A_WIDTH].reshape(BATCH, SEQ, A_HEADS, HEAD_DIM))
            outs.setdefault("av", []).append(kv[:, A_WIDTH:2 * A_WIDTH].reshape(BATCH, SEQ, A_HEADS, HEAD_DIM))
            kvb = proj_ctx[:, 3 * A_WIDTH + B_Q_WIDTH:]
            outs.setdefault("bk", []).append(kvb[:, :B_KV_WIDTH].reshape(BATCH, SEQ, B_KV_HEADS, HEAD_DIM))
            outs.setdefault("bv", []).append(kvb[:, B_KV_WIDTH:].reshape(BATCH, SEQ, B_KV_HEADS, HEAD_DIM))
        else:
            n_main = 2 * C_QK_WIDTH + 2 * C_V_WIDTH
            w_main = w_in_c[j]
            w_gates = jnp.pad(w_in_c[j][:, n_main:], ((0, 0), (0, LANES - 4 * H)))
            proj, gates = norm_mod_matmul(x, 0, N, norm1_g[l], mod, w_main, BF16, 0, 1, w_aux=w_gates,
                                          n_out=n_main, tn=1024)
            gates_dir = jnp.pad(gates[:, :4 * H].reshape(N, 2, 2, H).transpose(1, 2, 0, 3),
                                ((0, 0), (0, 0), (0, 0), (0, LANES - H)))
            bg = jnp.pad(b_gates_c[j].astype(F32).reshape(2, 2, 1, H), ((0, 0), (0, 0), (0, 0), (0, LANES - H)))
            S = BATCH + DEC_BATCH
            C0 = jnp.concatenate([jnp.zeros((BATCH, 2, H, C_QK_DIM, C_V_DIM), F32), state_C[:, j]], axis=0)
            n0 = jnp.concatenate([jnp.zeros((BATCH, 2, H, C_QK_DIM), F32), state_n[:, j]], axis=0)
            m0 = jnp.concatenate([jnp.zeros((BATCH, 2, H), F32), state_m[:, j]], axis=0)
            S0 = jnp.concatenate([C0, jnp.broadcast_to(n0[..., None], n0.shape + (LANES,))], axis=-1)
            m0 = jnp.broadcast_to(m0[..., None], (S, 2, H, LANES))
            seq_chunks = [SEQ // C_CHUNK] * BATCH + [DEC_SEQ // C_CHUNK] * DEC_BATCH
            h_dir, Sf, mf = mlstm_bidir(proj, gates_dir, bg, S0, m0, seq_chunks)
            x = mlstm_out_residual(h_dir, proj, 2, norm_c_g[j], w_out_c[j].astype(BF16), x, mod, gate_idx=2)
            outs.setdefault("C", []).append(Sf[:BATCH, ..., :C_V_DIM])
            outs.setdefault("n", []).append(Sf[:BATCH, ..., C_V_DIM])
            outs.setdefault("m", []).append(mf[:BATCH, :, :, 0])
        last = l == DEPTH - 1
        pieces = channel_mixer(x, norm2_g[l], mod, w_router_t, b_router, wg_all, wu_all, wd_all, l * n_e,
                               moe_chunks, [(0, Np), (Np, Nl)] if last else [(0, N)],
                               final_norm_g if last else None)
        if last:
            y_prompt = pieces[0].reshape(BATCH, SEQ, D)
            y_sample = pieces[1].reshape(DEC_BATCH, DEC_SEQ, D)
        else:
            x = pieces[0]

    return (y_prompt, y_sample,
            jnp.stack(outs["ak"], axis=1), jnp.stack(outs["av"], axis=1),
            jnp.stack(outs["bk"], axis=1), jnp.stack(outs["bv"], axis=1),
            jnp.stack(outs["C"], axis=1), jnp.stack(outs["n"], axis=1), jnp.stack(outs["m"], axis=1))
```

```python
import functools

import numpy as np
import jax
import jax.numpy as jnp
from jax import lax
from jax.experimental import pallas as pl
from jax.experimental.pallas import tpu as pltpu

F32 = jnp.float32
BF16 = jnp.bfloat16
I32 = jnp.int32

D_MODEL = 2048
BATCH = 16
SEQ = 256
DEPTH = 2
DEC_BATCH = 4
DEC_SEQ = 4096
PAST_LEN = 512
GRID_W = 64
HEAD_DIM = 128
A_HEADS = 8
NA_ROWS = 8
NA_COLS = 16
B_HEADS = 8
B_KV_HEADS = 2
B_WINDOW = 128
ROPE_THETA = 10000.0
C_HEADS = 8
C_QK_DIM = 128
C_V_DIM = 256
C_CHUNK = 128
N_EXPERTS = 16
N_GROUPS = 4
EXPERTS_PER_GROUP = N_EXPERTS // N_GROUPS
D_FF_EXPERT = 1024
EPS = 1e-6

A_WIDTH = A_HEADS * HEAD_DIM
B_Q_WIDTH = B_HEADS * HEAD_DIM
B_KV_WIDTH = B_KV_HEADS * HEAD_DIM
C_QK_WIDTH = C_HEADS * C_QK_DIM
C_V_WIDTH = C_HEADS * C_V_DIM

SEG_ROWS = 4096
NEG_BIG = -1e30
MOE_ROWS = 256
NORM_ROWS = 256
MIB = 1024 * 1024
LANES = 128

_NT = (((1,), (1,)), ((), ()))


def _cparams(sem, vmem_mib):
    return pltpu.CompilerParams(dimension_semantics=sem, vmem_limit_bytes=vmem_mib * MIB)


def _modulated_norm(x, g, mod, shift_idx, scale_idx):
    ms = jnp.mean(x * x, axis=-1, keepdims=True)
    y = x * lax.rsqrt(ms + EPS) * g
    return y * (1.0 + mod[scale_idx:scale_idx + 1, :]) + mod[shift_idx:shift_idx + 1, :]


def _hi_lo(x):
    hi = x.astype(BF16)
    return hi, (x - hi.astype(F32)).astype(BF16)


def _dot3(a, b, dims=None):
    a_hi, a_lo = _hi_lo(a)
    b_hi, b_lo = _hi_lo(b)
    dims = (((a.ndim - 1,), (0,)), ((), ())) if dims is None else dims
    dot = functools.partial(lax.dot_general, dimension_numbers=dims, preferred_element_type=F32)
    return dot(a_hi, b_hi) + dot(a_lo, b_hi) + dot(a_hi, b_lo)


def _ada_kernel(c_ref, w_ref, b_ref, o_ref):
    c = c_ref[...]
    o_ref[0] = _dot3(c * jax.nn.sigmoid(c), w_ref[0]) + b_ref[0]


def ada_mod_all(cvec8, w_ada, b_ada, tn=1024):
    L, D, D6 = w_ada.shape
    return pl.pallas_call(
        _ada_kernel,
        out_shape=jax.ShapeDtypeStruct((L, 8, D6), F32),
        grid=(L, D6 // tn),
        in_specs=[pl.BlockSpec((8, D), lambda l, j: (0, 0)),
                  pl.BlockSpec((1, D, tn), lambda l, j: (l, 0, j)),
                  pl.BlockSpec((1, 1, tn), lambda l, j: (l, 0, j))],
        out_specs=pl.BlockSpec((1, 8, tn), lambda l, j: (l, 0, j)),
        compiler_params=_cparams(("parallel", "parallel"), 40),
        name="ada_mod",
    )(cvec8, w_ada, b_ada.reshape(L, 1, D6))


def _nmm_kernel(x_ref, g_ref, mod_ref, w_ref, *rest, shift_idx, scale_idx, has_aux, precise):
    rest = list(rest)
    waux_ref = rest.pop(0) if has_aux else None
    o_ref = rest.pop(0)
    oaux_ref = rest.pop(0) if has_aux else None
    h_scr = rest.pop(0)
    hlo_scr = rest.pop(0) if precise else None

    @pl.when(pl.program_id(1) == 0)
    def _():
        for r in range(x_ref.shape[0] // NORM_ROWS):
            rows = pl.ds(r * NORM_ROWS, NORM_ROWS)
            hf = _modulated_norm(x_ref[rows, :], g_ref[...], mod_ref[0], shift_idx, scale_idx)
            h = hf.astype(BF16)
            h_scr[rows, :] = h
            if precise:
                hlo_scr[rows, :] = (hf - h.astype(F32)).astype(BF16)
            if has_aux:
                oaux_ref[rows, :] = jnp.dot(h, waux_ref[...].astype(BF16), preferred_element_type=F32)

    if precise:
        w_hi, w_lo = _hi_lo(w_ref[...])
        acc = (jnp.dot(h_scr[...], w_hi, preferred_element_type=F32)
               + jnp.dot(hlo_scr[...], w_hi, preferred_element_type=F32)
               + jnp.dot(h_scr[...], w_lo, preferred_element_type=F32))
    else:
        acc = jnp.dot(h_scr[...], w_ref[...].astype(BF16), preferred_element_type=F32)
    o_ref[...] = acc.astype(o_ref.dtype)


def norm_mod_matmul(x, row_off, n_rows, g, mod, w, out_dtype, shift_idx, scale_idx, w_aux=None, n_out=None,
                    seg_row_off=None, precise=False, tm=1024, tn=512):
    D = x.shape[1]
    n_out = w.shape[1] if n_out is None else n_out
    off_b = row_off // tm
    seg_b = off_b if seg_row_off is None else seg_row_off // tm
    per_seg = SEG_ROWS // tm
    has_aux = w_aux is not None
    in_specs = [pl.BlockSpec((tm, D), lambda i, j: (i + off_b, 0)),
                pl.BlockSpec((1, D), lambda i, j: (0, 0)),
                pl.BlockSpec((1, 6, D), lambda i, j: ((i + seg_b) // per_seg, 0, 0)),
                pl.BlockSpec((D, tn), lambda i, j: (0, j))]
    out_shape = jax.ShapeDtypeStruct((n_rows, n_out), out_dtype)
    out_specs = pl.BlockSpec((tm, tn), lambda i, j: (i, j))
    args = [x, g.reshape(1, D), mod, w]
    if has_aux:
        n_aux = w_aux.shape[1]
        in_specs.append(pl.BlockSpec((D, n_aux), lambda i, j: (0, 0)))
        out_shape = (out_shape, jax.ShapeDtypeStruct((n_rows, n_aux), F32))
        out_specs = (out_specs, pl.BlockSpec((tm, n_aux), lambda i, j: (i, 0)))
        args.append(w_aux)
    return pl.pallas_call(
        functools.partial(_nmm_kernel, shift_idx=shift_idx, scale_idx=scale_idx, has_aux=has_aux,
                          precise=precise),
        out_shape=out_shape,
        grid=(n_rows // tm, n_out // tn),
        in_specs=in_specs,
        out_specs=out_specs,
        scratch_shapes=[pltpu.VMEM((tm, D), BF16)] * (2 if precise else 1),
        compiler_params=_cparams(("parallel", "arbitrary"), 56),
        name="norm_mod_matmul",
    )(*args)


def _attn_out_kernel(oc_ref, oa_ref, ob_ref, w_ref, xc_ref, xl_ref, mod_ref, o_ref, *, gate_idx, n_ctx_tiles):
    i = pl.program_id(0)
    gate = mod_ref[0, gate_idx:gate_idx + 1, :]

    @pl.when(i < n_ctx_tiles)
    def _():
        oc_hi, oc_lo = _hi_lo(oc_ref[...])
        acc = (jnp.dot(oc_hi, w_ref[0], preferred_element_type=F32)
               + jnp.dot(oc_lo, w_ref[0], preferred_element_type=F32)
               + jnp.dot(oc_hi, w_ref[1], preferred_element_type=F32))
        o_ref[...] = xc_ref[...] + gate * acc

    @pl.when(i >= n_ctx_tiles)
    def _():
        a = jnp.concatenate([oa_ref[...], ob_ref[...]], axis=-1)
        acc = jnp.dot(a, w_ref[0], preferred_element_type=F32)
        o_ref[...] = xl_ref[...] + gate * acc


def attn_out_residual(o_ctx, oa, ob, w, x_ctx, x_lat, mod, gate_idx, tm=256):
    n_ctx, K = o_ctx.shape
    n_lat = oa.shape[0]
    D = w.shape[2]
    nct = n_ctx // tm
    per_seg = SEG_ROWS // tm

    def ctx_map(i):
        return (jnp.minimum(i, nct - 1), 0)

    def lat_map(i):
        return (jnp.maximum(i - nct, 0), 0)

    return pl.pallas_call(
        functools.partial(_attn_out_kernel, gate_idx=gate_idx, n_ctx_tiles=nct),
        out_shape=jax.ShapeDtypeStruct((n_ctx + n_lat, D), F32),
        grid=((n_ctx + n_lat) // tm,),
        in_specs=[pl.BlockSpec((tm, K), ctx_map),
                  pl.BlockSpec((tm, oa.shape[1]), lat_map),
                  pl.BlockSpec((tm, ob.shape[1]), lat_map),
                  pl.BlockSpec((2, K, D), lambda i: (0, 0, 0), pipeline_mode=pl.Buffered(1)),
                  pl.BlockSpec((tm, D), ctx_map),
                  pl.BlockSpec((tm, D), lat_map),
                  pl.BlockSpec((1, 6, D), lambda i: (i // per_seg, 0, 0))],
        out_specs=pl.BlockSpec((tm, D), lambda i: (i, 0)),
        compiler_params=_cparams(("arbitrary",), 48),
        name="attn_out_residual",
    )(o_ctx, oa, ob, w, x_ctx, x_lat, mod)


def _mlstm_out_kernel(hf_ref, hb_ref, o_ref, ng_ref, w_ref, x_ref, mod_ref, out_ref, *, gate_idx):
    hs = hf_ref[...] + hb_ref[...]
    parts = []
    for h in range(C_HEADS):
        sl = slice(h * C_V_DIM, (h + 1) * C_V_DIM)
        xs = hs[:, sl]
        ms = jnp.mean(xs * xs, axis=-1, keepdims=True)
        hn = xs * lax.rsqrt(ms + EPS) * ng_ref[:, sl]
        parts.append((jax.nn.sigmoid(o_ref[:, sl].astype(F32)) * hn).astype(BF16))
    a = jnp.concatenate(parts, axis=-1)
    acc = jnp.dot(a, w_ref[...], preferred_element_type=F32)
    out_ref[...] = x_ref[...] + mod_ref[0, gate_idx:gate_idx + 1, :] * acc


def mlstm_out_residual(h_dir, proj, o_col_block, norm_g, w, x, mod, gate_idx, tm=256):
    n, D = x.shape
    V = C_V_WIDTH
    per_seg = SEG_ROWS // tm
    return pl.pallas_call(
        functools.partial(_mlstm_out_kernel, gate_idx=gate_idx),
        out_shape=jax.ShapeDtypeStruct((n, D), F32),
        grid=(n // tm,),
        in_specs=[pl.BlockSpec((None, tm, V), lambda i: (0, i, 0)),
                  pl.BlockSpec((None, tm, V), lambda i: (1, i, 0)),
                  pl.BlockSpec((tm, V), lambda i: (i, o_col_block)),
                  pl.BlockSpec((1, V), lambda i: (0, 0)),
                  pl.BlockSpec((V, D), lambda i: (0, 0)),
                  pl.BlockSpec((tm, D), lambda i: (i, 0)),
                  pl.BlockSpec((1, 6, D), lambda i: (i // per_seg, 0, 0))],
        out_specs=pl.BlockSpec((tm, D), lambda i: (i, 0)),
        compiler_params=_cparams(("parallel",), 48),
        name="mlstm_out_residual",
    )(h_dir, h_dir, proj, norm_g.reshape(1, V), w, x, mod)


def _ctx_attn_kernel(sink_ref, p_ref, o_ref):
    group = B_HEADS // B_KV_HEADS
    qb0 = 3 * A_HEADS
    kb0 = qb0 + B_HEADS
    vb0 = kb0 + B_KV_HEADS

    def cols(c):
        return p_ref[:, c * HEAD_DIM:(c + 1) * HEAD_DIM]

    for h in range(A_HEADS + B_HEADS):
        if h < A_HEADS:
            q, k, v = cols(h), cols(A_HEADS + h), cols(2 * A_HEADS + h)
        else:
            hb = h - A_HEADS
            q, k, v = cols(qb0 + hb), cols(kb0 + hb // group), cols(vb0 + hb // group)
        s = _dot3(q, k, _NT) * (HEAD_DIM ** -0.5)
        sk = sink_ref[h]
        m = jnp.maximum(jnp.max(s, axis=-1, keepdims=True), sk)
        p = jnp.exp(s - m)
        l = jnp.sum(p, axis=-1, keepdims=True) + jnp.exp(sk - m)
        o_ref[:, h * HEAD_DIM:(h + 1) * HEAD_DIM] = (_dot3(p, v) / l).astype(o_ref.dtype)


def ctx_attention(proj, sink_b, n_batch, seq):
    n_heads = A_HEADS + B_HEADS
    sinks = jnp.concatenate([jnp.full((A_HEADS,), NEG_BIG, F32), sink_b.astype(F32)])
    return pl.pallas_call(
        _ctx_attn_kernel,
        out_shape=jax.ShapeDtypeStruct((n_batch * seq, n_heads * HEAD_DIM), F32),
        grid_spec=pltpu.PrefetchScalarGridSpec(
            num_scalar_prefetch=1,
            grid=(n_batch,),
            in_specs=[pl.BlockSpec((seq, proj.shape[1]), lambda b, s: (b, 0))],
            out_specs=pl.BlockSpec((seq, n_heads * HEAD_DIM), lambda b, s: (b, 0))),
        compiler_params=_cparams(("parallel",), 40),
        name="ctx_attention",
    )(sinks, proj)


NAT_QROWS = 4


def nat_bias_mask(rel_bias, rows):
    W = GRID_W
    nb = rows // NAT_QROWS
    kh = min(NA_ROWS, rows)
    n_dr, n_dc = 2 * NA_ROWS - 1, 2 * NA_COLS - 1
    H = rel_bias.shape[0]
    cidx = np.clip(np.arange(W)[None, :] - np.arange(W)[:, None] + NA_COLS - 1, 0, n_dc - 1)
    onehot = jnp.asarray((cidx.reshape(1, W * W) == np.arange(n_dc)[:, None]).astype(np.float32))
    col_bias = jnp.dot(rel_bias.astype(F32).reshape(H * n_dr, n_dc), onehot,
                       precision=lax.Precision.HIGHEST).reshape(H, n_dr, W, W)
    q_rows = []
    for qi in range(NAT_QROWS):
        tiles = [col_bias[:, int(np.clip(NAT_QROWS * (kj - 1) + kjr - qi + NA_ROWS - 1, 0, n_dr - 1))]
                 for kj in range(3) for kjr in range(NAT_QROWS)]
        q_rows.append(jnp.concatenate(tiles, axis=-1))
    bias = jnp.concatenate(q_rows, axis=1)
    variants = []
    for g in (0, 1, nb - 1):
        i = np.arange(NAT_QROWS)[:, None, None, None, None]
        qc = np.arange(W)[None, :, None, None, None]
        j = np.arange(3)[None, None, :, None, None]
        jr = np.arange(NAT_QROWS)[None, None, None, :, None]
        kc = np.arange(W)[None, None, None, None, :]
        r = NAT_QROWS * g + i
        kblk = g - 1 + j
        kr = NAT_QROWS * kblk + jr
        rs = np.clip(r - kh // 2, 0, rows - kh)
        row_ok = (kblk >= 0) & (kblk < nb) & (kr >= rs) & (kr < rs + kh)
        cstart = np.clip(qc - NA_COLS // 2, 0, W - NA_COLS)
        col_ok = (kc >= cstart) & (kc < cstart + NA_COLS)
        ok = np.broadcast_to(row_ok & col_ok, (NAT_QROWS, W, 3, NAT_QROWS, W))
        n_q, n_k = NAT_QROWS * W, 3 * NAT_QROWS * W
        variants.append(jnp.where(jnp.asarray(ok.reshape(1, n_q, n_k)), bias, NEG_BIG))
    return jnp.stack(variants, axis=0)


NAT_HEADS_PER_STEP = 8


def _nat_kernel(q_ref, k0_ref, k1_ref, k2_ref, v0_ref, v1_ref, v2_ref, kc_ref, vc_ref, bm_ref, o_ref):
    scale = HEAD_DIM ** -0.5
    tq = q_ref.shape[0]
    for h in range(q_ref.shape[1] // HEAD_DIM):
        cs = slice(h * HEAD_DIM, (h + 1) * HEAD_DIM)
        q = q_ref[:, cs]
        s_lat = [lax.dot_general(q, k_ref[:, cs], _NT, preferred_element_type=F32) * scale
                 + bm_ref[0, h, :, j * tq:(j + 1) * tq]
                 for j, k_ref in enumerate((k0_ref, k1_ref, k2_ref))]
        s_ctx = lax.dot_general(q, kc_ref[0, :, cs], _NT, preferred_element_type=F32) * scale
        m = jnp.max(s_ctx, axis=-1, keepdims=True)
        for s in s_lat:
            m = jnp.maximum(m, jnp.max(s, axis=-1, keepdims=True))
        p_ctx = jnp.exp(s_ctx - m)
        l = jnp.sum(p_ctx, axis=-1, keepdims=True)
        acc = jnp.dot(p_ctx.astype(BF16), vc_ref[0, :, cs], preferred_element_type=F32)
        for s, v_ref in zip(s_lat, (v0_ref, v1_ref, v2_ref)):
            p = jnp.exp(s - m)
            l = l + jnp.sum(p, axis=-1, keepdims=True)
            acc = acc + jnp.dot(p.astype(BF16), v_ref[:, cs], preferred_element_type=F32)
        o_ref[:, cs] = (acc / l).astype(o_ref.dtype)


def nat_attention(proj, cache_k, cache_v, bias_mask, n_batch, T, n_heads):
    tq = NAT_QROWS * GRID_W
    nb = T // tq
    P = cache_k.shape[1]
    hs = min(NAT_HEADS_PER_STEP, n_heads)
    ng = n_heads // hs

    def kv_map(col0, j):
        return lambda b, h, g: (b * nb + jnp.clip(g - 1 + j, 0, nb - 1), col0 + h)

    blk = (tq, hs * HEAD_DIM)
    in_specs = [pl.BlockSpec(blk, lambda b, h, g: (b * nb + g, h))]
    in_specs += [pl.BlockSpec(blk, kv_map(ng, j)) for j in range(3)]
    in_specs += [pl.BlockSpec(blk, kv_map(2 * ng, j)) for j in range(3)]
    in_specs += [pl.BlockSpec((1, P, hs * HEAD_DIM), lambda b, h, g: (b, 0, h))] * 2
    in_specs += [pl.BlockSpec((1, hs, tq, 3 * tq),
                              lambda b, h, g: (jnp.where(g == 0, 0, jnp.where(g == nb - 1, 2, 1)), h, 0, 0))]
    return pl.pallas_call(
        _nat_kernel,
        out_shape=jax.ShapeDtypeStruct((n_batch * T, n_heads * HEAD_DIM), BF16),
        grid=(n_batch, ng, nb),
        in_specs=in_specs,
        out_specs=pl.BlockSpec(blk, lambda b, h, g: (b * nb + g, h)),
        compiler_params=_cparams(("parallel", "parallel", "arbitrary"), 40),
        name="nat_attention",
    )(proj, proj, proj, proj, proj, proj, proj, cache_k, cache_v, bias_mask)


def rope_tables(T):
    t = jnp.arange(T)
    row = (t // GRID_W).astype(F32)
    col = (t % GRID_W).astype(F32)
    nf = HEAD_DIM // 4
    freqs = ROPE_THETA ** (-jnp.arange(nf, dtype=F32) / nf)
    ar = row[:, None] * freqs
    ac = col[:, None] * freqs
    cos = jnp.concatenate([jnp.cos(ar), jnp.cos(ar), jnp.cos(ac), jnp.cos(ac)], axis=-1)
    sin = jnp.concatenate([-jnp.sin(ar), jnp.sin(ar), -jnp.sin(ac), jnp.sin(ac)], axis=-1)
    return cos, sin


def _rope_kernel(q_ref, k_ref, cos_ref, sin_ref, o_ref):
    nf = HEAD_DIM // 4
    cos = cos_ref[...]
    sin = sin_ref[...]
    lane = lax.broadcasted_iota(I32, cos.shape, 1)
    first_half = (lane & nf) == 0
    col = 0
    for src, gain in ((q_ref, HEAD_DIM ** -0.5), (k_ref, None)):
        c, s = (cos, sin) if gain is None else (cos * gain, sin * gain)
        for h in range(src.shape[1] // HEAD_DIM):
            x = src[:, h * HEAD_DIM:(h + 1) * HEAD_DIM].astype(F32)
            upper = pltpu.roll(x, HEAD_DIM - nf, 1)
            lower = pltpu.roll(x, nf, 1)
            partner = jnp.where(first_half, upper, lower)
            o_ref[:, col:col + HEAD_DIM] = (x * c + partner * s).astype(o_ref.dtype)
            col += HEAD_DIM


def rope_heads(proj, q_col0, k_col0, cos, sin, T, tm=512):
    n = proj.shape[0]
    per_seq = T // tm
    return pl.pallas_call(
        _rope_kernel,
        out_shape=jax.ShapeDtypeStruct((n, B_Q_WIDTH + B_KV_WIDTH), BF16),
        grid=(n // tm,),
        in_specs=[pl.BlockSpec((tm, B_Q_WIDTH), lambda i: (i, q_col0 // B_Q_WIDTH)),
                  pl.BlockSpec((tm, B_KV_WIDTH), lambda i: (i, k_col0 // B_KV_WIDTH)),
                  pl.BlockSpec((tm, HEAD_DIM), lambda i: (i % per_seq, 0)),
                  pl.BlockSpec((tm, HEAD_DIM), lambda i: (i % per_seq, 0))],
        out_specs=pl.BlockSpec((tm, B_Q_WIDTH + B_KV_WIDTH), lambda i: (i, 0)),
        compiler_params=_cparams(("parallel",), 32),
        name="rope_heads",
    )(proj, proj, cos, sin)


SWA_TQ = 2 * B_WINDOW


def _swa_kernel(sink_ref, q_ref, k0_ref, k1_ref, k2_ref, k3_ref, v0_ref, v1_ref, v2_ref, v3_ref,
                kc_ref, vc_ref, o_ref, *, T):
    group = B_HEADS // B_KV_HEADS
    kvh = pl.program_id(1)
    n = pl.program_id(2)
    k = jnp.concatenate([k0_ref[...], k1_ref[...], k2_ref[...], k3_ref[...]], axis=0)
    v = jnp.concatenate([v0_ref[...], v1_ref[...], v2_ref[...], v3_ref[...]], axis=0)
    kc = kc_ref[0]
    vc = vc_ref[0]
    nk = k.shape[0]
    qpos = n * SWA_TQ + lax.broadcasted_iota(I32, (SWA_TQ, nk), 0)
    kpos = n * SWA_TQ - B_WINDOW + lax.broadcasted_iota(I32, (SWA_TQ, nk), 1)
    dist = jnp.abs(qpos - kpos)
    ok = jnp.where(kpos >= 0, jnp.where(kpos < T, dist, B_WINDOW + 1), B_WINDOW + 1) <= B_WINDOW
    outs = []
    for gi in range(group):
        q = q_ref[:, gi * HEAD_DIM:(gi + 1) * HEAD_DIM]
        s_lat = jnp.where(ok, lax.dot_general(q, k, _NT, preferred_element_type=F32), NEG_BIG)
        s_ctx = lax.dot_general(q, kc, _NT, preferred_element_type=F32)
        sk = sink_ref[kvh * group + gi]
        m = jnp.maximum(jnp.maximum(jnp.max(s_lat, axis=-1, keepdims=True),
                                    jnp.max(s_ctx, axis=-1, keepdims=True)), sk)
        p_lat = jnp.exp(s_lat - m)
        p_ctx = jnp.exp(s_ctx - m)
        l = (jnp.sum(p_lat, axis=-1, keepdims=True) + jnp.sum(p_ctx, axis=-1, keepdims=True)
             + jnp.exp(sk - m))
        acc = (jnp.dot(p_lat.astype(BF16), v, preferred_element_type=F32)
               + jnp.dot(p_ctx.astype(BF16), vc, preferred_element_type=F32))
        outs.append((acc / l).astype(o_ref.dtype))
    o_ref[...] = jnp.concatenate(outs, axis=-1)


def swa_attention(qk_rot, proj, v_col0, cache_k, cache_v, sink, n_batch, T):
    group = B_HEADS // B_KV_HEADS
    nq = T // SWA_TQ
    nkb = T // B_WINDOW
    P = cache_k.shape[1]

    def kv_map(col0, j):
        return lambda b, kvh, n, s: (b * nkb + jnp.clip(2 * n - 1 + j, 0, nkb - 1), col0 + kvh)

    kblk = (B_WINDOW, HEAD_DIM)
    in_specs = [pl.BlockSpec((SWA_TQ, group * HEAD_DIM), lambda b, kvh, n, s: (b * nq + n, kvh))]
    in_specs += [pl.BlockSpec(kblk, kv_map(B_HEADS, j)) for j in range(4)]
    in_specs += [pl.BlockSpec(kblk, kv_map(v_col0, j)) for j in range(4)]
    in_specs += [pl.BlockSpec((1, P, HEAD_DIM), lambda b, kvh, n, s: (b, 0, kvh))] * 2
    return pl.pallas_call(
        functools.partial(_swa_kernel, T=T),
        out_shape=jax.ShapeDtypeStruct((n_batch * T, B_Q_WIDTH), BF16),
        grid_spec=pltpu.PrefetchScalarGridSpec(
            num_scalar_prefetch=1,
            grid=(n_batch, B_KV_HEADS, nq),
            in_specs=in_specs,
            out_specs=pl.BlockSpec((SWA_TQ, group * HEAD_DIM), lambda b, kvh, n, s: (b * nq + n, kvh))),
        compiler_params=_cparams(("parallel", "parallel", "arbitrary"), 32),
        name="swa_attention",
    )(sink.astype(F32), qk_rot, qk_rot, qk_rot, qk_rot, qk_rot, proj, proj, proj, proj, cache_k, cache_v)


def _mlstm_kernel(rowblk_ref, seq_ref, first_ref, last_ref, *refs):
    s = pl.program_id(1)
    for d in range(2):
        @pl.when(pl.program_id(0) == d)
        def _(d=d):
            _mlstm_chunk(d, first_ref[s] == 1, last_ref[s] == 1, *refs)


def _mlstm_chunk(d, is_first, is_last, q_ref, k_ref, v_ref, g_ref, bg_ref, S0_ref, m0_ref,
                 h_ref, Sf_ref, mf_ref, S_scr, m_scr, qk_scr, inter_scr, num_scr):
    H, DK, DV, L, R = C_HEADS, C_QK_DIM, C_V_DIM, C_CHUNK, LANES
    scale = DK ** -0.5

    @pl.when(is_first)
    def _():
        S_scr[...] = S0_ref[0, 0]
        m_scr[...] = m0_ref[0, 0]

    gi = g_ref[0, 0] + bg_ref[0, 0]
    lf = jax.nn.log_sigmoid(g_ref[0, 1] + bg_ref[0, 1])
    row = lax.broadcasted_iota(I32, (L, L), 0)
    col = lax.broadcasted_iota(I32, (L, L), 1)
    causal = col <= row if d == 0 else col >= row
    b_all = jnp.dot(causal.astype(F32), lf, precision=lax.Precision.HIGHEST,
                    preferred_element_type=F32)
    a_row = (gi - b_all).T[0:H, :]
    b_row = b_all.T[0:H, :]
    lane = lax.broadcasted_iota(I32, (H, L), 1)
    cm = a_row
    k = 1
    while k < L:
        if d == 0:
            shifted = jnp.where(lane >= k, pltpu.roll(cm, k, 1), -jnp.inf)
        else:
            shifted = jnp.where(lane < L - k, pltpu.roll(cm, L - k, 1), -jnp.inf)
        cm = jnp.maximum(cm, shifted)
        k *= 2
    end = L - 1 if d == 0 else 0
    m_all = m_scr[...]
    M_row = jnp.maximum(m_all, cm)
    wi_row = jnp.exp(m_all - M_row)
    emt_row = jnp.exp(-(b_row + M_row))
    M_last = M_row[:, end:end + 1]
    b_last = b_row[:, end:end + 1]
    ws_row = jnp.exp(a_row - M_last)
    wc_all = jnp.exp(m_all - M_last)
    m_scr[...] = jnp.broadcast_to(b_last + M_last, m_all.shape)
    cols = jnp.concatenate([M_row, wi_row, emt_row, jnp.zeros((L - 3 * H, L), F32)], axis=0).T
    ones = jnp.ones((L, R), BF16)
    for h in range(H):
        qh = q_ref[:, h * DK:(h + 1) * DK]
        kh = k_ref[:, h * DK:(h + 1) * DK]
        Sh = S_scr[h]
        qk_scr[h] = lax.dot_general(qh, kh, _NT, preferred_element_type=F32)
        S_hi = Sh.astype(BF16)
        n_lo = (Sh[:, DV:] - S_hi[:, DV:].astype(F32)).astype(BF16)
        inter_scr[h] = jnp.dot(qh, jnp.concatenate([S_hi, n_lo], axis=-1),
                               preferred_element_type=F32)
    for h in range(H):
        kh = k_ref[:, h * DK:(h + 1) * DK]
        v_ext = jnp.concatenate([v_ref[:, h * DV:(h + 1) * DV], ones], axis=-1)
        kwT = kh.astype(F32).T * ws_row[h:h + 1, :]
        kwT_hi = kwT.astype(BF16)
        kwT_lo = (kwT - kwT_hi.astype(F32)).astype(BF16)
        upd = jnp.dot(kwT_hi, v_ext, preferred_element_type=F32)
        upd_n = upd[:, DV:] + jnp.dot(kwT_lo, ones, preferred_element_type=F32)
        wc = jnp.concatenate([wc_all[h:h + 1, :]] * (DV // R + 1), axis=-1)
        S_scr[h] = wc * S_scr[h] + jnp.concatenate([upd[:, :DV], upd_n], axis=-1)
    for h in range(H):
        M_col = cols[:, h:h + 1]
        w = jnp.where(causal, jnp.exp(a_row[h:h + 1, :] - M_col), 0.0)
        sqk = qk_scr[h] * scale * w
        qk_scr[h] = sqk
        num_scr[h] = jnp.dot(sqk.astype(BF16), v_ref[:, h * DV:(h + 1) * DV], preferred_element_type=F32)
    for h in range(H):
        wi_col = cols[:, H + h:H + h + 1]
        emt_col = cols[:, 2 * H + h:2 * H + h + 1]
        inter = inter_scr[h] * scale
        qn = inter[:, DV:DV + 1] + inter[:, DV + R:DV + R + 1]
        num = num_scr[h] + wi_col * inter[:, :DV]
        den = jnp.sum(qk_scr[h], axis=-1, keepdims=True) + wi_col * qn
        h_ref[:, h * DV:(h + 1) * DV] = num / jnp.maximum(jnp.abs(den), emt_col)

    @pl.when(is_last)
    def _():
        Sf_ref[0, 0] = S_scr[...]
        mf_ref[0, 0] = m_scr[...]


def mlstm_bidir(proj, gates_dir, b_gates_dir, S0, m0, seq_chunks):
    H, DK, DV, L = C_HEADS, C_QK_DIM, C_V_DIM, C_CHUNK
    DS = DV + LANES
    N = proj.shape[0]
    S = len(seq_chunks)
    rowblk, seq_id, first, last = [[], []], [], [], []
    base = 0
    for sq, nc in enumerate(seq_chunks):
        rowblk[0] += [base + c for c in range(nc)]
        rowblk[1] += [base + nc - 1 - c for c in range(nc)]
        seq_id += [sq] * nc
        first += [1] + [0] * (nc - 1)
        last += [0] * (nc - 1) + [1]
        base += nc
    n_steps = base
    rowblk = jnp.asarray(np.array(rowblk, np.int32).reshape(-1))
    tables = (rowblk, jnp.asarray(seq_id, I32), jnp.asarray(first, I32), jnp.asarray(last, I32))

    def row_map(colblk):
        return lambda d, s, rb, sq, fi, la: (rb[d * n_steps + s], colblk)

    def state_map(nd):
        return lambda d, s, rb, sq, fi, la: (sq[s], d) + (0,) * nd

    in_specs = [pl.BlockSpec((L, H * DK), row_map(0)),
                pl.BlockSpec((L, H * DK), row_map(1)),
                pl.BlockSpec((L, H * DV), row_map(1)),
                pl.BlockSpec((1, 2, L, LANES), lambda d, s, rb, sq, fi, la: (d, 0, rb[d * n_steps + s], 0)),
                pl.BlockSpec((1, 2, 1, LANES), lambda d, s, rb, sq, fi, la: (d, 0, 0, 0)),
                pl.BlockSpec((1, 1, H, DK, DS), state_map(3)),
                pl.BlockSpec((1, 1, H, LANES), state_map(2))]
    out_specs = (pl.BlockSpec((None, L, H * DV), lambda d, s, rb, sq, fi, la: (d, rb[d * n_steps + s], 0)),
                 pl.BlockSpec((1, 1, H, DK, DS), state_map(3)),
                 pl.BlockSpec((1, 1, H, LANES), state_map(2)))
    out_shape = (jax.ShapeDtypeStruct((2, N, H * DV), F32),
                 jax.ShapeDtypeStruct((S, 2, H, DK, DS), F32),
                 jax.ShapeDtypeStruct((S, 2, H, LANES), F32))
    return pl.pallas_call(
        _mlstm_kernel,
        out_shape=out_shape,
        grid_spec=pltpu.PrefetchScalarGridSpec(
            num_scalar_prefetch=4,
            grid=(2, n_steps),
            in_specs=in_specs,
            out_specs=out_specs,
            scratch_shapes=[pltpu.VMEM((H, DK, DS), F32), pltpu.VMEM((H, LANES), F32),
                            pltpu.VMEM((H, L, L), F32), pltpu.VMEM((H, L, DS + LANES), F32),
                            pltpu.VMEM((H, L, DV), F32)]),
        compiler_params=_cparams(("arbitrary", "arbitrary"), 40),
        name="mlstm_bidir",
    )(*tables, proj, proj, proj, gates_dir, b_gates_dir, S0, m0)


def _top2_of4(vals):
    m1, i1 = vals[0], jnp.zeros(vals[0].shape, I32)
    for j in range(1, 4):
        better = vals[j] > m1
        m1 = jnp.where(better, vals[j], m1)
        i1 = jnp.where(better, j, i1)
    m2, i2 = jnp.full(vals[0].shape, -jnp.inf, F32), jnp.zeros(vals[0].shape, I32)
    for j in range(4):
        cand = jnp.where(i1 == j, -jnp.inf, vals[j])
        better = cand > m2
        m2 = jnp.where(better, cand, m2)
        i2 = jnp.where(better, j, i2)
    return m1, i1, m2, i2


def _router_kernel(x_ref, g_ref, mod_ref, wr_ref, br_ref, h_ref, ids_ref, wts_ref, *, shift_idx, scale_idx):
    h = _modulated_norm(x_ref[...], g_ref[...], mod_ref[0], shift_idx, scale_idx)
    h_hi = h.astype(BF16)
    h_ref[...] = h_hi
    h_lo = (h - h_hi.astype(F32)).astype(BF16)
    w_hi = wr_ref[0]
    w_lo = wr_ref[1]
    logits = (lax.dot_general(w_hi, h_hi, _NT, preferred_element_type=F32)
              + lax.dot_general(w_lo, h_hi, _NT, preferred_element_type=F32)
              + lax.dot_general(w_hi, h_lo, _NT, preferred_element_type=F32))
    aff = jax.nn.sigmoid(logits)
    sel = aff + br_ref[...]
    aff_rows = [aff[e:e + 1, :] for e in range(N_EXPERTS)]
    sel_rows = [sel[e:e + 1, :] for e in range(N_EXPERTS)]
    tops = [_top2_of4(sel_rows[4 * gidx:4 * gidx + 4]) for gidx in range(N_GROUPS)]
    best = tops[0][0] + tops[0][2]
    grp = jnp.zeros(best.shape, I32)
    i1, i2 = tops[0][1], tops[0][3]
    for gidx in range(1, N_GROUPS):
        score = tops[gidx][0] + tops[gidx][2]
        better = score > best
        best = jnp.where(better, score, best)
        grp = jnp.where(better, gidx, grp)
        i1 = jnp.where(better, tops[gidx][1], i1)
        i2 = jnp.where(better, tops[gidx][3], i2)
    e1 = grp * EXPERTS_PER_GROUP + i1
    e2 = grp * EXPERTS_PER_GROUP + i2
    w1 = jnp.zeros(best.shape, F32)
    w2 = jnp.zeros(best.shape, F32)
    for e in range(N_EXPERTS):
        w1 = jnp.where(e1 == e, aff_rows[e], w1)
        w2 = jnp.where(e2 == e, aff_rows[e], w2)
    tot = w1 + w2
    ids_ref[...] = jnp.zeros(ids_ref.shape, I32)
    wts_ref[...] = jnp.zeros(wts_ref.shape, F32)
    ids_ref[0:1, :] = e1
    ids_ref[1:2, :] = e2
    wts_ref[0:1, :] = w1 / tot
    wts_ref[1:2, :] = w2 / tot


def router_weight_pair(w_router):
    w_t = w_router.T.astype(F32)
    w_hi = w_t.astype(BF16)
    return jnp.stack([w_hi, (w_t - w_hi.astype(F32)).astype(BF16)], axis=0)


def moe_router(x, row_off, n_rows, g, mod, w_router_t, b_router, shift_idx, scale_idx, tm=512):
    D = x.shape[1]
    off_b = row_off // tm
    per_seg = SEG_ROWS // tm
    return pl.pallas_call(
        functools.partial(_router_kernel, shift_idx=shift_idx, scale_idx=scale_idx),
        out_shape=(jax.ShapeDtypeStruct((n_rows, D), BF16),
                   jax.ShapeDtypeStruct((8, n_rows), I32),
                   jax.ShapeDtypeStruct((8, n_rows), F32)),
        grid=(n_rows // tm,),
        in_specs=[pl.BlockSpec((tm, D), lambda i: (i + off_b, 0)),
                  pl.BlockSpec((1, D), lambda i: (0, 0)),
                  pl.BlockSpec((1, 6, D), lambda i: ((i + off_b) // per_seg, 0, 0)),
                  pl.BlockSpec((2, N_EXPERTS, D), lambda i: (0, 0, 0)),
                  pl.BlockSpec((N_EXPERTS, 1), lambda i: (0, 0))],
        out_specs=(pl.BlockSpec((tm, D), lambda i: (i, 0)),
                   pl.BlockSpec((8, tm), lambda i: (0, i)),
                   pl.BlockSpec((8, tm), lambda i: (0, i))),
        compiler_params=_cparams(("parallel",), 40),
        name="moe_router",
    )(x, g.reshape(1, D), mod, w_router_t, b_router.reshape(N_EXPERTS, 1).astype(F32))


MOE_CAST_ROWS = 128


def _moe_expert_kernel(be_ref, first_ref, next_ref, nu_ref, x_ref, wg_hbm, wu_hbm, wd_hbm, y_ref,
                       wg_st, wu_st, wd_st, wg_bf, wu_bf, wd_bf, sems, *, e_off):
    i = pl.program_id(0)

    def weight_copies(e):
        return (pltpu.make_async_copy(wg_hbm.at[e_off + e], wg_st, sems.at[0]),
                pltpu.make_async_copy(wu_hbm.at[e_off + e], wu_st, sems.at[1]),
                pltpu.make_async_copy(wd_hbm.at[e_off + e], wd_st, sems.at[2]))

    @pl.when(i == 0)
    def _():
        for cp in weight_copies(be_ref[0]):
            cp.start()

    @pl.when(first_ref[i] == 1)
    def _():
        for cp in weight_copies(be_ref[i]):
            cp.wait()
        for st, bf in ((wg_st, wg_bf), (wu_st, wu_bf), (wd_st, wd_bf)):
            def cast_rows(c, carry, st=st, bf=bf):
                r = pl.multiple_of(c * MOE_CAST_ROWS, MOE_CAST_ROWS)
                bf[pl.ds(r, MOE_CAST_ROWS), :] = st[pl.ds(r, MOE_CAST_ROWS), :].astype(BF16)
                return carry
            lax.fori_loop(0, st.shape[0] // MOE_CAST_ROWS, cast_rows, 0)

        @pl.when(next_ref[i] >= 0)
        def _():
            for cp in weight_copies(next_ref[i]):
                cp.start()

    @pl.when(i < nu_ref[0])
    def _():
        x = x_ref[...]
        gate = jnp.dot(x, wg_bf[...], preferred_element_type=F32)
        up = jnp.dot(x, wu_bf[...], preferred_element_type=F32)
        act = (gate * jax.nn.sigmoid(gate) * up).astype(BF16)
        y_ref[...] = jnp.dot(act, wd_bf[...], preferred_element_type=F32).astype(y_ref.dtype)

    @pl.when(i >= nu_ref[0])
    def _():
        y_ref[...] = jnp.zeros(y_ref.shape, y_ref.dtype)


def moe_experts(xg, plan, wg, wu, wd, e_off):
    R, D = xg.shape
    F = wg.shape[2]
    nb = R // MOE_ROWS
    hbm = pl.BlockSpec(memory_space=pl.ANY)
    return pl.pallas_call(
        functools.partial(_moe_expert_kernel, e_off=e_off),
        out_shape=jax.ShapeDtypeStruct((R, D), BF16),
        grid_spec=pltpu.PrefetchScalarGridSpec(
            num_scalar_prefetch=4,
            grid=(nb,),
            in_specs=[pl.BlockSpec((MOE_ROWS, D), lambda i, *_: (i, 0)), hbm, hbm, hbm],
            out_specs=pl.BlockSpec((MOE_ROWS, D), lambda i, *_: (i, 0)),
            scratch_shapes=[pltpu.VMEM((D, F), F32), pltpu.VMEM((D, F), F32), pltpu.VMEM((F, D), F32),
                            pltpu.VMEM((D, F), BF16), pltpu.VMEM((D, F), BF16), pltpu.VMEM((F, D), BF16),
                            pltpu.SemaphoreType.DMA((3,))]),
        compiler_params=_cparams(("arbitrary",), 56),
        name="moe_experts",
    )(plan["block_e"], plan["first"], plan["next_e"], plan["n_used"], xg, wg, wu, wd)


def _combine_kernel(x_ref, mod_ref, *rest, gate_idx, final, tile_off, chunk_tiles):
    rest = list(rest)
    fg_ref = rest.pop(0) if final else None
    o_ref = rest.pop()
    t = pl.program_id(0) + tile_off
    for c, (lo, n) in enumerate(chunk_tiles):
        y0_ref, y1_ref, w_ref = rest[3 * c:3 * c + 3]

        @pl.when((t >= lo) & (t < lo + n))
        def _(y0_ref=y0_ref, y1_ref=y1_ref, w_ref=w_ref):
            y = w_ref[:, 0:1] * y0_ref[...].astype(F32) + w_ref[:, 1:2] * y1_ref[...].astype(F32)
            x = x_ref[...] + mod_ref[0, gate_idx:gate_idx + 1, :] * y
            if final:
                ms = jnp.mean(x * x, axis=-1, keepdims=True)
                x = x * lax.rsqrt(ms + EPS) * fg_ref[...]
            o_ref[...] = x


def moe_combine(x, row_off, n_rows, parts, mod, gate_idx, final_g=None, tm=512):
    D = x.shape[1]
    xb = row_off // tm
    per_seg = SEG_ROWS // tm
    final = final_g is not None
    in_specs = [pl.BlockSpec((tm, D), lambda i: (i + xb, 0)),
                pl.BlockSpec((1, 6, D), lambda i: ((i + xb) // per_seg, 0, 0))]
    args = [x, mod]
    if final:
        in_specs.append(pl.BlockSpec((1, D), lambda i: (0, 0)))
        args.append(final_g.reshape(1, D))
    chunk_tiles = []
    for c_off, c_n, y0, y1, wts in parts:
        lo, n = c_off // tm, c_n // tm
        chunk_tiles.append((lo, n))

        def chunk_map(i, lo=lo, n=n):
            return (jnp.clip(i + xb - lo, 0, n - 1), 0)

        in_specs += [pl.BlockSpec((tm, D), chunk_map), pl.BlockSpec((tm, D), chunk_map),
                     pl.BlockSpec((tm, 2), chunk_map)]
        args += [y0, y1, wts]
    return pl.pallas_call(
        functools.partial(_combine_kernel, gate_idx=gate_idx, final=final, tile_off=xb,
                          chunk_tiles=tuple(chunk_tiles)),
        out_shape=jax.ShapeDtypeStruct((n_rows, D), F32),
        grid=(n_rows // tm,),
        in_specs=in_specs,
        out_specs=pl.BlockSpec((tm, D), lambda i: (i, 0)),
        compiler_params=_cparams(("arbitrary",), 48),
        name="moe_combine",
    )(*args)


def moe_dispatch_plan(ids, n_tokens):
    n_assign = 2 * n_tokens
    n_blocks = n_assign // MOE_ROWS + N_EXPERTS
    experts = jnp.arange(N_EXPERTS, dtype=I32)
    flat_e = ids.T.reshape(n_assign)
    onehot = (flat_e[:, None] == experts[None, :]).astype(I32)
    csum = jnp.cumsum(onehot, axis=0)
    rank = jnp.sum((csum - onehot) * onehot, axis=1)
    counts = csum[-1]
    padded = (counts + MOE_ROWS - 1) // MOE_ROWS * MOE_ROWS
    pend = jnp.cumsum(padded)
    dest = (pend - padded)[flat_e] + rank
    n_used = (pend[-1] // MOE_ROWS).astype(I32)
    block_idx = jnp.arange(n_blocks, dtype=I32)
    block_e = jnp.minimum(jnp.sum((pend[None, :] <= block_idx[:, None] * MOE_ROWS).astype(I32), axis=1),
                          N_EXPERTS - 1)
    prev_e = jnp.concatenate([jnp.full((1,), -1, I32), block_e[:-1]])
    first = ((block_idx < n_used) & (block_e != prev_e)).astype(I32)
    later = (experts[None, :] > experts[:, None]) & (counts[None, :] > 0)
    next_of = jnp.min(jnp.where(later, experts[None, :], N_EXPERTS), axis=1)
    next_of = jnp.where(next_of == N_EXPERTS, -1, next_of)
    next_e = jnp.sum(jnp.where(block_e[:, None] == experts[None, :], next_of[None, :], 0), axis=1).astype(I32)
    src_tok = jnp.zeros((n_blocks * MOE_ROWS,), I32).at[dest].set(
        jnp.arange(n_assign, dtype=I32) // 2, unique_indices=True, mode="promise_in_bounds")
    plan = {"block_e": block_e, "first": first, "next_e": next_e, "n_used": n_used.reshape(1)}
    return dest.reshape(n_tokens, 2), src_tok, plan


def channel_mixer(x, g, mod, w_router_t, b_router, wg, wu, wd, e_off, chunks, outputs, final_g=None):
    h, ids8, wts8 = moe_router(x, 0, x.shape[0], g, mod, w_router_t, b_router, shift_idx=3, scale_idx=4)
    parts = []
    for c_off, c_n in chunks:
        dest, src_tok, plan = moe_dispatch_plan(ids8[:2, c_off:c_off + c_n], c_n)
        xg = h.at[src_tok + c_off].get(mode="promise_in_bounds")
        ybuf = moe_experts(xg, plan, wg, wu, wd, e_off)
        y0 = ybuf.at[dest[:, 0]].get(mode="promise_in_bounds")
        y1 = ybuf.at[dest[:, 1]].get(mode="promise_in_bounds")
        parts.append((c_off, c_n, y0, y1, wts8[:2, c_off:c_off + c_n].T))
    return [moe_combine(x, o_off, o_n, parts, mod, 5, final_g) for o_off, o_n in outputs]


def kernel(x_prompt, x_sample, c, cache_a_k, cache_a_v, cache_b_k, cache_b_v, state_C, state_n, state_m,
           c_ctx, norm1_g, norm2_g, w_ada, b_ada, w_in_ab, w_out_ab, rel_bias_a, sink_b, w_in_c, b_gates_c,
           norm_c_g, w_out_c, w_router, b_router, w_gate_e, w_up_e, w_down_e, final_norm_g):
    D = D_MODEL
    Np, Nl = BATCH * SEQ, DEC_BATCH * DEC_SEQ
    N = Np + Nl
    H = C_HEADS
    x = None
    x_ctx, x_lat = x_prompt.reshape(Np, D), x_sample.reshape(Nl, D)
    moe_chunks = [(0, N)]
    n_e = w_gate_e.shape[1]
    wg_all = w_gate_e.reshape(DEPTH * n_e, D, D_FF_EXPERT)
    wu_all = w_up_e.reshape(DEPTH * n_e, D, D_FF_EXPERT)
    wd_all = w_down_e.reshape(DEPTH * n_e, D_FF_EXPERT, D)

    cvec = jnp.concatenate([c_ctx[None, :], c, jnp.zeros((8 - 1 - DEC_BATCH, D), F32)], axis=0)
    mod_all = ada_mod_all(cvec, w_ada, b_ada).reshape(DEPTH, 8, 6, D)
    w_router_t = router_weight_pair(w_router)

    outs = {}
    for l in range(DEPTH):
        mod = mod_all[l]
        j = l // 2
        if l % 2 == 0:
            w_in = w_in_ab[j]
            if x is not None:
                x_ctx, x_lat = x[:Np], x[Np:]
            proj_ctx = norm_mod_matmul(x_ctx, 0, Np, norm1_g[l], mod, w_in, F32, 0, 1, seg_row_off=0,
                                       precise=True, tm=512)
            proj_lat = norm_mod_matmul(x_lat, 0, Nl, norm1_g[l], mod, w_in, BF16, 0, 1, seg_row_off=Np, tn=768)
            o_ctx = ctx_attention(proj_ctx, sink_b[j], BATCH, SEQ)
            bias_mask = nat_bias_mask(rel_bias_a[j], DEC_SEQ // GRID_W)
            oa = nat_attention(proj_lat, cache_a_k[:, j].reshape(DEC_BATCH, PAST_LEN, A_WIDTH).astype(BF16),
                               cache_a_v[:, j].reshape(DEC_BATCH, PAST_LEN, A_WIDTH).astype(BF16), bias_mask,
                               DEC_BATCH, DEC_SEQ, A_HEADS)
            cos, sin = rope_tables(DEC_SEQ)
            qk_rot = rope_heads(proj_lat, 3 * A_WIDTH, 3 * A_WIDTH + B_Q_WIDTH, cos, sin, DEC_SEQ)
            ob = swa_attention(qk_rot, proj_lat, 3 * A_HEADS + B_HEADS + B_KV_HEADS,
                               cache_b_k[:, j].reshape(DEC_BATCH, PAST_LEN, B_KV_WIDTH).astype(BF16),
                               cache_b_v[:, j].reshape(DEC_BATCH, PAST_LEN, B_KV_WIDTH).astype(BF16), sink_b[j],
                               DEC_BATCH, DEC_SEQ)
            x = attn_out_residual(o_ctx, oa, ob, jnp.stack(_hi_lo(w_out_ab[j]), axis=0), x_ctx, x_lat, mod,
                                  gate_idx=2)
            kv = proj_ctx[:, A_WIDTH:]
            outs.setdefault("ak", []).append(kv[:, :A_WIDTH].reshape(BATCH, SEQ, A_HEADS, HEAD_DIM))
            outs.setdefault("av", []).append(kv[:, A_WIDTH:2 * A_WIDTH].reshape(BATCH, SEQ, A_HEADS, HEAD_DIM))
            kvb = proj_ctx[:, 3 * A_WIDTH + B_Q_WIDTH:]
            outs.setdefault("bk", []).append(kvb[:, :B_KV_WIDTH].reshape(BATCH, SEQ, B_KV_HEADS, HEAD_DIM))
            outs.setdefault("bv", []).append(kvb[:, B_KV_WIDTH:].reshape(BATCH, SEQ, B_KV_HEADS, HEAD_DIM))
        else:
            n_main = 2 * C_QK_WIDTH + 2 * C_V_WIDTH
            w_main = w_in_c[j]
            w_gates = jnp.pad(w_in_c[j][:, n_main:], ((0, 0), (0, LANES - 4 * H)))
            proj, gates = norm_mod_matmul(x, 0, N, norm1_g[l], mod, w_main, BF16, 0, 1, w_aux=w_gates,
                                          n_out=n_main, tn=1024)
            gates_dir = jnp.pad(gates[:, :4 * H].reshape(N, 2, 2, H).transpose(1, 2, 0, 3),
                                ((0, 0), (0, 0), (0, 0), (0, LANES - H)))
            bg = jnp.pad(b_gates_c[j].astype(F32).reshape(2, 2, 1, H), ((0, 0), (0, 0), (0, 0), (0, LANES - H)))
            S = BATCH + DEC_BATCH
            C0 = jnp.concatenate([jnp.zeros((BATCH, 2, H, C_QK_DIM, C_V_DIM), F32), state_C[:, j]], axis=0)
            n0 = jnp.concatenate([jnp.zeros((BATCH, 2, H, C_QK_DIM), F32), state_n[:, j]], axis=0)
            m0 = jnp.concatenate([jnp.zeros((BATCH, 2, H), F32), state_m[:, j]], axis=0)
            S0 = jnp.concatenate([C0, jnp.broadcast_to(n0[..., None], n0.shape + (LANES,))], axis=-1)
            m0 = jnp.broadcast_to(m0[..., None], (S, 2, H, LANES))
            seq_chunks = [SEQ // C_CHUNK] * BATCH + [DEC_SEQ // C_CHUNK] * DEC_BATCH
            h_dir, Sf, mf = mlstm_bidir(proj, gates_dir, bg, S0, m0, seq_chunks)
            x = mlstm_out_residual(h_dir, proj, 2, norm_c_g[j], w_out_c[j].astype(BF16), x, mod, gate_idx=2)
            outs.setdefault("C", []).append(Sf[:BATCH, ..., :C_V_DIM])
            outs.setdefault("n", []).append(Sf[:BATCH, ..., C_V_DIM])
            outs.setdefault("m", []).append(mf[:BATCH, :, :, 0])
        last = l == DEPTH - 1
        pieces = channel_mixer(x, norm2_g[l], mod, w_router_t, b_router, wg_all, wu_all, wd_all, l * n_e,
                               moe_chunks, [(0, Np), (Np, Nl)] if last else [(0, N)],
                               final_norm_g if last else None)
        if last:
            y_prompt = pieces[0].reshape(BATCH, SEQ, D)
            y_sample = pieces[1].reshape(DEC_BATCH, DEC_SEQ, D)
        else:
            x = pieces[0]

    return (y_prompt, y_sample,
            jnp.stack(outs["ak"], axis=1), jnp.stack(outs["av"], axis=1),
            jnp.stack(outs["bk"], axis=1), jnp.stack(outs["bv"], axis=1),
            jnp.stack(outs["C"], axis=1), jnp.stack(outs["n"], axis=1), jnp.stack(outs["m"], axis=1))
```
